```python
import math
import jax, jax.numpy as jnp
from jax import lax
import numpy as np

D_MODEL = 1024
BATCH = 32
SEQ = 2048
DEPTH = 1

F32 = jnp.float32
EPS = 1e-6
N_MOD = 6

ATTN_HEADS = 8
ATTN_HEAD_DIM = 64
ATTN_DIM = ATTN_HEADS * ATTN_HEAD_DIM
Q_BLOCK = 128

SSD_HEADS = 8
SSD_HEAD_DIM = 64
SSD_DIM = SSD_HEADS * SSD_HEAD_DIM
SSD_GROUPS = 2
SSD_STATE = 128
SSD_CONV = 4
SSD_CHUNK = 128
SSD_CONV_DIM = SSD_DIM + 2 * SSD_GROUPS * SSD_STATE

IN_PROJ_DIM = 3 * ATTN_DIM + ATTN_HEADS + SSD_DIM + SSD_CONV_DIM + SSD_HEADS

N_EXPERTS = 256
TOP_K = 8
N_ROUTE_GROUPS = 8
TOPK_ROUTE_GROUPS = 4
EXPERT_DIM = 256
ROUTED_SCALE = 2.5
MOE_BLOCK = 128

kernel_name = "hybrid_fox_ssd_moe_adaln"


def rmsnorm(x, g):
    xf = x.astype(F32)
    y = xf * lax.rsqrt(jnp.mean(xf * xf, axis=-1, keepdims=True) + EPS)
    return (y * g.astype(F32)).astype(x.dtype)


def forgetting_attention(q, k, v, fg_logit, fg_bias):
    b, L, H, Dh = q.shape
    log_f = jax.nn.log_sigmoid(fg_logit.astype(F32) + fg_bias.astype(F32))
    cum = jnp.cumsum(log_f, axis=1).transpose(0, 2, 1)
    scale = Dh ** -0.5
    outs = []
    for i in range(L // Q_BLOCK):
        q0, q1 = i * Q_BLOCK, (i + 1) * Q_BLOCK
        s = jnp.einsum("bqhd,bkhd->bhqk", q[:, q0:q1], k[:, :q1], preferred_element_type=F32) * scale
        s = s + cum[:, :, q0:q1, None] - cum[:, :, None, :q1]
        causal = jnp.arange(q0, q1)[:, None] >= jnp.arange(q1)[None, :]
        s = jnp.where(causal, s, -jnp.inf)
        p = jax.nn.softmax(s, axis=-1).astype(v.dtype)
        outs.append(jnp.einsum("bhqk,bkhd->bqhd", p, v[:, :q1]))
    return jnp.concatenate(outs, axis=1)


def causal_depthwise_conv(x, w, bias):
    K, C = w.shape
    y = lax.conv_general_dilated(x, w[:, None, :], window_strides=(1,), padding=[(K - 1, 0)],
                                 dimension_numbers=("NWC", "WIO", "NWC"), feature_group_count=C)
    return y + bias


def segsum(x):
    T = x.shape[-1]
    xx = jnp.broadcast_to(x[..., :, None], x.shape + (T,))
    xx = jnp.where(jnp.tril(jnp.ones((T, T), bool), -1), xx, 0.0)
    cs = jnp.cumsum(xx, axis=-2)
    return jnp.where(jnp.tril(jnp.ones((T, T), bool)), cs, -jnp.inf)


def ssd_chunked(xs, dt, A, Bm, Cm):
    b, L, H, P = xs.shape
    G, N = Bm.shape[-2:]
    R = H // G
    nc, Q = L // SSD_CHUNK, SSD_CHUNK
    X = (xs * dt[..., None]).reshape(b, nc, Q, G, R, P)
    Adt = (A * dt).reshape(b, nc, Q, G, R).transpose(0, 3, 4, 1, 2)
    Bc = Bm.reshape(b, nc, Q, G, N)
    Cc = Cm.reshape(b, nc, Q, G, N)
    A_cs = jnp.cumsum(Adt, axis=-1)
    Lmat = jnp.exp(segsum(Adt))
    y_diag = jnp.einsum("bclgn,bcsgn,bgrcls,bcsgrp->bclgrp", Cc, Bc, Lmat, X)
    decay_states = jnp.exp(A_cs[..., -1:] - A_cs)
    states = jnp.einsum("bclgn,bgrcl,bclgrp->bcgrpn", Bc, decay_states, X)
    states = jnp.concatenate([jnp.zeros_like(states[:, :1]), states], axis=1)
    chunk_decay = jnp.exp(segsum(jnp.pad(A_cs[..., -1], ((0, 0), (0, 0), (0, 0), (1, 0)))))
    states = jnp.einsum("bgrzc,bcgrpn->bzgrpn", chunk_decay, states)[:, :-1]
    y_off = jnp.einsum("bclgn,bcgrpn,bgrcl->bclgrp", Cc, states, jnp.exp(A_cs))
    return (y_diag + y_off).reshape(b, L, H, P)


def hybrid_mixer(h, w_in, fg_bias, conv_w, conv_b, dt_bias, a_log, d_skip, attn_norm_g, ssd_norm_g, w_out):
    b, L, _ = h.shape
    proj = jnp.einsum("bld,de->ble", h, w_in)
    sizes = (ATTN_DIM, ATTN_DIM, ATTN_DIM, ATTN_HEADS, SSD_DIM, SSD_CONV_DIM, SSD_HEADS)
    q, k, v, fg, z, xbc, dt = jnp.split(proj, np.cumsum(sizes)[:-1].tolist(), axis=-1)
    hs = (b, L, ATTN_HEADS, ATTN_HEAD_DIM)
    attn = forgetting_attention(q.reshape(hs), k.reshape(hs), v.reshape(hs), fg, fg_bias)
    attn = rmsnorm(attn, attn_norm_g.reshape(ATTN_HEADS, ATTN_HEAD_DIM)).reshape(b, L, ATTN_DIM)
    xbc = jax.nn.silu(causal_depthwise_conv(xbc, conv_w, conv_b))
    xs, bm, cm = jnp.split(xbc, [SSD_DIM, SSD_DIM + SSD_GROUPS * SSD_STATE], axis=-1)
    xs = xs.reshape(b, L, SSD_HEADS, SSD_HEAD_DIM).astype(F32)
    dt = jax.nn.softplus(dt.astype(F32) + dt_bias.astype(F32))
    A = -jnp.exp(a_log.astype(F32))
    y = ssd_chunked(xs, dt, A,
                    bm.reshape(b, L, SSD_GROUPS, SSD_STATE).astype(F32),
                    cm.reshape(b, L, SSD_GROUPS, SSD_STATE).astype(F32))
    y = y + d_skip.astype(F32)[:, None] * xs
    y = y.reshape(b, L, SSD_DIM) * jax.nn.silu(z.astype(F32))
    y = rmsnorm(y.reshape(b, L, SSD_GROUPS, SSD_DIM // SSD_GROUPS),
                ssd_norm_g.reshape(SSD_GROUPS, SSD_DIM // SSD_GROUPS)).reshape(b, L, SSD_DIM)
    mixed = jnp.concatenate([attn.astype(h.dtype), y.astype(h.dtype)], axis=-1)
    return jnp.einsum("ble,ed->bld", mixed, w_out)


def route(h2, w_router, router_bias):
    T = h2.shape[0]
    scores = jax.nn.sigmoid(jnp.einsum("td,de->te", h2, w_router, preferred_element_type=F32))
    biased = scores + router_bias.astype(F32)
    grp = biased.reshape(T, N_ROUTE_GROUPS, N_EXPERTS // N_ROUTE_GROUPS)
    grp_score = lax.top_k(grp, 2)[0].sum(-1)
    _, top_groups = lax.top_k(grp_score, TOPK_ROUTE_GROUPS)
    group_mask = jnp.any(top_groups[..., None] == jnp.arange(N_ROUTE_GROUPS), axis=-2)
    expert_mask = jnp.repeat(group_mask, N_EXPERTS // N_ROUTE_GROUPS, axis=-1)
    _, idx = lax.top_k(jnp.where(expert_mask, biased, -jnp.inf), TOP_K)
    w = jnp.take_along_axis(scores, idx, axis=-1)
    w = w / jnp.sum(w, axis=-1, keepdims=True) * ROUTED_SCALE
    return idx, w


def swiglu(x, wg, wu, wd):
    return (jax.nn.silu(x @ wg) * (x @ wu)) @ wd


def moe_dispatch(h2, idx, wts, w_gate_e, w_up_e, w_down_e):
    T, D = h2.shape
    A = T * TOP_K
    expert_flat = idx.reshape(A)
    token_flat = jnp.repeat(jnp.arange(T, dtype=jnp.int32), TOP_K)
    weight_flat = wts.reshape(A)
    order = jnp.argsort(expert_flat)
    e_sorted = expert_flat[order]
    counts = jnp.bincount(expert_flat, length=N_EXPERTS)
    padded = (counts + MOE_BLOCK - 1) // MOE_BLOCK * MOE_BLOCK
    start = jnp.cumsum(counts) - counts
    pend = jnp.cumsum(padded)
    pstart = pend - padded
    dest = pstart[e_sorted] + (jnp.arange(A, dtype=jnp.int32) - start[e_sorted])
    n_blocks = (A + MOE_BLOCK - 1) // MOE_BLOCK + N_EXPERTS
    n_rows = n_blocks * MOE_BLOCK
    row_token = jnp.full((n_rows,), T, jnp.int32).at[dest].set(token_flat[order])
    row_weight = jnp.zeros((n_rows,), F32).at[dest].set(weight_flat[order])
    block_start = jnp.arange(n_blocks, dtype=jnp.int32) * MOE_BLOCK
    block_expert = jnp.minimum(jnp.searchsorted(pend, block_start, side="right"), N_EXPERTS - 1)
    h_pad = jnp.concatenate([h2, jnp.zeros((1, D), h2.dtype)], axis=0)

    def step(acc, blk):
        tok, wt, e = blk
        y = swiglu(h_pad[tok], w_gate_e[e], w_up_e[e], w_down_e[e])
        return acc.at[tok].add(y * wt[:, None].astype(y.dtype)), None

    acc, _ = lax.scan(step, jnp.zeros((T + 1, D), h2.dtype),
                      (row_token.reshape(n_blocks, MOE_BLOCK), row_weight.reshape(n_blocks, MOE_BLOCK), block_expert))
    return acc[:T]


def moe_ffn(h, w_router, router_bias, w_gate_e, w_up_e, w_down_e, w_gate_s, w_up_s, w_down_s):
    h2 = h.reshape(-1, h.shape[-1])
    idx, wts = route(h2, w_router, router_bias)
    routed = moe_dispatch(h2, idx, wts, w_gate_e, w_up_e, w_down_e)
    shared = swiglu(h2, w_gate_s, w_up_s, w_down_s)
    return (routed + shared).reshape(h.shape)


def setup_inputs(seed: int = 0) -> dict:
    key = jax.random.key(seed)
    ks = jax.random.split(key, 27)
    D, E, F = D_MODEL, N_EXPERTS, EXPERT_DIM

    def nrm(k, shape, s):
        return jax.random.normal(k, shape, F32) * s

    def gain(k, shape):
        return 1.0 + 0.1 * jax.random.normal(k, shape, F32)

    dt0 = jnp.exp(jax.random.uniform(ks[9], (DEPTH, SSD_HEADS), F32, math.log(1e-3), math.log(1e-1)))
    return {
        "x": nrm(ks[0], (BATCH, SEQ, D), 1.0),
        "c": nrm(ks[1], (BATCH, D), 1.0),
        "norm1_g": gain(ks[2], (DEPTH, D)),
        "w_ada": nrm(ks[3], (DEPTH, D, N_MOD * D), 0.5 * D ** -0.5),
        "b_ada": nrm(ks[4], (DEPTH, N_MOD * D), 0.01),
        "w_in": nrm(ks[5], (DEPTH, D, IN_PROJ_DIM), D ** -0.5),
        "fg_bias": jax.random.uniform(ks[6], (DEPTH, ATTN_HEADS), F32, 1.0, 4.0),
        "conv_w": nrm(ks[7], (DEPTH, SSD_CONV, SSD_CONV_DIM), SSD_CONV ** -0.5),
        "conv_b": nrm(ks[8], (DEPTH, SSD_CONV_DIM), 0.01),
        "dt_bias": dt0 + jnp.log(-jnp.expm1(-dt0)),
        "a_log": jnp.log(jax.random.uniform(ks[10], (DEPTH, SSD_HEADS), F32, 1.0, 16.0)),
        "d_skip": gain(ks[11], (DEPTH, SSD_HEADS)),
        "attn_norm_g": gain(ks[12], (DEPTH, ATTN_DIM)),
        "ssd_norm_g": gain(ks[13], (DEPTH, SSD_DIM)),
        "w_out": nrm(ks[14], (DEPTH, D, D), D ** -0.5),
        "norm2_g": gain(ks[15], (DEPTH, D)),
        "w_router": nrm(ks[16], (DEPTH, D, E), D ** -0.5),
        "router_bias": nrm(ks[17], (DEPTH, E), 0.01),
        "w_gate_e": nrm(ks[18], (DEPTH, E, D, F), D ** -0.5),
        "w_up_e": nrm(ks[19], (DEPTH, E, D, F), D ** -0.5),
        "w_down_e": nrm(ks[20], (DEPTH, E, F, D), F ** -0.5),
        "w_gate_s": nrm(ks[21], (DEPTH, D, F), D ** -0.5),
        "w_up_s": nrm(ks[22], (DEPTH, D, F), D ** -0.5),
        "w_down_s": nrm(ks[23], (DEPTH, F, D), F ** -0.5),
        "normf_g": gain(ks[24], (D,)),
        "w_ada_f": nrm(ks[25], (D, 2 * D), 0.5 * D ** -0.5),
        "b_ada_f": nrm(ks[26], (2 * D,), 0.01),
    }


def reference(x, c, norm1_g, w_ada, b_ada, w_in, fg_bias, conv_w, conv_b, dt_bias, a_log, d_skip,
              attn_norm_g, ssd_norm_g, w_out, norm2_g, w_router, router_bias,
              w_gate_e, w_up_e, w_down_e, w_gate_s, w_up_s, w_down_s, normf_g, w_ada_f, b_ada_f):
    b, L, D = x.shape
    c_act = jax.nn.silu(c)
    for l in range(DEPTH):
        mod = (jnp.einsum("bd,de->be", c_act, w_ada[l]) + b_ada[l]).reshape(b, N_MOD, D)[:, :, None, :]
        shift1, scale1, gate1, shift2, scale2, gate2 = (mod[:, i] for i in range(N_MOD))
        h = rmsnorm(x, norm1_g[l]) * (1 + scale1) + shift1
        x = x + gate1 * hybrid_mixer(h, w_in[l], fg_bias[l], conv_w[l], conv_b[l], dt_bias[l], a_log[l],
                                     d_skip[l], attn_norm_g[l], ssd_norm_g[l], w_out[l])
        h = rmsnorm(x, norm2_g[l]) * (1 + scale2) + shift2
        x = x + gate2 * moe_ffn(h, w_router[l], router_bias[l], w_gate_e[l], w_up_e[l], w_down_e[l],
                                w_gate_s[l], w_up_s[l], w_down_s[l])
    modf = (jnp.einsum("bd,de->be", c_act, w_ada_f) + b_ada_f).reshape(b, 2, D)[:, :, None, :]
    return rmsnorm(x, normf_g) * (1 + modf[:, 1]) + modf[:, 0]
```

```python
import functools

import jax
import jax.numpy as jnp
from jax import lax
from jax.experimental import pallas as pl
from jax.experimental.pallas import tpu as pltpu

F32 = jnp.float32
BF16 = jnp.bfloat16
I32 = jnp.int32
U32 = jnp.uint32

EPS = 1e-6
D_MODEL = 1024
N_MOD = 6

ATTN_HEADS = 8
ATTN_HEAD_DIM = 64
ATTN_DIM = ATTN_HEADS * ATTN_HEAD_DIM

SSD_HEADS = 8
SSD_HEAD_DIM = 64
SSD_DIM = SSD_HEADS * SSD_HEAD_DIM
SSD_GROUPS = 2
SSD_STATE = 128
SSD_CONV = 4
SSD_CHUNK = 128
SSD_CONV_DIM = SSD_DIM + 2 * SSD_GROUPS * SSD_STATE
SSD_GROUP_DIM = SSD_DIM // SSD_GROUPS

N_EXPERTS = 256
TOP_K = 8
N_ROUTE_GROUPS = 8
TOPK_ROUTE_GROUPS = 4
GROUP_SIZE = N_EXPERTS // N_ROUTE_GROUPS
EXPERT_DIM = 256
ROUTED_SCALE = 2.5

LANES = 128
SMALL_W = LANES
FG_LANE0 = 0
DT_LANE0 = 8
AUG_W = LANES

IN_TM = 512
PREP_TB = 256
ATT_TQ = 256
ATT_TK = 256
POST_TM = 512
DISP_TM = 512
EXP_BLK = 256
COMB_TM = 256
VMEM_LIMIT = 56 * 1024 * 1024
NEG_BIG = -1e30


def _split3(x):
    hi = x.astype(BF16)
    r1 = x - hi.astype(F32)
    mid = r1.astype(BF16)
    lo = (r1 - mid.astype(F32)).astype(BF16)
    return hi, mid, lo


def _dot(a, b):
    return jnp.dot(a, b, preferred_element_type=F32)


def _dot_nt(a, b):
    return lax.dot_general(a, b, (((1,), (1,)), ((), ())), preferred_element_type=F32)


def _dot_exact_lhs01(lhs_bf16, x, pieces=3):
    parts = _split3(x)[:pieces]
    out = _dot(lhs_bf16, parts[0])
    for p in parts[1:]:
        out = out + _dot(lhs_bf16, p)
    return out


def _dot_exact_rhs01(x, rhs_bf16, pieces=2):
    parts = _split3(x)[:pieces]
    out = _dot(parts[0], rhs_bf16)
    for p in parts[1:]:
        out = out + _dot(p, rhs_bf16)
    return out


def _sigmoid(x):
    return 1.0 / (1.0 + jnp.exp(-x))


def _silu(x):
    return x * _sigmoid(x)


def _softplus(x):
    return jnp.maximum(x, 0.0) + jnp.log(1.0 + jnp.exp(-jnp.abs(x)))


def _log_sigmoid(x):
    return jnp.minimum(x, 0.0) - jnp.log(1.0 + jnp.exp(-jnp.abs(x)))


def _pack_bf16_pair(x):
    n = x.shape[1] // 2
    lo = pltpu.bitcast(x[:, :n].astype(BF16).astype(F32), U32)
    hi = pltpu.bitcast(x[:, n:].astype(BF16).astype(F32), U32)
    return (hi & jnp.uint32(0xFFFF0000)) | (lo >> 16)


def _unpack_bf16_pair(w):
    lo = pltpu.bitcast(w << 16, F32)
    hi = pltpu.bitcast(w & jnp.uint32(0xFFFF0000), F32)
    return jnp.concatenate([lo, hi], axis=1)


def _cparams(sem):
    return pltpu.CompilerParams(dimension_semantics=sem, vmem_limit_bytes=VMEM_LIMIT)


def _mod_kernel(c_ref, w_ref, b_ref, o_ref):
    c = c_ref[...]
    o_ref[...] = jnp.dot(_silu(c), w_ref[...], preferred_element_type=F32,
                         precision=lax.Precision.HIGHEST) + b_ref[...]


def _modulation(c, w, b):
    bsz, d = c.shape
    n = w.shape[1]
    tn = 1024
    return pl.pallas_call(
        _mod_kernel,
        out_shape=jax.ShapeDtypeStruct((bsz, n), F32),
        grid=(n // tn,),
        in_specs=[pl.BlockSpec((bsz, d), lambda j: (0, 0)),
                  pl.BlockSpec((d, tn), lambda j: (0, j)),
                  pl.BlockSpec((1, tn), lambda j: (0, j))],
        out_specs=pl.BlockSpec((bsz, tn), lambda j: (0, j)),
        compiler_params=_cparams(("arbitrary",)),
        name="modulation",
    )(c, w, b.reshape(1, n))


_COL_Q, _COL_K, _COL_V, _COL_Z, _COL_XBC, _COL_SM, _COL_END = 0, 512, 1024, 1536, 2048, 3072, 3200


def _inproj_kernel(x_ref, g_ref, sc_ref, sh_ref, w_ref, q_ref, k_ref, v_ref, z_ref, xbc_ref, sm_ref):
    x = x_ref[...]
    var = jnp.mean(x * x, axis=-1, keepdims=True)
    h = x * lax.rsqrt(var + EPS) * g_ref[...]
    h = h * (1.0 + sc_ref[0]) + sh_ref[0]
    hb = h.astype(BF16)
    q_ref[...] = _dot(hb, w_ref[:, _COL_Q:_COL_K]).astype(BF16)
    k_ref[...] = _dot(hb, w_ref[:, _COL_K:_COL_V]).astype(BF16)
    v_ref[...] = _dot(hb, w_ref[:, _COL_V:_COL_Z]).astype(BF16)
    z_ref[...] = _dot(hb, w_ref[:, _COL_Z:_COL_XBC]).astype(BF16)
    xbc_ref[:, :512] = _dot(hb, w_ref[:, _COL_XBC:_COL_XBC + 512]).astype(BF16)
    xbc_ref[:, 512:] = _dot(hb, w_ref[:, _COL_XBC + 512:_COL_SM]).astype(BF16)
    sm_ref[...] = _dot(hb, w_ref[:, _COL_SM:_COL_END])


def _in_proj(x2, mod3, norm_g, w_cat, seq_len):
    t, d = x2.shape
    tm = IN_TM
    per_b = seq_len // tm
    row = lambda i: (i, 0)
    return pl.pallas_call(
        _inproj_kernel,
        out_shape=(jax.ShapeDtypeStruct((t, ATTN_DIM), BF16),) * 3
        + (jax.ShapeDtypeStruct((t, SSD_DIM), BF16),
           jax.ShapeDtypeStruct((t, SSD_CONV_DIM), BF16),
           jax.ShapeDtypeStruct((t, SMALL_W), F32)),
        grid=(t // tm,),
        in_specs=[pl.BlockSpec((tm, d), row),
                  pl.BlockSpec((1, d), lambda i: (0, 0)),
                  pl.BlockSpec((1, 1, d), lambda i: ((i // per_b) * N_MOD + 1, 0, 0)),
                  pl.BlockSpec((1, 1, d), lambda i: ((i // per_b) * N_MOD + 0, 0, 0)),
                  pl.BlockSpec((d, _COL_END), lambda i: (0, 0))],
        out_specs=(pl.BlockSpec((tm, ATTN_DIM), row),) * 3
        + (pl.BlockSpec((tm, SSD_DIM), row),
           pl.BlockSpec((tm, SSD_CONV_DIM), row),
           pl.BlockSpec((tm, SMALL_W), row)),
        compiler_params=_cparams(("arbitrary",)),
        name="in_proj",
    )(x2, norm_g.reshape(1, d), mod3, mod3, w_cat)


def _foxprep_kernel(q_ref, k_ref, sm_ref, fgb_ref, tri_ref, qa_ref, ka_ref, carry_ref):
    j = pl.program_id(1)

    @pl.when(j == 0)
    def _():
        carry_ref[...] = jnp.zeros_like(carry_ref)

    tb = sm_ref.shape[1]
    log_f = _log_sigmoid(sm_ref[0] + fgb_ref[...])
    cum = _dot_exact_lhs01(tri_ref[...], log_f) + carry_ref[...]
    carry_ref[...] = cum[tb - 1:tb, :]
    hi, mid, lo = (p.astype(F32) for p in _split3(cum))
    qf = q_ref[0].astype(F32) * (ATTN_HEAD_DIM ** -0.5)
    kf = k_ref[0].astype(F32)
    lane = lax.broadcasted_iota(I32, (tb, AUG_W - ATTN_HEAD_DIM), 1)
    for h in range(ATTN_HEADS):
        c0 = FG_LANE0 + h
        chi, cmid, clo = hi[:, c0:c0 + 1], mid[:, c0:c0 + 1], lo[:, c0:c0 + 1]
        aug_q = jnp.where(lane == 0, chi, jnp.where(lane == 1, cmid, jnp.where(lane == 2, clo,
                          jnp.where(lane < 6, 1.0, 0.0))))
        aug_k = jnp.where(lane < 3, 1.0, jnp.where(lane == 3, -chi, jnp.where(lane == 4, -cmid,
                          jnp.where(lane == 5, -clo, 0.0))))
        sl = slice(h * ATTN_HEAD_DIM, (h + 1) * ATTN_HEAD_DIM)
        qa_ref[0, h] = jnp.concatenate([qf[:, sl], aug_q], axis=1).astype(BF16)
        ka_ref[0, h] = jnp.concatenate([kf[:, sl], aug_k], axis=1).astype(BF16)


def _fox_prep(q3, k3, sm3, fgb_row, tri):
    bsz, seq_len, _ = q3.shape
    tb = PREP_TB
    blk = lambda b, j: (b, j, 0)
    aug = jax.ShapeDtypeStruct((bsz, ATTN_HEADS, seq_len, AUG_W), BF16)
    return pl.pallas_call(
        _foxprep_kernel,
        out_shape=(aug, aug),
        grid=(bsz, seq_len // tb),
        in_specs=[pl.BlockSpec((1, tb, ATTN_DIM), blk),
                  pl.BlockSpec((1, tb, ATTN_DIM), blk),
                  pl.BlockSpec((1, tb, SMALL_W), blk),
                  pl.BlockSpec((1, SMALL_W), lambda b, j: (0, 0)),
                  pl.BlockSpec((tb, tb), lambda b, j: (0, 0))],
        out_specs=(pl.BlockSpec((1, ATTN_HEADS, tb, AUG_W), lambda b, j: (b, 0, j, 0)),) * 2,
        scratch_shapes=[pltpu.VMEM((1, SMALL_W), F32)],
        compiler_params=_cparams(("arbitrary", "arbitrary")),
        name="fox_prep",
    )(q3, k3, sm3, fgb_row, tri)


def _attn_kernel(qa_ref, ka_ref, v_ref, g_ref, o_ref):
    i = pl.program_id(2)
    tq, tk = ATT_TQ, ATT_TK
    row = lax.broadcasted_iota(I32, (tq, tk), 0)
    col = lax.broadcasted_iota(I32, (tq, tk), 1)
    causal = row >= col
    heads = []
    for hh in range(2):
        q = qa_ref[0, hh]

        def step(kblk, vblk, carry, mask):
            m, l, acc = carry
            s = _dot_nt(q, kblk)
            if mask:
                s = jnp.where(causal, s, NEG_BIG)
            m_new = jnp.maximum(m, jnp.max(s, axis=-1, keepdims=True))
            alpha = jnp.exp(m - m_new)
            p = jnp.exp(s - m_new)
            l = alpha * l + jnp.sum(p, axis=-1, keepdims=True)
            acc = alpha * acc + _dot(p.astype(BF16), vblk)
            return m_new, l, acc

        def body(j, carry):
            off = pl.multiple_of(j * tk, tk)
            return step(ka_ref[0, hh, pl.ds(off, tk), :], v_ref[0, pl.ds(off, tk), :], carry, False)

        init = (jnp.full((tq, 1), NEG_BIG, F32), jnp.zeros((tq, 1), F32), jnp.zeros((tq, LANES), F32))
        carry = lax.fori_loop(0, i, body, init)
        off = pl.multiple_of(i * tk, tk)
        m, l, acc = step(ka_ref[0, hh, pl.ds(off, tk), :], v_ref[0, pl.ds(off, tk), :], carry, True)
        heads.append(acc / l)
    lane = lax.broadcasted_iota(I32, (tq, LANES), 1)
    first = lane < ATTN_HEAD_DIM
    o = jnp.where(first, heads[0], heads[1])
    sq = o * o
    s_all = jnp.sum(sq, axis=-1, keepdims=True)
    s0 = jnp.sum(jnp.where(first, sq, 0.0), axis=-1, keepdims=True)
    ms = jnp.where(first, s0, s_all - s0) * (1.0 / ATTN_HEAD_DIM)
    o_ref[0] = (o * lax.rsqrt(ms + EPS) * g_ref[...]).astype(BF16)


def _fox_attn(qa, ka, v3, g_row):
    bsz, _, seq_len, _ = qa.shape
    tq = ATT_TQ
    return pl.pallas_call(
        _attn_kernel,
        out_shape=jax.ShapeDtypeStruct((bsz, seq_len, ATTN_DIM), BF16),
        grid=(bsz, ATTN_HEADS // 2, seq_len // tq),
        in_specs=[pl.BlockSpec((1, 2, tq, AUG_W), lambda b, p, i: (b, p, i, 0)),
                  pl.BlockSpec((1, 2, seq_len, AUG_W), lambda b, p, i: (b, p, 0, 0)),
                  pl.BlockSpec((1, seq_len, LANES), lambda b, p, i: (b, 0, p)),
                  pl.BlockSpec((1, LANES), lambda b, p, i: (0, p))],
        out_specs=pl.BlockSpec((1, tq, LANES), lambda b, p, i: (b, i, p)),
        compiler_params=_cparams(("arbitrary", "arbitrary", "arbitrary")),
        name="fox_attn",
    )(qa, ka, v3, g_row)


def _ssd_kernel(xbc_ref, z_ref, sm_ref, cw_ref, cb_ref, dtb_ref, alog_ref, dsk_ref, g_ref, tri_ref, exp_ref,
                o_ref, ext_ref, state_ref):
    c = pl.program_id(1)
    q = SSD_CHUNK

    @pl.when(c == 0)
    def _():
        ext_ref[0:8, :] = jnp.zeros((8, SSD_CONV_DIM), F32)
        state_ref[...] = jnp.zeros_like(state_ref)

    ext_ref[8:8 + q, :] = xbc_ref[0].astype(F32)
    conv = cb_ref[...] + cw_ref[0:1, :] * ext_ref[5:5 + q, :]
    for j in range(1, SSD_CONV):
        conv = conv + cw_ref[j:j + 1, :] * ext_ref[5 + j:5 + j + q, :]
    ext_ref[0:8, :] = ext_ref[q:q + 8, :]
    xc = _silu(conv)
    xs = xc[:, :SSD_DIM]

    dt = _softplus(sm_ref[0] + dtb_ref[...])
    a_dt = -jnp.exp(alog_ref[...]) * dt
    a_cs = _dot_exact_lhs01(tri_ref[...], a_dt)
    a_last = a_cs[q - 1:q, :]
    e_cs = jnp.exp(a_cs)
    dec = jnp.exp(a_last - a_cs)
    a_cs_t = a_cs.T
    expand = exp_ref[...]
    dt_x = _dot_exact_rhs01(dt, expand)
    e_x = _dot_exact_rhs01(e_cs, expand)
    dec_x = _dot_exact_rhs01(dec, expand)
    x_dt = xs * dt_x
    x_dec = (x_dt * dec_x).astype(BF16)
    x_dt_b = x_dt.astype(BF16)

    row = lax.broadcasted_iota(I32, (q, q), 0)
    col = lax.broadcasted_iota(I32, (q, q), 1)
    lower = row >= col
    lane = lax.broadcasted_iota(I32, (q, LANES), 1)
    first = lane < SSD_HEAD_DIM
    y_parts = []
    for g in range(SSD_GROUPS):
        b_g = xc[:, SSD_DIM + g * SSD_STATE:SSD_DIM + (g + 1) * SSD_STATE]
        c_g = xc[:, SSD_DIM + (SSD_GROUPS + g) * SSD_STATE:SSD_DIM + (SSD_GROUPS + g + 1) * SSD_STATE]
        c_gb = c_g.astype(BF16)
        cb = _dot_nt(c_gb, b_g.astype(BF16))
        gs = slice(g * SSD_GROUP_DIM, (g + 1) * SSD_GROUP_DIM)
        st_prev = state_ref[g]
        y_off = _dot(c_gb, st_prev.astype(BF16)) * e_x[:, gs]
        s_new = _dot(b_g.T.astype(BF16), x_dec[:, gs])
        state_ref[g] = st_prev * e_x[q - 1:q, gs] + s_new
        for pr in range(2):
            pair = []
            for hh in range(2):
                h = g * 4 + pr * 2 + hh
                a_col = a_cs[:, DT_LANE0 + h:DT_LANE0 + h + 1]
                a_row = a_cs_t[DT_LANE0 + h:DT_LANE0 + h + 1, :]
                lmat = jnp.where(lower, jnp.exp(jnp.minimum(a_col - a_row, 0.0)), 0.0)
                m_h = (cb * lmat).astype(BF16)
                ps = slice((g * 2 + pr) * LANES, (g * 2 + pr + 1) * LANES)
                pair.append(_dot(m_h, x_dt_b[:, ps]))
            y_parts.append(jnp.where(first, pair[0], pair[1]) + y_off[:, pr * LANES:(pr + 1) * LANES])
    y = jnp.concatenate(y_parts, axis=1) + dsk_ref[...] * xs
    y = y * _silu(z_ref[0].astype(F32))
    outs = []
    for g in range(SSD_GROUPS):
        yg = y[:, g * SSD_GROUP_DIM:(g + 1) * SSD_GROUP_DIM]
        ms = jnp.mean(yg * yg, axis=-1, keepdims=True)
        outs.append(yg * lax.rsqrt(ms + EPS))
    o_ref[0] = (jnp.concatenate(outs, axis=1) * g_ref[...]).astype(BF16)


def _ssd(xbc3, z3, sm3, conv_w, conv_b, dtb_row, alog_row, dskip_row, g_row, tri, expand):
    bsz, seq_len, _ = xbc3.shape
    q = SSD_CHUNK
    blk = lambda b, c: (b, c, 0)
    const = lambda b, c: (0, 0)
    return pl.pallas_call(
        _ssd_kernel,
        out_shape=jax.ShapeDtypeStruct((bsz, seq_len, SSD_DIM), BF16),
        grid=(bsz, seq_len // q),
        in_specs=[pl.BlockSpec((1, q, SSD_CONV_DIM), blk),
                  pl.BlockSpec((1, q, SSD_DIM), blk),
                  pl.BlockSpec((1, q, SMALL_W), blk),
                  pl.BlockSpec((SSD_CONV, SSD_CONV_DIM), const),
                  pl.BlockSpec((1, SSD_CONV_DIM), const),
                  pl.BlockSpec((1, SMALL_W), const),
                  pl.BlockSpec((1, SMALL_W), const),
                  pl.BlockSpec((1, SSD_DIM), const),
                  pl.BlockSpec((1, SSD_DIM), const),
                  pl.BlockSpec((q, q), const),
                  pl.BlockSpec((SMALL_W, SSD_DIM), const)],
        out_specs=pl.BlockSpec((1, q, SSD_DIM), blk),
        scratch_shapes=[pltpu.VMEM((q + 8, SSD_CONV_DIM), F32),
                        pltpu.VMEM((SSD_GROUPS, SSD_STATE, SSD_GROUP_DIM), F32)],
        compiler_params=_cparams(("arbitrary", "arbitrary")),
        name="ssd",
    )(xbc3, z3, sm3, conv_w, conv_b, dtb_row, alog_row, dskip_row, g_row, tri, expand)


def _post_kernel(attn_ref, ssd_ref, x_ref, g1_ref, sh2_ref, sc2_ref, g2_ref, woa_ref, wos_ref, n2_ref,
                 wrt_ref, rb_ref, wgs_ref, wus_ref, wds_ref, upper_ref,
                 x1s_ref, h2_ref, eidx_ref, wts_ref, rank_ref, cnt_ref):
    i = pl.program_id(0)
    tm = x_ref.shape[0]

    @pl.when(i == 0)
    def _():
        cnt_ref[...] = jnp.zeros_like(cnt_ref)

    mixed = _dot(attn_ref[...], woa_ref[...]) + _dot(ssd_ref[...], wos_ref[...])
    x1 = x_ref[...] + g1_ref[0] * mixed
    var = jnp.mean(x1 * x1, axis=-1, keepdims=True)
    h2 = x1 * lax.rsqrt(var + EPS) * n2_ref[...]
    h2 = h2 * (1.0 + sc2_ref[0]) + sh2_ref[0]
    hb = h2.astype(BF16)
    h2_ref[...] = _pack_bf16_pair(h2)
    act = _silu(_dot(hb, wgs_ref[...])) * _dot(hb, wus_ref[...])
    shared = _dot(act.astype(BF16), wds_ref[...])
    x1s_ref[...] = x1 + g2_ref[0] * shared

    scores = _sigmoid(_dot_nt(wrt_ref[...], hb))
    biased = scores + rb_ref[...]
    grp = biased.reshape(N_ROUTE_GROUPS, GROUP_SIZE, tm)
    gi = lax.broadcasted_iota(I32, grp.shape, 1)
    m1 = jnp.max(grp, axis=1, keepdims=True)
    i1 = jnp.min(jnp.where(grp == m1, gi, GROUP_SIZE), axis=1, keepdims=True)
    m2 = jnp.max(jnp.where(gi == i1, -jnp.inf, grp), axis=1, keepdims=True)
    gsc = (m1 + m2).reshape(N_ROUTE_GROUPS, tm)
    gidx = lax.broadcasted_iota(I32, gsc.shape, 0)
    beaten = jnp.zeros(gsc.shape, I32)
    for o in range(N_ROUTE_GROUPS):
        other = gsc[o:o + 1, :]
        beats = (other > gsc) | ((other == gsc) & (gidx > o))
        beaten = beaten + beats.astype(I32)
    gmask = (beaten < TOPK_ROUTE_GROUPS).astype(F32)
    emask = jnp.broadcast_to(gmask.reshape(N_ROUTE_GROUPS, 1, tm), grp.shape).reshape(N_EXPERTS, tm)
    masked = jnp.where(emask > 0.5, biased, -jnp.inf)
    eiota = lax.broadcasted_iota(I32, (N_EXPERTS, tm), 0)
    idx_rows, w_rows = [], []
    sel = jnp.zeros((N_EXPERTS, tm), F32)
    for _ in range(TOP_K):
        mk = jnp.max(masked, axis=0, keepdims=True)
        ik = jnp.min(jnp.where(masked == mk, eiota, N_EXPERTS), axis=0, keepdims=True)
        hit = eiota == ik
        w_rows.append(jnp.sum(jnp.where(hit, scores, 0.0), axis=0, keepdims=True))
        idx_rows.append(ik)
        masked = jnp.where(hit, -jnp.inf, masked)
        sel = jnp.where(hit, 1.0, sel)
    w_all = jnp.concatenate(w_rows, axis=0)
    wts_ref[...] = w_all / jnp.sum(w_all, axis=0, keepdims=True) * ROUTED_SCALE
    eidx_ref[...] = jnp.concatenate(idx_rows, axis=0)
    before = cnt_ref[...][:, 0:1] + _dot(sel.astype(BF16), upper_ref[...])
    rank_rows = [jnp.sum(jnp.where(eiota == ik, before, 0.0), axis=0, keepdims=True) for ik in idx_rows]
    rank_ref[...] = jnp.concatenate(rank_rows, axis=0).astype(I32)
    cnt_ref[...] = cnt_ref[...] + jnp.sum(sel, axis=1, keepdims=True)


def _post_mixer(attn2, ssd2, x2, mod3, w_out_a, w_out_s, norm2_g, w_router_t, rb_col,
                wgs, wus, wds, upper, seq_len):
    t, d = x2.shape
    tm = POST_TM
    per_b = seq_len // tm
    row = lambda i: (i, 0)
    const = lambda i: (0, 0)
    modspec = lambda k: pl.BlockSpec((1, 1, d), lambda i: ((i // per_b) * N_MOD + k, 0, 0))
    slot = lambda i: (0, i)
    return pl.pallas_call(
        _post_kernel,
        out_shape=(jax.ShapeDtypeStruct((t, d), F32),
                   jax.ShapeDtypeStruct((t, d // 2), U32),
                   jax.ShapeDtypeStruct((TOP_K, t), I32),
                   jax.ShapeDtypeStruct((TOP_K, t), F32),
                   jax.ShapeDtypeStruct((TOP_K, t), I32),
                   jax.ShapeDtypeStruct((N_EXPERTS, LANES), F32)),
        grid=(t // tm,),
        in_specs=[pl.BlockSpec((tm, ATTN_DIM), row),
                  pl.BlockSpec((tm, SSD_DIM), row),
                  pl.BlockSpec((tm, d), row),
                  modspec(2), modspec(3), modspec(4), modspec(5),
                  pl.BlockSpec((ATTN_DIM, d), const),
                  pl.BlockSpec((SSD_DIM, d), const),
                  pl.BlockSpec((1, d), const),
                  pl.BlockSpec((N_EXPERTS, d), const),
                  pl.BlockSpec((N_EXPERTS, 1), const),
                  pl.BlockSpec((d, EXPERT_DIM), const),
                  pl.BlockSpec((d, EXPERT_DIM), const),
                  pl.BlockSpec((EXPERT_DIM, d), const),
                  pl.BlockSpec((tm, tm), const)],
        out_specs=(pl.BlockSpec((tm, d), row),
                   pl.BlockSpec((tm, d // 2), row),
                   pl.BlockSpec((TOP_K, tm), slot),
                   pl.BlockSpec((TOP_K, tm), slot),
                   pl.BlockSpec((TOP_K, tm), slot),
                   pl.BlockSpec((N_EXPERTS, LANES), const)),
        compiler_params=_cparams(("arbitrary",)),
        name="post_mixer",
    )(attn2, ssd2, x2, mod3, mod3, mod3, mod3, w_out_a, w_out_s, norm2_g.reshape(1, d),
      w_router_t, rb_col, wgs, wus, wds, upper)


def _dispatch_kernel(dest_ref, h_ref, xs_ref, sem):
    tm = h_ref.shape[0]

    def row_copy(t, k):
        return pltpu.make_async_copy(h_ref.at[pl.ds(t, 1), :], xs_ref.at[pl.ds(dest_ref[k, t], 1), :], sem)

    def issue(t, _):
        for k in range(TOP_K):
            row_copy(t, k).start()
        return 0

    lax.fori_loop(0, tm, issue, 0)

    def drain(t, _):
        for k in range(TOP_K):
            row_copy(t, k).wait()
        return 0

    lax.fori_loop(0, tm, drain, 0)


def _dispatch(dest, h2p, n_rows):
    t, w = h2p.shape
    tm = DISP_TM
    return pl.pallas_call(
        _dispatch_kernel,
        out_shape=jax.ShapeDtypeStruct((n_rows, w), U32),
        grid=(t // tm,),
        in_specs=[pl.BlockSpec((TOP_K, tm), lambda i: (0, i), memory_space=pltpu.SMEM),
                  pl.BlockSpec((tm, w), lambda i: (i, 0))],
        out_specs=pl.BlockSpec(memory_space=pl.ANY),
        scratch_shapes=[pltpu.SemaphoreType.DMA],
        compiler_params=_cparams(("arbitrary",)),
        name="dispatch",
    )(dest, h2p)


def _expert_kernel(be_ref, nu_ref, x_ref, wg_ref, wu_ref, wd_ref, y_ref):
    b = pl.program_id(0)

    @pl.when(b < nu_ref[0])
    def _():
        x = _unpack_bf16_pair(x_ref[...]).astype(BF16)
        gate = _dot(x, wg_ref[0].astype(BF16))
        up = _dot(x, wu_ref[0].astype(BF16))
        act = (_silu(gate) * up).astype(BF16)
        y_ref[...] = _pack_bf16_pair(_dot(act, wd_ref[0].astype(BF16)))


def _experts(block_expert, n_used, xs, wg, wu, wd):
    n_rows, w = xs.shape
    n_blocks = n_rows // EXP_BLK
    d, f = wg.shape[1], wg.shape[2]
    rows = lambda b, be, nu: (jnp.minimum(b, nu[0] - 1), 0)
    wsel = lambda b, be, nu: (be[b], 0, 0)
    grid_spec = pltpu.PrefetchScalarGridSpec(
        num_scalar_prefetch=2,
        grid=(n_blocks,),
        in_specs=[pl.BlockSpec((EXP_BLK, w), rows),
                  pl.BlockSpec((1, d, f), wsel),
                  pl.BlockSpec((1, d, f), wsel),
                  pl.BlockSpec((1, f, d), wsel)],
        out_specs=pl.BlockSpec((EXP_BLK, w), rows),
    )
    return pl.pallas_call(
        _expert_kernel,
        out_shape=jax.ShapeDtypeStruct((n_rows, w), U32),
        grid_spec=grid_spec,
        compiler_params=_cparams(("arbitrary",)),
        name="experts",
    )(block_expert, n_used, xs, wg, wu, wd)


def _combine_kernel(dest_ref, ys_ref, x1s_ref, w_ref, g2_ref, nf_ref, mf0_ref, mf1_ref, o_ref, buf_ref, sem):
    tm = x1s_ref.shape[0]

    def row_copy(t, k):
        return pltpu.make_async_copy(ys_ref.at[pl.ds(dest_ref[k, t], 1), :], buf_ref.at[k, pl.ds(t, 1), :], sem)

    def issue(t, _):
        for k in range(TOP_K):
            row_copy(t, k).start()
        return 0

    lax.fori_loop(0, tm, issue, 0)

    def drain(t, _):
        for k in range(TOP_K):
            row_copy(t, k).wait()
        return 0

    lax.fori_loop(0, tm, drain, 0)

    w = w_ref[...]
    routed = w[:, 0:1] * _unpack_bf16_pair(buf_ref[0])
    for k in range(1, TOP_K):
        routed = routed + w[:, k:k + 1] * _unpack_bf16_pair(buf_ref[k])
    xo = x1s_ref[...] + g2_ref[0] * routed
    var = jnp.mean(xo * xo, axis=-1, keepdims=True)
    y = xo * lax.rsqrt(var + EPS) * nf_ref[...]
    o_ref[...] = y * (1.0 + mf1_ref[0]) + mf0_ref[0]


def _combine(dest, ys, x1s, wts_t, mod3, normf_g, modf3, seq_len):
    t, d = x1s.shape
    tm = COMB_TM
    per_b = seq_len // tm
    row = lambda i: (i, 0)
    return pl.pallas_call(
        _combine_kernel,
        out_shape=jax.ShapeDtypeStruct((t, d), F32),
        grid=(t // tm,),
        in_specs=[pl.BlockSpec((TOP_K, tm), lambda i: (0, i), memory_space=pltpu.SMEM),
                  pl.BlockSpec(memory_space=pl.ANY),
                  pl.BlockSpec((tm, d), row),
                  pl.BlockSpec((tm, TOP_K), row),
                  pl.BlockSpec((1, 1, d), lambda i: ((i // per_b) * N_MOD + 5, 0, 0)),
                  pl.BlockSpec((1, d), lambda i: (0, 0)),
                  pl.BlockSpec((1, 1, d), lambda i: ((i // per_b) * 2 + 0, 0, 0)),
                  pl.BlockSpec((1, 1, d), lambda i: ((i // per_b) * 2 + 1, 0, 0))],
        out_specs=pl.BlockSpec((tm, d), row),
        scratch_shapes=[pltpu.VMEM((TOP_K, tm, d // 2), U32), pltpu.SemaphoreType.DMA],
        compiler_params=_cparams(("arbitrary",)),
        name="combine",
    )(dest, ys, x1s, wts_t, mod3, normf_g.reshape(1, d), modf3, modf3)


def _lane_row(vec, lane0):
    return jnp.zeros((1, SMALL_W), F32).at[0, lane0:lane0 + vec.shape[0]].set(vec.astype(F32))


def _layer(x2, mod3, bsz, seq_len, norm1_g, w_in, fg_bias, conv_w, conv_b, dt_bias, a_log, d_skip,
           attn_norm_g, ssd_norm_g, w_out, norm2_g, w_router, router_bias,
           w_gate_e, w_up_e, w_down_e, w_gate_s, w_up_s, w_down_s):
    t, d = x2.shape
    o_q, o_k, o_v, o_fg = 0, ATTN_DIM, 2 * ATTN_DIM, 3 * ATTN_DIM
    o_z = o_fg + ATTN_HEADS
    o_xbc = o_z + SSD_DIM
    o_dt = o_xbc + SSD_CONV_DIM
    small = jnp.zeros((d, SMALL_W), F32)
    small = small.at[:, FG_LANE0:FG_LANE0 + ATTN_HEADS].set(w_in[:, o_fg:o_z])
    small = small.at[:, DT_LANE0:DT_LANE0 + SSD_HEADS].set(w_in[:, o_dt:o_dt + SSD_HEADS])
    w_cat = jnp.concatenate([w_in[:, o_q:o_fg], w_in[:, o_z:o_dt], small], axis=1).astype(BF16)

    q2, k2, v2, z2, xbc2, sm2 = _in_proj(x2, mod3, norm1_g, w_cat, seq_len)
    shp = lambda a: a.reshape(bsz, seq_len, a.shape[-1])

    tri_prep = jnp.tril(jnp.ones((PREP_TB, PREP_TB), F32)).astype(BF16)
    qa, ka = _fox_prep(shp(q2), shp(k2), shp(sm2), _lane_row(fg_bias, FG_LANE0), tri_prep)
    attn3 = _fox_attn(qa, ka, shp(v2), attn_norm_g.reshape(1, ATTN_DIM).astype(F32))

    tri_chunk = jnp.tril(jnp.ones((SSD_CHUNK, SSD_CHUNK), F32)).astype(BF16)
    head_of_lane = jnp.arange(SSD_DIM, dtype=I32) // SSD_HEAD_DIM
    expand = (jnp.arange(SMALL_W, dtype=I32)[:, None] == head_of_lane[None, :] + DT_LANE0).astype(BF16)
    dskip_row = jnp.repeat(d_skip.astype(F32), SSD_HEAD_DIM).reshape(1, SSD_DIM)
    ssd3 = _ssd(shp(xbc2), shp(z2), shp(sm2), conv_w.astype(F32), conv_b.reshape(1, -1).astype(F32),
                _lane_row(dt_bias, DT_LANE0), _lane_row(a_log, DT_LANE0), dskip_row,
                ssd_norm_g.reshape(1, SSD_DIM).astype(F32), tri_chunk, expand)

    upper = jnp.triu(jnp.ones((POST_TM, POST_TM), F32), 1).astype(BF16)
    x1s, h2p, eidx, wts, rank, cnt = _post_mixer(
        attn3.reshape(t, ATTN_DIM), ssd3.reshape(t, SSD_DIM), x2, mod3,
        w_out[:ATTN_DIM].astype(BF16), w_out[ATTN_DIM:].astype(BF16), norm2_g,
        w_router.T.astype(BF16), router_bias.reshape(N_EXPERTS, 1).astype(F32),
        w_gate_s.astype(BF16), w_up_s.astype(BF16), w_down_s.astype(BF16), upper, seq_len)

    counts = cnt[:, 0].astype(I32)
    padded = (counts + EXP_BLK - 1) // EXP_BLK * EXP_BLK
    pend = jnp.cumsum(padded)
    pstart = pend - padded
    n_blocks = (t * TOP_K) // EXP_BLK + N_EXPERTS
    block_start = jnp.arange(n_blocks, dtype=I32) * EXP_BLK
    block_expert = jnp.minimum(jnp.searchsorted(pend, block_start, side="right"), N_EXPERTS - 1).astype(I32)
    n_used = (pend[-1] // EXP_BLK).astype(I32).reshape(1)
    dest = pstart[eidx] + rank

    xs = _dispatch(dest, h2p, n_blocks * EXP_BLK)
    ys = _experts(block_expert, n_used, xs, w_gate_e, w_up_e, w_down_e)
    return dest, ys, x1s, wts


def kernel(x, c, norm1_g, w_ada, b_ada, w_in, fg_bias, conv_w, conv_b, dt_bias, a_log, d_skip, attn_norm_g,
           ssd_norm_g, w_out, norm2_g, w_router, router_bias, w_gate_e, w_up_e, w_down_e, w_gate_s, w_up_s,
           w_down_s, normf_g, w_ada_f, b_ada_f):
    bsz, seq_len, d = x.shape
    assert w_ada.shape[0] == 1, "single-layer kernel"
    t = bsz * seq_len
    mod3 = _modulation(c, w_ada[0], b_ada[0]).reshape(bsz * N_MOD, 1, d)
    modf3 = _modulation(c, w_ada_f, b_ada_f).reshape(bsz * 2, 1, d)
    x2 = x.reshape(t, d)
    dest, ys, x1s, wts = _layer(
        x2, mod3, bsz, seq_len, norm1_g[0], w_in[0], fg_bias[0], conv_w[0], conv_b[0], dt_bias[0], a_log[0],
        d_skip[0], attn_norm_g[0], ssd_norm_g[0], w_out[0], norm2_g[0], w_router[0], router_bias[0],
        w_gate_e[0], w_up_e[0], w_down_e[0], w_gate_s[0], w_up_s[0], w_down_s[0])
    out = _combine(dest, ys, x1s, wts.T, mod3, normf_g, modf3, seq_len)
    return out.reshape(bsz, seq_len, d)
```

```python
import functools

import jax
import jax.numpy as jnp
from jax import lax
from jax.experimental import pallas as pl
from jax.experimental.pallas import tpu as pltpu

F32 = jnp.float32
BF16 = jnp.bfloat16
I32 = jnp.int32
U32 = jnp.uint32

EPS = 1e-6
D_MODEL = 1024
N_MOD = 6

ATTN_HEADS = 8
ATTN_HEAD_DIM = 64
ATTN_DIM = ATTN_HEADS * ATTN_HEAD_DIM

SSD_HEADS = 8
SSD_HEAD_DIM = 64
SSD_DIM = SSD_HEADS * SSD_HEAD_DIM
SSD_GROUPS = 2
SSD_STATE = 128
SSD_CONV = 4
SSD_CHUNK = 128
SSD_CONV_DIM = SSD_DIM + 2 * SSD_GROUPS * SSD_STATE
SSD_GROUP_DIM = SSD_DIM // SSD_GROUPS

N_EXPERTS = 256
TOP_K = 8
N_ROUTE_GROUPS = 8
TOPK_ROUTE_GROUPS = 4
GROUP_SIZE = N_EXPERTS // N_ROUTE_GROUPS
EXPERT_DIM = 256
ROUTED_SCALE = 2.5

LANES = 128
SMALL_W = LANES
FG_LANE0 = 0
DT_LANE0 = 8
AUG_W = LANES

IN_TM = 512
PREP_TB = 256
ATT_TQ = 512
ATT_TK = 512
POST_TM = 512
DISP_TM = 512
EXP_BLK = 512
EXP_SUB = 256
COMB_TM = 256
VMEM_LIMIT = 56 * 1024 * 1024
NEG_BIG = -1e30
LOG2E = 1.4426950408889634
DEST_TM = 2048


def _split3(x):
    hi = x.astype(BF16)
    r1 = x - hi.astype(F32)
    mid = r1.astype(BF16)
    lo = (r1 - mid.astype(F32)).astype(BF16)
    return hi, mid, lo


def _dot(a, b):
    return jnp.dot(a, b, preferred_element_type=F32)


def _dot_nt(a, b):
    return lax.dot_general(a, b, (((1,), (1,)), ((), ())), preferred_element_type=F32)


def _dot_exact_lhs01(lhs_bf16, x, pieces=3):
    parts = _split3(x)[:pieces]
    out = _dot(lhs_bf16, parts[0])
    for p in parts[1:]:
        out = out + _dot(lhs_bf16, p)
    return out


def _dot_exact_rhs01(x, rhs_bf16, pieces=2):
    parts = _split3(x)[:pieces]
    out = _dot(parts[0], rhs_bf16)
    for p in parts[1:]:
        out = out + _dot(p, rhs_bf16)
    return out


def _sigmoid(x):
    return 1.0 / (1.0 + jnp.exp(-x))


def _silu(x):
    return x * _sigmoid(x)


def _softplus(x):
    return jnp.maximum(x, 0.0) + jnp.log(1.0 + jnp.exp(-jnp.abs(x)))


def _log_sigmoid(x):
    return jnp.minimum(x, 0.0) - jnp.log(1.0 + jnp.exp(-jnp.abs(x)))


def _pack_bf16_pair(x):
    n = x.shape[1] // 2
    lo = pltpu.bitcast(x[:, :n].astype(BF16).astype(F32), U32)
    hi = pltpu.bitcast(x[:, n:].astype(BF16).astype(F32), U32)
    return (hi & jnp.uint32(0xFFFF0000)) | (lo >> 16)


def _unpack_bf16_pair(w):
    lo = pltpu.bitcast(w << 16, F32)
    hi = pltpu.bitcast(w & jnp.uint32(0xFFFF0000), F32)
    return jnp.concatenate([lo, hi], axis=1)


def _cparams(sem):
    return pltpu.CompilerParams(dimension_semantics=sem, vmem_limit_bytes=VMEM_LIMIT)


def _mod_kernel(c_ref, w_ref, b_ref, o_ref):
    c = c_ref[...]
    o_ref[...] = jnp.dot(_silu(c), w_ref[...], preferred_element_type=F32,
                         precision=lax.Precision.HIGHEST) + b_ref[...]


def _modulation(c, w, b):
    bsz, d = c.shape
    n = w.shape[1]
    tn = 1024
    return pl.pallas_call(
        _mod_kernel,
        out_shape=jax.ShapeDtypeStruct((bsz, n), F32),
        grid=(n // tn,),
        in_specs=[pl.BlockSpec((bsz, d), lambda j: (0, 0)),
                  pl.BlockSpec((d, tn), lambda j: (0, j)),
                  pl.BlockSpec((1, tn), lambda j: (0, j))],
        out_specs=pl.BlockSpec((bsz, tn), lambda j: (0, j)),
        compiler_params=_cparams(("arbitrary",)),
        name="modulation",
    )(c, w, b.reshape(1, n))


_COL_Q, _COL_K, _COL_V, _COL_Z, _COL_XBC, _COL_SM, _COL_END = 0, 512, 1024, 1536, 2048, 3072, 3200


def _inproj_kernel(x_ref, g_ref, sc_ref, sh_ref, w_ref, q_ref, k_ref, v_ref, z_ref, xbc_ref, sm_ref):
    x = x_ref[...]
    var = jnp.mean(x * x, axis=-1, keepdims=True)
    h = x * lax.rsqrt(var + EPS) * g_ref[...]
    h = h * (1.0 + sc_ref[0]) + sh_ref[0]
    hb = h.astype(BF16)
    q_ref[...] = _dot(hb, w_ref[:, _COL_Q:_COL_K]).astype(BF16)
    k_ref[...] = _dot(hb, w_ref[:, _COL_K:_COL_V]).astype(BF16)
    v_ref[...] = _dot(hb, w_ref[:, _COL_V:_COL_Z]).astype(BF16)
    z_ref[...] = _dot(hb, w_ref[:, _COL_Z:_COL_XBC]).astype(BF16)
    xbc_ref[:, :512] = _dot(hb, w_ref[:, _COL_XBC:_COL_XBC + 512]).astype(BF16)
    xbc_ref[:, 512:] = _dot(hb, w_ref[:, _COL_XBC + 512:_COL_SM]).astype(BF16)
    sm_ref[...] = _dot(hb, w_ref[:, _COL_SM:_COL_END])


def _in_proj(x2, mod3, norm_g, w_cat, seq_len):
    t, d = x2.shape
    tm = IN_TM
    per_b = seq_len // tm
    row = lambda i: (i, 0)
    return pl.pallas_call(
        _inproj_kernel,
        out_shape=(jax.ShapeDtypeStruct((t, ATTN_DIM), BF16),) * 3
        + (jax.ShapeDtypeStruct((t, SSD_DIM), BF16),
           jax.ShapeDtypeStruct((t, SSD_CONV_DIM), BF16),
           jax.ShapeDtypeStruct((t, SMALL_W), F32)),
        grid=(t // tm,),
        in_specs=[pl.BlockSpec((tm, d), row),
                  pl.BlockSpec((1, d), lambda i: (0, 0)),
                  pl.BlockSpec((1, 1, d), lambda i: ((i // per_b) * N_MOD + 1, 0, 0)),
                  pl.BlockSpec((1, 1, d), lambda i: ((i // per_b) * N_MOD + 0, 0, 0)),
                  pl.BlockSpec((d, _COL_END), lambda i: (0, 0))],
        out_specs=(pl.BlockSpec((tm, ATTN_DIM), row),) * 3
        + (pl.BlockSpec((tm, SSD_DIM), row),
           pl.BlockSpec((tm, SSD_CONV_DIM), row),
           pl.BlockSpec((tm, SMALL_W), row)),
        compiler_params=_cparams(("arbitrary",)),
        name="in_proj",
    )(x2, norm_g.reshape(1, d), mod3, mod3, w_cat)


def _foxprep_kernel(q_ref, k_ref, v_ref, sm_ref, fgb_ref, tri_ref, qa_ref, ka_ref, va_ref, carry_ref):
    j = pl.program_id(1)

    @pl.when(j == 0)
    def _():
        carry_ref[...] = jnp.zeros_like(carry_ref)

    tb = sm_ref.shape[1]
    log_f = _log_sigmoid(sm_ref[0] + fgb_ref[...]) * LOG2E
    cum = _dot_exact_lhs01(tri_ref[...], log_f) + carry_ref[...]
    carry_ref[...] = cum[tb - 1:tb, :]
    hi, mid, lo = (p.astype(F32) for p in _split3(cum))
    qf = q_ref[0].astype(F32) * (ATTN_HEAD_DIM ** -0.5 * LOG2E)
    kf = k_ref[0].astype(F32)
    vf = v_ref[0].astype(F32)
    lane = lax.broadcasted_iota(I32, (tb, AUG_W - ATTN_HEAD_DIM), 1)
    aug_v = jnp.where(lane == 0, 1.0, 0.0)
    for h in range(ATTN_HEADS):
        c0 = FG_LANE0 + h
        chi, cmid, clo = hi[:, c0:c0 + 1], mid[:, c0:c0 + 1], lo[:, c0:c0 + 1]
        aug_q = jnp.where(lane == 0, chi, jnp.where(lane == 1, cmid, jnp.where(lane == 2, clo,
                          jnp.where(lane < 6, 1.0, 0.0))))
        aug_k = jnp.where(lane < 3, 1.0, jnp.where(lane == 3, -chi, jnp.where(lane == 4, -cmid,
                          jnp.where(lane == 5, -clo, 0.0))))
        sl = slice(h * ATTN_HEAD_DIM, (h + 1) * ATTN_HEAD_DIM)
        qa_ref[0, h] = jnp.concatenate([qf[:, sl], aug_q], axis=1).astype(BF16)
        ka_ref[0, h] = jnp.concatenate([kf[:, sl], aug_k], axis=1).astype(BF16)
        va_ref[0, h] = jnp.concatenate([vf[:, sl], aug_v], axis=1).astype(BF16)


def _fox_prep(q3, k3, v3, sm3, fgb_row, tri):
    bsz, seq_len, _ = q3.shape
    tb = PREP_TB
    blk = lambda b, j: (b, j, 0)
    aug = jax.ShapeDtypeStruct((bsz, ATTN_HEADS, seq_len, AUG_W), BF16)
    return pl.pallas_call(
        _foxprep_kernel,
        out_shape=(aug, aug, aug),
        grid=(bsz, seq_len // tb),
        in_specs=[pl.BlockSpec((1, tb, ATTN_DIM), blk),
                  pl.BlockSpec((1, tb, ATTN_DIM), blk),
                  pl.BlockSpec((1, tb, ATTN_DIM), blk),
                  pl.BlockSpec((1, tb, SMALL_W), blk),
                  pl.BlockSpec((1, SMALL_W), lambda b, j: (0, 0)),
                  pl.BlockSpec((tb, tb), lambda b, j: (0, 0))],
        out_specs=(pl.BlockSpec((1, ATTN_HEADS, tb, AUG_W), lambda b, j: (b, 0, j, 0)),) * 3,
        scratch_shapes=[pltpu.VMEM((1, SMALL_W), F32)],
        compiler_params=_cparams(("arbitrary", "arbitrary")),
        name="fox_prep",
    )(q3, k3, v3, sm3, fgb_row, tri)


def _attn_kernel(qa_ref, ka_ref, va_ref, g_ref, o_ref):
    i = pl.program_id(2)
    tq, tk = ATT_TQ, ATT_TK
    diff = lax.broadcasted_iota(I32, (tq, tk), 0) - lax.broadcasted_iota(I32, (tq, tk), 1)

    def chain_step(hh, state, j):
        m_old, acc = state
        off = pl.multiple_of(j * tk, tk)
        s = _dot_nt(qa_ref[0, hh], ka_ref[0, hh, pl.ds(off, tk), :])
        s = jnp.where(diff >= (j - i) * tk, s, NEG_BIG)
        m_new = jnp.maximum(m_old, jnp.max(s, axis=-1, keepdims=True))
        p = jnp.exp2(s - m_new).astype(BF16)
        acc = jnp.exp2(m_old - m_new) * acc + _dot(p, va_ref[0, hh, pl.ds(off, tk), :])
        return m_new, acc

    def body(j, states):
        return tuple(chain_step(hh, states[hh], j) for hh in range(2))

    init = tuple((jnp.full((tq, 1), NEG_BIG, F32), jnp.zeros((tq, LANES), F32)) for _ in range(2))
    states = lax.fori_loop(0, i + 1, body, init)
    outs = []
    sum_lane = lax.broadcasted_iota(I32, (tq, LANES), 1) == ATTN_HEAD_DIM
    for hh in range(2):
        acc = states[hh][1]
        row_sum = jnp.sum(jnp.where(sum_lane, acc, 0.0), axis=-1, keepdims=True)
        outs.append(acc / row_sum)
    lane = lax.broadcasted_iota(I32, (tq, LANES), 1)
    first = lane < ATTN_HEAD_DIM
    o = jnp.where(first, outs[0], pltpu.roll(outs[1], ATTN_HEAD_DIM, 1))
    sq = o * o
    s_all = jnp.sum(sq, axis=-1, keepdims=True)
    s0 = jnp.sum(jnp.where(first, sq, 0.0), axis=-1, keepdims=True)
    ms = jnp.where(first, s0, s_all - s0) * (1.0 / ATTN_HEAD_DIM)
    o_ref[0] = (o * lax.rsqrt(ms + EPS) * g_ref[...]).astype(BF16)


def _fox_attn(qa, ka, va, g_row):
    bsz, _, seq_len, _ = qa.shape
    tq = ATT_TQ
    kv_spec = pl.BlockSpec((1, 2, seq_len, AUG_W), lambda b, p, i: (b, p, 0, 0))
    return pl.pallas_call(
        _attn_kernel,
        out_shape=jax.ShapeDtypeStruct((bsz, seq_len, ATTN_DIM), BF16),
        grid=(bsz, ATTN_HEADS // 2, seq_len // tq),
        in_specs=[pl.BlockSpec((1, 2, tq, AUG_W), lambda b, p, i: (b, p, i, 0)),
                  kv_spec, kv_spec,
                  pl.BlockSpec((1, LANES), lambda b, p, i: (0, p))],
        out_specs=pl.BlockSpec((1, tq, LANES), lambda b, p, i: (b, i, p)),
        compiler_params=_cparams(("arbitrary", "arbitrary", "arbitrary")),
        name="fox_attn",
    )(qa, ka, va, g_row)


def _ssd_kernel(xbc_ref, z_ref, sm_ref, cw_ref, cb_ref, dtb_ref, alog_ref, dsk_ref, g_ref, tri_ref, exp_ref,
                o_ref, ext_ref, state_ref):
    c = pl.program_id(1)
    q = SSD_CHUNK

    @pl.when(c == 0)
    def _():
        ext_ref[0:8, :] = jnp.zeros((8, SSD_CONV_DIM), F32)
        state_ref[...] = jnp.zeros_like(state_ref)

    ext_ref[8:8 + q, :] = xbc_ref[0].astype(F32)
    conv = cb_ref[...] + cw_ref[0:1, :] * ext_ref[5:5 + q, :]
    for j in range(1, SSD_CONV):
        conv = conv + cw_ref[j:j + 1, :] * ext_ref[5 + j:5 + j + q, :]
    ext_ref[0:8, :] = ext_ref[q:q + 8, :]
    xc = _silu(conv)
    xs = xc[:, :SSD_DIM]

    dt = _softplus(sm_ref[0] + dtb_ref[...])
    a_dt = -jnp.exp(alog_ref[...]) * dt
    a_cs = _dot_exact_lhs01(tri_ref[...], a_dt)
    a_last = a_cs[q - 1:q, :]
    e_cs = jnp.exp(a_cs)
    dec = jnp.exp(a_last - a_cs)
    a_cs_t = a_cs.T
    expand = exp_ref[...]
    dt_x = _dot_exact_rhs01(dt, expand)
    e_x = _dot_exact_rhs01(e_cs, expand)
    dec_x = _dot_exact_rhs01(dec, expand)
    x_dt = xs * dt_x
    x_dec = (x_dt * dec_x).astype(BF16)
    x_dt_b = x_dt.astype(BF16)

    row = lax.broadcasted_iota(I32, (q, q), 0)
    col = lax.broadcasted_iota(I32, (q, q), 1)
    lower = row >= col
    lane = lax.broadcasted_iota(I32, (q, LANES), 1)
    first = lane < SSD_HEAD_DIM
    y_parts = []
    for g in range(SSD_GROUPS):
        b_g = xc[:, SSD_DIM + g * SSD_STATE:SSD_DIM + (g + 1) * SSD_STATE]
        c_g = xc[:, SSD_DIM + (SSD_GROUPS + g) * SSD_STATE:SSD_DIM + (SSD_GROUPS + g + 1) * SSD_STATE]
        c_gb = c_g.astype(BF16)
        cb = _dot_nt(c_gb, b_g.astype(BF16))
        gs = slice(g * SSD_GROUP_DIM, (g + 1) * SSD_GROUP_DIM)
        st_prev = state_ref[g]
        y_off = _dot(c_gb, st_prev.astype(BF16)) * e_x[:, gs]
        s_new = _dot(b_g.T.astype(BF16), x_dec[:, gs])
        state_ref[g] = st_prev * e_x[q - 1:q, gs] + s_new
        for pr in range(2):
            pair = []
            for hh in range(2):
                h = g * 4 + pr * 2 + hh
                a_col = a_cs[:, DT_LANE0 + h:DT_LANE0 + h + 1]
                a_row = a_cs_t[DT_LANE0 + h:DT_LANE0 + h + 1, :]
                lmat = jnp.where(lower, jnp.exp(jnp.minimum(a_col - a_row, 0.0)), 0.0)
                m_h = (cb * lmat).astype(BF16)
                ps = slice((g * 2 + pr) * LANES, (g * 2 + pr + 1) * LANES)
                pair.append(_dot(m_h, x_dt_b[:, ps]))
            y_parts.append(jnp.where(first, pair[0], pair[1]) + y_off[:, pr * LANES:(pr + 1) * LANES])
    y = jnp.concatenate(y_parts, axis=1) + dsk_ref[...] * xs
    y = y * _silu(z_ref[0].astype(F32))
    outs = []
    for g in range(SSD_GROUPS):
        yg = y[:, g * SSD_GROUP_DIM:(g + 1) * SSD_GROUP_DIM]
        ms = jnp.mean(yg * yg, axis=-1, keepdims=True)
        outs.append(yg * lax.rsqrt(ms + EPS))
    o_ref[0] = (jnp.concatenate(outs, axis=1) * g_ref[...]).astype(BF16)


def _ssd(xbc3, z3, sm3, conv_w, conv_b, dtb_row, alog_row, dskip_row, g_row, tri, expand):
    bsz, seq_len, _ = xbc3.shape
    q = SSD_CHUNK
    blk = lambda b, c: (b, c, 0)
    const = lambda b, c: (0, 0)
    return pl.pallas_call(
        _ssd_kernel,
        out_shape=jax.ShapeDtypeStruct((bsz, seq_len, SSD_DIM), BF16),
        grid=(bsz, seq_len // q),
        in_specs=[pl.BlockSpec((1, q, SSD_CONV_DIM), blk),
                  pl.BlockSpec((1, q, SSD_DIM), blk),
                  pl.BlockSpec((1, q, SMALL_W), blk),
                  pl.BlockSpec((SSD_CONV, SSD_CONV_DIM), const),
                  pl.BlockSpec((1, SSD_CONV_DIM), const),
                  pl.BlockSpec((1, SMALL_W), const),
                  pl.BlockSpec((1, SMALL_W), const),
                  pl.BlockSpec((1, SSD_DIM), const),
                  pl.BlockSpec((1, SSD_DIM), const),
                  pl.BlockSpec((q, q), const),
                  pl.BlockSpec((SMALL_W, SSD_DIM), const)],
        out_specs=pl.BlockSpec((1, q, SSD_DIM), blk),
        scratch_shapes=[pltpu.VMEM((q + 8, SSD_CONV_DIM), F32),
                        pltpu.VMEM((SSD_GROUPS, SSD_STATE, SSD_GROUP_DIM), F32)],
        compiler_params=_cparams(("arbitrary", "arbitrary")),
        name="ssd",
    )(xbc3, z3, sm3, conv_w, conv_b, dtb_row, alog_row, dskip_row, g_row, tri, expand)


def _post_kernel(attn_ref, ssd_ref, x_ref, g1_ref, sh2_ref, sc2_ref, g2_ref, woa_ref, wos_ref, n2_ref,
                 wrt_ref, rb_ref, wgs_ref, wus_ref, wds_ref, upper_ref,
                 x1s_ref, h2_ref, eidx_ref, wts_ref, rank_ref, cnt_ref):
    i = pl.program_id(0)
    tm = x_ref.shape[0]

    @pl.when(i == 0)
    def _():
        cnt_ref[...] = jnp.zeros_like(cnt_ref)

    mixed = _dot(attn_ref[...], woa_ref[...]) + _dot(ssd_ref[...], wos_ref[...])
    x1 = x_ref[...] + g1_ref[0] * mixed
    var = jnp.mean(x1 * x1, axis=-1, keepdims=True)
    h2 = x1 * lax.rsqrt(var + EPS) * n2_ref[...]
    h2 = h2 * (1.0 + sc2_ref[0]) + sh2_ref[0]
    hb = h2.astype(BF16)
    h2_ref[...] = _pack_bf16_pair(h2)
    act = _silu(_dot(hb, wgs_ref[...])) * _dot(hb, wus_ref[...])
    shared = _dot(act.astype(BF16), wds_ref[...])
    x1s_ref[...] = x1 + g2_ref[0] * shared

    scores = _sigmoid(_dot_nt(wrt_ref[...], hb))
    biased = scores + rb_ref[...]
    grp = biased.reshape(N_ROUTE_GROUPS, GROUP_SIZE, tm)
    gi = lax.broadcasted_iota(I32, grp.shape, 1)
    m1 = jnp.max(grp, axis=1, keepdims=True)
    i1 = jnp.min(jnp.where(grp == m1, gi, GROUP_SIZE), axis=1, keepdims=True)
    m2 = jnp.max(jnp.where(gi == i1, -jnp.inf, grp), axis=1, keepdims=True)
    gsc = (m1 + m2).reshape(N_ROUTE_GROUPS, tm)
    gidx = lax.broadcasted_iota(I32, gsc.shape, 0)
    beaten = jnp.zeros(gsc.shape, I32)
    for o in range(N_ROUTE_GROUPS):
        other = gsc[o:o + 1, :]
        beats = (other > gsc) | ((other == gsc) & (gidx > o))
        beaten = beaten + beats.astype(I32)
    gmask = (beaten < TOPK_ROUTE_GROUPS).astype(F32)
    emask = jnp.broadcast_to(gmask.reshape(N_ROUTE_GROUPS, 1, tm), grp.shape).reshape(N_EXPERTS, tm)
    masked = jnp.where(emask > 0.5, biased, -jnp.inf)
    eiota = lax.broadcasted_iota(I32, (N_EXPERTS, tm), 0)
    idx_rows, w_rows = [], []
    sel = jnp.zeros((N_EXPERTS, tm), F32)
    for _ in range(TOP_K):
        mk = jnp.max(masked, axis=0, keepdims=True)
        ik = jnp.min(jnp.where(masked == mk, eiota, N_EXPERTS), axis=0, keepdims=True)
        hit = eiota == ik
        w_rows.append(jnp.sum(jnp.where(hit, scores, 0.0), axis=0, keepdims=True))
        idx_rows.append(ik)
        masked = jnp.where(hit, -jnp.inf, masked)
        sel = jnp.where(hit, 1.0, sel)
    w_all = jnp.concatenate(w_rows, axis=0)
    wts_ref[...] = w_all / jnp.sum(w_all, axis=0, keepdims=True) * ROUTED_SCALE
    eidx_ref[...] = jnp.concatenate(idx_rows, axis=0)
    before = cnt_ref[...][:, 0:1] + _dot(sel.astype(BF16), upper_ref[...])
    rank_rows = [jnp.sum(jnp.where(eiota == ik, before, 0.0), axis=0, keepdims=True) for ik in idx_rows]
    rank_ref[...] = jnp.concatenate(rank_rows, axis=0).astype(I32)
    cnt_ref[...] = cnt_ref[...] + jnp.sum(sel, axis=1, keepdims=True)


def _post_mixer(attn2, ssd2, x2, mod3, w_out_a, w_out_s, norm2_g, w_router_t, rb_col,
                wgs, wus, wds, upper, seq_len):
    t, d = x2.shape
    tm = POST_TM
    per_b = seq_len // tm
    row = lambda i: (i, 0)
    const = lambda i: (0, 0)
    modspec = lambda k: pl.BlockSpec((1, 1, d), lambda i: ((i // per_b) * N_MOD + k, 0, 0))
    slot = lambda i: (0, i)
    return pl.pallas_call(
        _post_kernel,
        out_shape=(jax.ShapeDtypeStruct((t, d), F32),
                   jax.ShapeDtypeStruct((t, d // 2), U32),
                   jax.ShapeDtypeStruct((TOP_K, t), I32),
                   jax.ShapeDtypeStruct((TOP_K, t), F32),
                   jax.ShapeDtypeStruct((TOP_K, t), I32),
                   jax.ShapeDtypeStruct((N_EXPERTS, LANES), F32)),
        grid=(t // tm,),
        in_specs=[pl.BlockSpec((tm, ATTN_DIM), row),
                  pl.BlockSpec((tm, SSD_DIM), row),
                  pl.BlockSpec((tm, d), row),
                  modspec(2), modspec(3), modspec(4), modspec(5),
                  pl.BlockSpec((ATTN_DIM, d), const),
                  pl.BlockSpec((SSD_DIM, d), const),
                  pl.BlockSpec((1, d), const),
                  pl.BlockSpec((N_EXPERTS, d), const),
                  pl.BlockSpec((N_EXPERTS, 1), const),
                  pl.BlockSpec((d, EXPERT_DIM), const),
                  pl.BlockSpec((d, EXPERT_DIM), const),
                  pl.BlockSpec((EXPERT_DIM, d), const),
                  pl.BlockSpec((tm, tm), const)],
        out_specs=(pl.BlockSpec((tm, d), row),
                   pl.BlockSpec((tm, d // 2), row),
                   pl.BlockSpec((TOP_K, tm), slot),
                   pl.BlockSpec((TOP_K, tm), slot),
                   pl.BlockSpec((TOP_K, tm), slot),
                   pl.BlockSpec((N_EXPERTS, LANES), const)),
        compiler_params=_cparams(("arbitrary",)),
        name="post_mixer",
    )(attn2, ssd2, x2, mod3, mod3, mod3, mod3, w_out_a, w_out_s, norm2_g.reshape(1, d),
      w_router_t, rb_col, wgs, wus, wds, upper)


def _dest_kernel(eidx_ref, rank_ref, pst_ref, dest_ref):
    tm = eidx_ref.shape[1]
    eiota = lax.broadcasted_iota(I32, (N_EXPERTS, tm), 0)
    rows = []
    for k in range(TOP_K):
        onehot = jnp.where(eiota == eidx_ref[k:k + 1, :], 1.0, 0.0).astype(BF16)
        r = _dot(pst_ref[...], onehot)
        rows.append(r[0:1, :] + r[1:2, :] + r[2:3, :])
    dest_ref[...] = jnp.concatenate(rows, axis=0).astype(I32) + rank_ref[...]


def _dest_rows(eidx, rank, pstart):
    k, t = eidx.shape
    tm = min(DEST_TM, t)
    pieces = jnp.stack([pstart & 0xFF0000, pstart & 0xFF00, pstart & 0xFF], axis=0)
    pst = jnp.zeros((8, N_EXPERTS), F32).at[:3].set(pieces.astype(F32)).astype(BF16)
    slot = lambda i: (0, i)
    return pl.pallas_call(
        _dest_kernel,
        out_shape=jax.ShapeDtypeStruct((k, t), I32),
        grid=(t // tm,),
        in_specs=[pl.BlockSpec((k, tm), slot), pl.BlockSpec((k, tm), slot),
                  pl.BlockSpec((8, N_EXPERTS), lambda i: (0, 0))],
        out_specs=pl.BlockSpec((k, tm), slot),
        compiler_params=_cparams(("arbitrary",)),
        name="dest_rows",
    )(eidx, rank, pst)


def _dispatch_kernel(dest_ref, h_ref, xs_ref, sem):
    tm = h_ref.shape[0]

    def row_copy(t, k):
        return pltpu.make_async_copy(h_ref.at[pl.ds(t, 1), :], xs_ref.at[pl.ds(dest_ref[k, t], 1), :], sem)

    def issue(t, _):
        for k in range(TOP_K):
            row_copy(t, k).start(priority=k % 2)
        return 0

    lax.fori_loop(0, tm, issue, 0)

    def drain(t, _):
        for k in range(TOP_K):
            row_copy(t, k).wait()
        return 0

    lax.fori_loop(0, tm, drain, 0)


def _dispatch(dest, h2p, n_rows):
    t, w = h2p.shape
    tm = DISP_TM
    return pl.pallas_call(
        _dispatch_kernel,
        out_shape=jax.ShapeDtypeStruct((n_rows, w), U32),
        grid=(t // tm,),
        in_specs=[pl.BlockSpec((TOP_K, tm), lambda i: (0, i), memory_space=pltpu.SMEM),
                  pl.BlockSpec((tm, w), lambda i: (i, 0))],
        out_specs=pl.BlockSpec(memory_space=pl.ANY),
        scratch_shapes=[pltpu.SemaphoreType.DMA],
        compiler_params=_cparams(("arbitrary",)),
        name="dispatch",
    )(dest, h2p)


def _expert_kernel(be_ref, nu_ref, x_ref, wg_ref, wu_ref, wd_ref, y_ref):
    b = pl.program_id(0)

    @pl.when(b < nu_ref[0])
    def _():
        wg = wg_ref[0].astype(BF16)
        wu = wu_ref[0].astype(BF16)
        wd = wd_ref[0].astype(BF16)
        for r in range(0, EXP_BLK, EXP_SUB):
            x = _unpack_bf16_pair(x_ref[r:r + EXP_SUB, :]).astype(BF16)
            act = (_silu(_dot(x, wg)) * _dot(x, wu)).astype(BF16)
            y_ref[r:r + EXP_SUB, :] = _pack_bf16_pair(_dot(act, wd))


def _experts(block_expert, n_used, xs, wg, wu, wd):
    n_rows, w = xs.shape
    n_blocks = n_rows // EXP_BLK
    d, f = wg.shape[1], wg.shape[2]
    rows = lambda b, be, nu: (jnp.minimum(b, nu[0] - 1), 0)
    wsel = lambda b, be, nu: (be[b], 0, 0)
    grid_spec = pltpu.PrefetchScalarGridSpec(
        num_scalar_prefetch=2,
        grid=(n_blocks,),
        in_specs=[pl.BlockSpec((EXP_BLK, w), rows),
                  pl.BlockSpec((1, d, f), wsel),
                  pl.BlockSpec((1, d, f), wsel),
                  pl.BlockSpec((1, f, d), wsel)],
        out_specs=pl.BlockSpec((EXP_BLK, w), rows),
    )
    return pl.pallas_call(
        _expert_kernel,
        out_shape=jax.ShapeDtypeStruct((n_rows, w), U32),
        grid_spec=grid_spec,
        compiler_params=_cparams(("arbitrary",)),
        name="experts",
    )(block_expert, n_used, xs, wg, wu, wd)


def _combine_kernel(dest_ref, ys_ref, x1s_ref, w_ref, g2_ref, nf_ref, mf0_ref, mf1_ref, o_ref, buf_ref, sem):
    tm = x1s_ref.shape[0]

    def row_copy(t, k):
        return pltpu.make_async_copy(ys_ref.at[pl.ds(dest_ref[k, t], 1), :], buf_ref.at[k, pl.ds(t, 1), :], sem)

    def issue(t, _):
        for k in range(TOP_K):
            row_copy(t, k).start(priority=k % 2)
        return 0

    lax.fori_loop(0, tm, issue, 0)

    def drain(t, _):
        for k in range(TOP_K):
            row_copy(t, k).wait()
        return 0

    lax.fori_loop(0, tm, drain, 0)

    w = w_ref[...]
    routed = w[:, 0:1] * _unpack_bf16_pair(buf_ref[0])
    for k in range(1, TOP_K):
        routed = routed + w[:, k:k + 1] * _unpack_bf16_pair(buf_ref[k])
    xo = x1s_ref[...] + g2_ref[0] * routed
    var = jnp.mean(xo * xo, axis=-1, keepdims=True)
    y = xo * lax.rsqrt(var + EPS) * nf_ref[...]
    o_ref[...] = y * (1.0 + mf1_ref[0]) + mf0_ref[0]


def _combine(dest, ys, x1s, wts_t, mod3, normf_g, modf3, seq_len):
    t, d = x1s.shape
    tm = COMB_TM
    per_b = seq_len // tm
    row = lambda i: (i, 0)
    return pl.pallas_call(
        _combine_kernel,
        out_shape=jax.ShapeDtypeStruct((t, d), F32),
        grid=(t // tm,),
        in_specs=[pl.BlockSpec((TOP_K, tm), lambda i: (0, i), memory_space=pltpu.SMEM),
                  pl.BlockSpec(memory_space=pl.ANY),
                  pl.BlockSpec((tm, d), row),
                  pl.BlockSpec((tm, TOP_K), row),
                  pl.BlockSpec((1, 1, d), lambda i: ((i // per_b) * N_MOD + 5, 0, 0)),
                  pl.BlockSpec((1, d), lambda i: (0, 0)),
                  pl.BlockSpec((1, 1, d), lambda i: ((i // per_b) * 2 + 0, 0, 0)),
                  pl.BlockSpec((1, 1, d), lambda i: ((i // per_b) * 2 + 1, 0, 0))],
        out_specs=pl.BlockSpec((tm, d), row),
        scratch_shapes=[pltpu.VMEM((TOP_K, tm, d // 2), U32), pltpu.SemaphoreType.DMA],
        compiler_params=_cparams(("arbitrary",)),
        name="combine",
    )(dest, ys, x1s, wts_t, mod3, normf_g.reshape(1, d), modf3, modf3)


def _lane_row(vec, lane0):
    return jnp.zeros((1, SMALL_W), F32).at[0, lane0:lane0 + vec.shape[0]].set(vec.astype(F32))


def _layer(x2, mod3, bsz, seq_len, norm1_g, w_in, fg_bias, conv_w, conv_b, dt_bias, a_log, d_skip,
           attn_norm_g, ssd_norm_g, w_out, norm2_g, w_router, router_bias,
           w_gate_e, w_up_e, w_down_e, w_gate_s, w_up_s, w_down_s):
    t, d = x2.shape
    o_q, o_k, o_v, o_fg = 0, ATTN_DIM, 2 * ATTN_DIM, 3 * ATTN_DIM
    o_z = o_fg + ATTN_HEADS
    o_xbc = o_z + SSD_DIM
    o_dt = o_xbc + SSD_CONV_DIM
    small = jnp.zeros((d, SMALL_W), F32)
    small = small.at[:, FG_LANE0:FG_LANE0 + ATTN_HEADS].set(w_in[:, o_fg:o_z])
    small = small.at[:, DT_LANE0:DT_LANE0 + SSD_HEADS].set(w_in[:, o_dt:o_dt + SSD_HEADS])
    w_cat = jnp.concatenate([w_in[:, o_q:o_fg], w_in[:, o_z:o_dt], small], axis=1).astype(BF16)

    q2, k2, v2, z2, xbc2, sm2 = _in_proj(x2, mod3, norm1_g, w_cat, seq_len)
    shp = lambda a: a.reshape(bsz, seq_len, a.shape[-1])

    tri_prep = jnp.tril(jnp.ones((PREP_TB, PREP_TB), F32)).astype(BF16)
    qa, ka, va = _fox_prep(shp(q2), shp(k2), shp(v2), shp(sm2), _lane_row(fg_bias, FG_LANE0), tri_prep)
    attn3 = _fox_attn(qa, ka, va, attn_norm_g.reshape(1, ATTN_DIM).astype(F32))

    tri_chunk = jnp.tril(jnp.ones((SSD_CHUNK, SSD_CHUNK), F32)).astype(BF16)
    head_of_lane = jnp.arange(SSD_DIM, dtype=I32) // SSD_HEAD_DIM
    expand = (jnp.arange(SMALL_W, dtype=I32)[:, None] == head_of_lane[None, :] + DT_LANE0).astype(BF16)
    dskip_row = jnp.repeat(d_skip.astype(F32), SSD_HEAD_DIM).reshape(1, SSD_DIM)
    ssd3 = _ssd(shp(xbc2), shp(z2), shp(sm2), conv_w.astype(F32), conv_b.reshape(1, -1).astype(F32),
                _lane_row(dt_bias, DT_LANE0), _lane_row(a_log, DT_LANE0), dskip_row,
                ssd_norm_g.reshape(1, SSD_DIM).astype(F32), tri_chunk, expand)

    upper = jnp.triu(jnp.ones((POST_TM, POST_TM), F32), 1).astype(BF16)
    x1s, h2p, eidx, wts, rank, cnt = _post_mixer(
        attn3.reshape(t, ATTN_DIM), ssd3.reshape(t, SSD_DIM), x2, mod3,
        w_out[:ATTN_DIM].astype(BF16), w_out[ATTN_DIM:].astype(BF16), norm2_g,
        w_router.T.astype(BF16), router_bias.reshape(N_EXPERTS, 1).astype(F32),
        w_gate_s.astype(BF16), w_up_s.astype(BF16), w_down_s.astype(BF16), upper, seq_len)

    counts = cnt[:, 0].astype(I32)
    padded = (counts + EXP_BLK - 1) // EXP_BLK * EXP_BLK
    pend = jnp.cumsum(padded)
    pstart = pend - padded
    n_blocks = (t * TOP_K) // EXP_BLK + N_EXPERTS
    block_start = jnp.arange(n_blocks, dtype=I32) * EXP_BLK
    block_expert = jnp.minimum(jnp.sum((pend[None, :] <= block_start[:, None]).astype(I32), axis=1),
                               N_EXPERTS - 1)
    n_used = (pend[-1] // EXP_BLK).astype(I32).reshape(1)
    dest = _dest_rows(eidx, rank, pstart)

    xs = _dispatch(dest, h2p, n_blocks * EXP_BLK)
    ys = _experts(block_expert, n_used, xs, w_gate_e, w_up_e, w_down_e)
    return dest, ys, x1s, wts


def kernel(x, c, norm1_g, w_ada, b_ada, w_in, fg_bias, conv_w, conv_b, dt_bias, a_log, d_skip, attn_norm_g,
           ssd_norm_g, w_out, norm2_g, w_router, router_bias, w_gate_e, w_up_e, w_down_e, w_gate_s, w_up_s,
           w_down_s, normf_g, w_ada_f, b_ada_f):
    bsz, seq_len, d = x.shape
    assert w_ada.shape[0] == 1, "single-layer kernel"
    t = bsz * seq_len
    mod3 = _modulation(c, w_ada[0], b_ada[0]).reshape(bsz * N_MOD, 1, d)
    modf3 = _modulation(c, w_ada_f, b_ada_f).reshape(bsz * 2, 1, d)
    x2 = x.reshape(t, d)
    dest, ys, x1s, wts = _layer(
        x2, mod3, bsz, seq_len, norm1_g[0], w_in[0], fg_bias[0], conv_w[0], conv_b[0], dt_bias[0], a_log[0],
        d_skip[0], attn_norm_g[0], ssd_norm_g[0], w_out[0], norm2_g[0], w_router[0], router_bias[0],
        w_gate_e[0], w_up_e[0], w_down_e[0], w_gate_s[0], w_up_s[0], w_down_s[0])
    out = _combine(dest, ys, x1s, wts.T, mod3, normf_g, modf3, seq_len)
    return out.reshape(bsz, seq_len, d)
```

```python
import functools

import jax
import jax.numpy as jnp
from jax import lax
from jax.experimental import pallas as pl
from jax.experimental.pallas import tpu as pltpu
from jax.experimental.pallas import tpu_sc as plsc

F32 = jnp.float32
BF16 = jnp.bfloat16
I32 = jnp.int32
U32 = jnp.uint32

EPS = 1e-6
D_MODEL = 1024
N_MOD = 6

ATTN_HEADS = 8
ATTN_HEAD_DIM = 64
ATTN_DIM = ATTN_HEADS * ATTN_HEAD_DIM

SSD_HEADS = 8
SSD_HEAD_DIM = 64
SSD_DIM = SSD_HEADS * SSD_HEAD_DIM
SSD_GROUPS = 2
SSD_STATE = 128
SSD_CONV = 4
SSD_CHUNK = 128
SSD_CONV_DIM = SSD_DIM + 2 * SSD_GROUPS * SSD_STATE
SSD_GROUP_DIM = SSD_DIM // SSD_GROUPS

N_EXPERTS = 256
TOP_K = 8
N_ROUTE_GROUPS = 8
TOPK_ROUTE_GROUPS = 4
GROUP_SIZE = N_EXPERTS // N_ROUTE_GROUPS
EXPERT_DIM = 256
ROUTED_SCALE = 2.5

LANES = 128
SMALL_W = LANES
FG_LANE0 = 0
DT_LANE0 = 8
AUG_W = LANES

IN_TM = 512
PREP_TB = 256
ATT_TQ = 512
ATT_TK = 512
POST_TM = 512
SC_CHUNK = 128
EXP_BLK = 512
EXP_SUB = 256
COMB_TM = 256
VMEM_LIMIT = 56 * 1024 * 1024
NEG_BIG = -1e30
LOG2E = 1.4426950408889634
DEST_TM = 2048


def _split3(x):
    hi = x.astype(BF16)
    r1 = x - hi.astype(F32)
    mid = r1.astype(BF16)
    lo = (r1 - mid.astype(F32)).astype(BF16)
    return hi, mid, lo


def _dot(a, b):
    return jnp.dot(a, b, preferred_element_type=F32)


def _dot_nt(a, b):
    return lax.dot_general(a, b, (((1,), (1,)), ((), ())), preferred_element_type=F32)


def _dot_exact_lhs01(lhs_bf16, x, pieces=3):
    parts = _split3(x)[:pieces]
    out = _dot(lhs_bf16, parts[0])
    for p in parts[1:]:
        out = out + _dot(lhs_bf16, p)
    return out


def _dot_exact_rhs01(x, rhs_bf16, pieces=2):
    parts = _split3(x)[:pieces]
    out = _dot(parts[0], rhs_bf16)
    for p in parts[1:]:
        out = out + _dot(p, rhs_bf16)
    return out


def _sigmoid(x):
    return 1.0 / (1.0 + jnp.exp(-x))


def _silu(x):
    return x * _sigmoid(x)


def _softplus(x):
    return jnp.maximum(x, 0.0) + jnp.log(1.0 + jnp.exp(-jnp.abs(x)))


def _log_sigmoid(x):
    return jnp.minimum(x, 0.0) - jnp.log(1.0 + jnp.exp(-jnp.abs(x)))


def _pack_bf16_pair(x):
    n = x.shape[1] // 2
    lo = pltpu.bitcast(x[:, :n].astype(BF16).astype(F32), U32)
    hi = pltpu.bitcast(x[:, n:].astype(BF16).astype(F32), U32)
    return (hi & jnp.uint32(0xFFFF0000)) | (lo >> 16)


def _unpack_bf16_pair(w):
    lo = pltpu.bitcast(w << 16, F32)
    hi = pltpu.bitcast(w & jnp.uint32(0xFFFF0000), F32)
    return jnp.concatenate([lo, hi], axis=1)


def _cparams(sem):
    return pltpu.CompilerParams(dimension_semantics=sem, vmem_limit_bytes=VMEM_LIMIT)


def _mod_kernel(c_ref, w_ref, b_ref, o_ref):
    c = c_ref[...]
    o_ref[...] = jnp.dot(_silu(c), w_ref[...], preferred_element_type=F32,
                         precision=lax.Precision.HIGHEST) + b_ref[...]


def _modulation(c, w, b):
    bsz, d = c.shape
    n = w.shape[1]
    tn = 1024
    return pl.pallas_call(
        _mod_kernel,
        out_shape=jax.ShapeDtypeStruct((bsz, n), F32),
        grid=(n // tn,),
        in_specs=[pl.BlockSpec((bsz, d), lambda j: (0, 0)),
                  pl.BlockSpec((d, tn), lambda j: (0, j)),
                  pl.BlockSpec((1, tn), lambda j: (0, j))],
        out_specs=pl.BlockSpec((bsz, tn), lambda j: (0, j)),
        compiler_params=_cparams(("arbitrary",)),
        name="modulation",
    )(c, w, b.reshape(1, n))


_COL_Q, _COL_K, _COL_V, _COL_Z, _COL_XBC, _COL_SM, _COL_END = 0, 512, 1024, 1536, 2048, 3072, 3200


def _inproj_kernel(x_ref, g_ref, sc_ref, sh_ref, w_ref, q_ref, k_ref, v_ref, z_ref, xbc_ref, sm_ref):
    x = x_ref[...]
    var = jnp.mean(x * x, axis=-1, keepdims=True)
    h = x * lax.rsqrt(var + EPS) * g_ref[...]
    h = h * (1.0 + sc_ref[0]) + sh_ref[0]
    hb = h.astype(BF16)
    q_ref[...] = _dot(hb, w_ref[:, _COL_Q:_COL_K]).astype(BF16)
    k_ref[...] = _dot(hb, w_ref[:, _COL_K:_COL_V]).astype(BF16)
    v_ref[...] = _dot(hb, w_ref[:, _COL_V:_COL_Z]).astype(BF16)
    z_ref[...] = _dot(hb, w_ref[:, _COL_Z:_COL_XBC]).astype(BF16)
    xbc_ref[:, :512] = _dot(hb, w_ref[:, _COL_XBC:_COL_XBC + 512]).astype(BF16)
    xbc_ref[:, 512:] = _dot(hb, w_ref[:, _COL_XBC + 512:_COL_SM]).astype(BF16)
    sm_ref[...] = _dot(hb, w_ref[:, _COL_SM:_COL_END])


def _in_proj(x2, mod3, norm_g, w_cat, seq_len):
    t, d = x2.shape
    tm = IN_TM
    per_b = seq_len // tm
    row = lambda i: (i, 0)
    return pl.pallas_call(
        _inproj_kernel,
        out_shape=(jax.ShapeDtypeStruct((t, ATTN_DIM), BF16),) * 3
        + (jax.ShapeDtypeStruct((t, SSD_DIM), BF16),
           jax.ShapeDtypeStruct((t, SSD_CONV_DIM), BF16),
           jax.ShapeDtypeStruct((t, SMALL_W), F32)),
        grid=(t // tm,),
        in_specs=[pl.BlockSpec((tm, d), row),
                  pl.BlockSpec((1, d), lambda i: (0, 0)),
                  pl.BlockSpec((1, 1, d), lambda i: ((i // per_b) * N_MOD + 1, 0, 0)),
                  pl.BlockSpec((1, 1, d), lambda i: ((i // per_b) * N_MOD + 0, 0, 0)),
                  pl.BlockSpec((d, _COL_END), lambda i: (0, 0))],
        out_specs=(pl.BlockSpec((tm, ATTN_DIM), row),) * 3
        + (pl.BlockSpec((tm, SSD_DIM), row),
           pl.BlockSpec((tm, SSD_CONV_DIM), row),
           pl.BlockSpec((tm, SMALL_W), row)),
        compiler_params=_cparams(("arbitrary",)),
        name="in_proj",
    )(x2, norm_g.reshape(1, d), mod3, mod3, w_cat)


def _foxprep_kernel(q_ref, k_ref, v_ref, sm_ref, fgb_ref, tri_ref, qa_ref, ka_ref, va_ref, carry_ref):
    j = pl.program_id(1)

    @pl.when(j == 0)
    def _():
        carry_ref[...] = jnp.zeros_like(carry_ref)

    tb = sm_ref.shape[1]
    log_f = _log_sigmoid(sm_ref[0] + fgb_ref[...]) * LOG2E
    cum = _dot_exact_lhs01(tri_ref[...], log_f) + carry_ref[...]
    carry_ref[...] = cum[tb - 1:tb, :]
    hi, mid, lo = (p.astype(F32) for p in _split3(cum))
    qf = q_ref[0].astype(F32) * (ATTN_HEAD_DIM ** -0.5 * LOG2E)
    kf = k_ref[0].astype(F32)
    vf = v_ref[0].astype(F32)
    lane = lax.broadcasted_iota(I32, (tb, AUG_W - ATTN_HEAD_DIM), 1)
    aug_v = jnp.where(lane == 0, 1.0, 0.0)
    for h in range(ATTN_HEADS):
        c0 = FG_LANE0 + h
        chi, cmid, clo = hi[:, c0:c0 + 1], mid[:, c0:c0 + 1], lo[:, c0:c0 + 1]
        aug_q = jnp.where(lane == 0, chi, jnp.where(lane == 1, cmid, jnp.where(lane == 2, clo,
                          jnp.where(lane < 6, 1.0, 0.0))))
        aug_k = jnp.where(lane < 3, 1.0, jnp.where(lane == 3, -chi, jnp.where(lane == 4, -cmid,
                          jnp.where(lane == 5, -clo, 0.0))))
        sl = slice(h * ATTN_HEAD_DIM, (h + 1) * ATTN_HEAD_DIM)
        qa_ref[0, h] = jnp.concatenate([qf[:, sl], aug_q], axis=1).astype(BF16)
        ka_ref[0, h] = jnp.concatenate([kf[:, sl], aug_k], axis=1).astype(BF16)
        va_ref[0, h] = jnp.concatenate([vf[:, sl], aug_v], axis=1).astype(BF16)


def _fox_prep(q3, k3, v3, sm3, fgb_row, tri):
    bsz, seq_len, _ = q3.shape
    tb = PREP_TB
    blk = lambda b, j: (b, j, 0)
    aug = jax.ShapeDtypeStruct((bsz, ATTN_HEADS, seq_len, AUG_W), BF16)
    return pl.pallas_call(
        _foxprep_kernel,
        out_shape=(aug, aug, aug),
        grid=(bsz, seq_len // tb),
        in_specs=[pl.BlockSpec((1, tb, ATTN_DIM), blk),
                  pl.BlockSpec((1, tb, ATTN_DIM), blk),
                  pl.BlockSpec((1, tb, ATTN_DIM), blk),
                  pl.BlockSpec((1, tb, SMALL_W), blk),
                  pl.BlockSpec((1, SMALL_W), lambda b, j: (0, 0)),
                  pl.BlockSpec((tb, tb), lambda b, j: (0, 0))],
        out_specs=(pl.BlockSpec((1, ATTN_HEADS, tb, AUG_W), lambda b, j: (b, 0, j, 0)),) * 3,
        scratch_shapes=[pltpu.VMEM((1, SMALL_W), F32)],
        compiler_params=_cparams(("arbitrary", "arbitrary")),
        name="fox_prep",
    )(q3, k3, v3, sm3, fgb_row, tri)


def _attn_kernel(qa_ref, ka_ref, va_ref, g_ref, o_ref):
    i = pl.program_id(2)
    tq, tk = ATT_TQ, ATT_TK
    diff = lax.broadcasted_iota(I32, (tq, tk), 0) - lax.broadcasted_iota(I32, (tq, tk), 1)

    def chain_step(hh, state, j):
        m_old, acc = state
        off = pl.multiple_of(j * tk, tk)
        s = _dot_nt(qa_ref[0, hh], ka_ref[0, hh, pl.ds(off, tk), :])
        s = jnp.where(diff >= (j - i) * tk, s, NEG_BIG)
        m_new = jnp.maximum(m_old, jnp.max(s, axis=-1, keepdims=True))
        p = jnp.exp2(s - m_new).astype(BF16)
        acc = jnp.exp2(m_old - m_new) * acc + _dot(p, va_ref[0, hh, pl.ds(off, tk), :])
        return m_new, acc

    def body(j, states):
        return tuple(chain_step(hh, states[hh], j) for hh in range(2))

    init = tuple((jnp.full((tq, 1), NEG_BIG, F32), jnp.zeros((tq, LANES), F32)) for _ in range(2))
    states = lax.fori_loop(0, i + 1, body, init)
    outs = []
    sum_lane = lax.broadcasted_iota(I32, (tq, LANES), 1) == ATTN_HEAD_DIM
    for hh in range(2):
        acc = states[hh][1]
        row_sum = jnp.sum(jnp.where(sum_lane, acc, 0.0), axis=-1, keepdims=True)
        outs.append(acc / row_sum)
    lane = lax.broadcasted_iota(I32, (tq, LANES), 1)
    first = lane < ATTN_HEAD_DIM
    o = jnp.where(first, outs[0], pltpu.roll(outs[1], ATTN_HEAD_DIM, 1))
    sq = o * o
    s_all = jnp.sum(sq, axis=-1, keepdims=True)
    s0 = jnp.sum(jnp.where(first, sq, 0.0), axis=-1, keepdims=True)
    ms = jnp.where(first, s0, s_all - s0) * (1.0 / ATTN_HEAD_DIM)
    o_ref[0] = (o * lax.rsqrt(ms + EPS) * g_ref[...]).astype(BF16)


def _fox_attn(qa, ka, va, g_row):
    bsz, _, seq_len, _ = qa.shape
    tq = ATT_TQ
    kv_spec = pl.BlockSpec((1, 2, seq_len, AUG_W), lambda b, p, i: (b, p, 0, 0))
    return pl.pallas_call(
        _attn_kernel,
        out_shape=jax.ShapeDtypeStruct((bsz, seq_len, ATTN_DIM), BF16),
        grid=(bsz, ATTN_HEADS // 2, seq_len // tq),
        in_specs=[pl.BlockSpec((1, 2, tq, AUG_W), lambda b, p, i: (b, p, i, 0)),
                  kv_spec, kv_spec,
                  pl.BlockSpec((1, LANES), lambda b, p, i: (0, p))],
        out_specs=pl.BlockSpec((1, tq, LANES), lambda b, p, i: (b, i, p)),
        compiler_params=_cparams(("arbitrary", "arbitrary", "arbitrary")),
        name="fox_attn",
    )(qa, ka, va, g_row)


def _ssd_kernel(xbc_ref, z_ref, sm_ref, cw_ref, cb_ref, dtb_ref, alog_ref, dsk_ref, g_ref, tri_ref, exp_ref,
                o_ref, ext_ref, state_ref):
    c = pl.program_id(1)
    q = SSD_CHUNK

    @pl.when(c == 0)
    def _():
        ext_ref[0:8, :] = jnp.zeros((8, SSD_CONV_DIM), F32)
        state_ref[...] = jnp.zeros_like(state_ref)

    ext_ref[8:8 + q, :] = xbc_ref[0].astype(F32)
    conv = cb_ref[...] + cw_ref[0:1, :] * ext_ref[5:5 + q, :]
    for j in range(1, SSD_CONV):
        conv = conv + cw_ref[j:j + 1, :] * ext_ref[5 + j:5 + j + q, :]
    ext_ref[0:8, :] = ext_ref[q:q + 8, :]
    xc = _silu(conv)
    xs = xc[:, :SSD_DIM]

    dt = _softplus(sm_ref[0] + dtb_ref[...])
    a_dt = -jnp.exp(alog_ref[...]) * dt
    a_cs = _dot_exact_lhs01(tri_ref[...], a_dt)
    a_last = a_cs[q - 1:q, :]
    e_cs = jnp.exp(a_cs)
    dec = jnp.exp(a_last - a_cs)
    a_cs_t = a_cs.T
    expand = exp_ref[...]
    dt_x = _dot_exact_rhs01(dt, expand)
    e_x = _dot_exact_rhs01(e_cs, expand)
    dec_x = _dot_exact_rhs01(dec, expand)
    x_dt = xs * dt_x
    x_dec = (x_dt * dec_x).astype(BF16)
    x_dt_b = x_dt.astype(BF16)

    row = lax.broadcasted_iota(I32, (q, q), 0)
    col = lax.broadcasted_iota(I32, (q, q), 1)
    lower = row >= col
    lane = lax.broadcasted_iota(I32, (q, LANES), 1)
    first = lane < SSD_HEAD_DIM
    y_parts = []
    for g in range(SSD_GROUPS):
        b_g = xc[:, SSD_DIM + g * SSD_STATE:SSD_DIM + (g + 1) * SSD_STATE]
        c_g = xc[:, SSD_DIM + (SSD_GROUPS + g) * SSD_STATE:SSD_DIM + (SSD_GROUPS + g + 1) * SSD_STATE]
        c_gb = c_g.astype(BF16)
        cb = _dot_nt(c_gb, b_g.astype(BF16))
        gs = slice(g * SSD_GROUP_DIM, (g + 1) * SSD_GROUP_DIM)
        st_prev = state_ref[g]
        y_off = _dot(c_gb, st_prev.astype(BF16)) * e_x[:, gs]
        s_new = _dot(b_g.T.astype(BF16), x_dec[:, gs])
        state_ref[g] = st_prev * e_x[q - 1:q, gs] + s_new
        for pr in range(2):
            pair = []
            for hh in range(2):
                h = g * 4 + pr * 2 + hh
                a_col = a_cs[:, DT_LANE0 + h:DT_LANE0 + h + 1]
                a_row = a_cs_t[DT_LANE0 + h:DT_LANE0 + h + 1, :]
                lmat = jnp.where(lower, jnp.exp(jnp.minimum(a_col - a_row, 0.0)), 0.0)
                m_h = (cb * lmat).astype(BF16)
                ps = slice((g * 2 + pr) * LANES, (g * 2 + pr + 1) * LANES)
                pair.append(_dot(m_h, x_dt_b[:, ps]))
            y_parts.append(jnp.where(first, pair[0], pair[1]) + y_off[:, pr * LANES:(pr + 1) * LANES])
    y = jnp.concatenate(y_parts, axis=1) + dsk_ref[...] * xs
    y = y * _silu(z_ref[0].astype(F32))
    outs = []
    for g in range(SSD_GROUPS):
        yg = y[:, g * SSD_GROUP_DIM:(g + 1) * SSD_GROUP_DIM]
        ms = jnp.mean(yg * yg, axis=-1, keepdims=True)
        outs.append(yg * lax.rsqrt(ms + EPS))
    o_ref[0] = (jnp.concatenate(outs, axis=1) * g_ref[...]).astype(BF16)


def _ssd(xbc3, z3, sm3, conv_w, conv_b, dtb_row, alog_row, dskip_row, g_row, tri, expand):
    bsz, seq_len, _ = xbc3.shape
    q = SSD_CHUNK
    blk = lambda b, c: (b, c, 0)
    const = lambda b, c: (0, 0)
    return pl.pallas_call(
        _ssd_kernel,
        out_shape=jax.ShapeDtypeStruct((bsz, seq_len, SSD_DIM), BF16),
        grid=(bsz, seq_len // q),
        in_specs=[pl.BlockSpec((1, q, SSD_CONV_DIM), blk),
                  pl.BlockSpec((1, q, SSD_DIM), blk),
                  pl.BlockSpec((1, q, SMALL_W), blk),
                  pl.BlockSpec((SSD_CONV, SSD_CONV_DIM), const),
                  pl.BlockSpec((1, SSD_CONV_DIM), const),
                  pl.BlockSpec((1, SMALL_W), const),
                  pl.BlockSpec((1, SMALL_W), const),
                  pl.BlockSpec((1, SSD_DIM), const),
                  pl.BlockSpec((1, SSD_DIM), const),
                  pl.BlockSpec((q, q), const),
                  pl.BlockSpec((SMALL_W, SSD_DIM), const)],
        out_specs=pl.BlockSpec((1, q, SSD_DIM), blk),
        scratch_shapes=[pltpu.VMEM((q + 8, SSD_CONV_DIM), F32),
                        pltpu.VMEM((SSD_GROUPS, SSD_STATE, SSD_GROUP_DIM), F32)],
        compiler_params=_cparams(("arbitrary", "arbitrary")),
        name="ssd",
    )(xbc3, z3, sm3, conv_w, conv_b, dtb_row, alog_row, dskip_row, g_row, tri, expand)


def _post_kernel(attn_ref, ssd_ref, x_ref, g1_ref, sh2_ref, sc2_ref, g2_ref, woa_ref, wos_ref, n2_ref,
                 wrt_ref, rb_ref, wgs_ref, wus_ref, wds_ref, upper_ref,
                 x1s_ref, h2_ref, eidx_ref, wts_ref, rank_ref, cnt_ref):
    i = pl.program_id(0)
    tm = x_ref.shape[0]

    @pl.when(i == 0)
    def _():
        cnt_ref[...] = jnp.zeros_like(cnt_ref)

    mixed = _dot(attn_ref[...], woa_ref[...]) + _dot(ssd_ref[...], wos_ref[...])
    x1 = x_ref[...] + g1_ref[0] * mixed
    var = jnp.mean(x1 * x1, axis=-1, keepdims=True)
    h2 = x1 * lax.rsqrt(var + EPS) * n2_ref[...]
    h2 = h2 * (1.0 + sc2_ref[0]) + sh2_ref[0]
    hb = h2.astype(BF16)
    h2_ref[...] = _pack_bf16_pair(h2)
    act = _silu(_dot(hb, wgs_ref[...])) * _dot(hb, wus_ref[...])
    shared = _dot(act.astype(BF16), wds_ref[...])
    x1s_ref[...] = x1 + g2_ref[0] * shared

    scores = _sigmoid(_dot_nt(wrt_ref[...], hb))
    biased = scores + rb_ref[...]
    grp = biased.reshape(N_ROUTE_GROUPS, GROUP_SIZE, tm)
    gi = lax.broadcasted_iota(I32, grp.shape, 1)
    m1 = jnp.max(grp, axis=1, keepdims=True)
    i1 = jnp.min(jnp.where(grp == m1, gi, GROUP_SIZE), axis=1, keepdims=True)
    m2 = jnp.max(jnp.where(gi == i1, -jnp.inf, grp), axis=1, keepdims=True)
    gsc = (m1 + m2).reshape(N_ROUTE_GROUPS, tm)
    gidx = lax.broadcasted_iota(I32, gsc.shape, 0)
    beaten = jnp.zeros(gsc.shape, I32)
    for o in range(N_ROUTE_GROUPS):
        other = gsc[o:o + 1, :]
        beats = (other > gsc) | ((other == gsc) & (gidx > o))
        beaten = beaten + beats.astype(I32)
    gmask = (beaten < TOPK_ROUTE_GROUPS).astype(F32)
    emask = jnp.broadcast_to(gmask.reshape(N_ROUTE_GROUPS, 1, tm), grp.shape).reshape(N_EXPERTS, tm)
    masked = jnp.where(emask > 0.5, biased, -jnp.inf)
    eiota = lax.broadcasted_iota(I32, (N_EXPERTS, tm), 0)
    idx_rows, w_rows = [], []
    sel = jnp.zeros((N_EXPERTS, tm), F32)
    for _ in range(TOP_K):
        mk = jnp.max(masked, axis=0, keepdims=True)
        ik = jnp.min(jnp.where(masked == mk, eiota, N_EXPERTS), axis=0, keepdims=True)
        hit = eiota == ik
        w_rows.append(jnp.sum(jnp.where(hit, scores, 0.0), axis=0, keepdims=True))
        idx_rows.append(ik)
        masked = jnp.where(hit, -jnp.inf, masked)
        sel = jnp.where(hit, 1.0, sel)
    w_all = jnp.concatenate(w_rows, axis=0)
    wts_ref[...] = w_all / jnp.sum(w_all, axis=0, keepdims=True) * ROUTED_SCALE
    eidx_ref[...] = jnp.concatenate(idx_rows, axis=0)
    before = cnt_ref[...][:, 0:1] + _dot(sel.astype(BF16), upper_ref[...])
    rank_rows = [jnp.sum(jnp.where(eiota == ik, before, 0.0), axis=0, keepdims=True) for ik in idx_rows]
    rank_ref[...] = jnp.concatenate(rank_rows, axis=0).astype(I32)
    cnt_ref[...] = cnt_ref[...] + jnp.sum(sel, axis=1, keepdims=True)


def _post_mixer(attn2, ssd2, x2, mod3, w_out_a, w_out_s, norm2_g, w_router_t, rb_col,
                wgs, wus, wds, upper, seq_len):
    t, d = x2.shape
    tm = POST_TM
    per_b = seq_len // tm
    row = lambda i: (i, 0)
    const = lambda i: (0, 0)
    modspec = lambda k: pl.BlockSpec((1, 1, d), lambda i: ((i // per_b) * N_MOD + k, 0, 0))
    slot = lambda i: (0, i)
    return pl.pallas_call(
        _post_kernel,
        out_shape=(jax.ShapeDtypeStruct((t, d), F32),
                   jax.ShapeDtypeStruct((t, d // 2), U32),
                   jax.ShapeDtypeStruct((TOP_K, t), I32),
                   jax.ShapeDtypeStruct((TOP_K, t), F32),
                   jax.ShapeDtypeStruct((TOP_K, t), I32),
                   jax.ShapeDtypeStruct((N_EXPERTS, LANES), F32)),
        grid=(t // tm,),
        in_specs=[pl.BlockSpec((tm, ATTN_DIM), row),
                  pl.BlockSpec((tm, SSD_DIM), row),
                  pl.BlockSpec((tm, d), row),
                  modspec(2), modspec(3), modspec(4), modspec(5),
                  pl.BlockSpec((ATTN_DIM, d), const),
                  pl.BlockSpec((SSD_DIM, d), const),
                  pl.BlockSpec((1, d), const),
                  pl.BlockSpec((N_EXPERTS, d), const),
                  pl.BlockSpec((N_EXPERTS, 1), const),
                  pl.BlockSpec((d, EXPERT_DIM), const),
                  pl.BlockSpec((d, EXPERT_DIM), const),
                  pl.BlockSpec((EXPERT_DIM, d), const),
                  pl.BlockSpec((tm, tm), const)],
        out_specs=(pl.BlockSpec((tm, d), row),
                   pl.BlockSpec((tm, d // 2), row),
                   pl.BlockSpec((TOP_K, tm), slot),
                   pl.BlockSpec((TOP_K, tm), slot),
                   pl.BlockSpec((TOP_K, tm), slot),
                   pl.BlockSpec((N_EXPERTS, LANES), const)),
        compiler_params=_cparams(("arbitrary",)),
        name="post_mixer",
    )(attn2, ssd2, x2, mod3, mod3, mod3, mod3, w_out_a, w_out_s, norm2_g.reshape(1, d),
      w_router_t, rb_col, wgs, wus, wds, upper)


def _dest_kernel(eidx_ref, rank_ref, pst_ref, dest_ref):
    tm = eidx_ref.shape[1]
    eiota = lax.broadcasted_iota(I32, (N_EXPERTS, tm), 0)
    rows = []
    for k in range(TOP_K):
        onehot = jnp.where(eiota == eidx_ref[k:k + 1, :], 1.0, 0.0).astype(BF16)
        r = _dot(pst_ref[...], onehot)
        rows.append(r[0:1, :] + r[1:2, :] + r[2:3, :])
    dest_ref[...] = jnp.concatenate(rows, axis=0).astype(I32) + rank_ref[...]


def _dest_rows(eidx, rank, pstart):
    k, t = eidx.shape
    tm = min(DEST_TM, t)
    pieces = jnp.stack([pstart & 0xFF0000, pstart & 0xFF00, pstart & 0xFF], axis=0)
    pst = jnp.zeros((8, N_EXPERTS), F32).at[:3].set(pieces.astype(F32)).astype(BF16)
    slot = lambda i: (0, i)
    return pl.pallas_call(
        _dest_kernel,
        out_shape=jax.ShapeDtypeStruct((k, t), I32),
        grid=(t // tm,),
        in_specs=[pl.BlockSpec((k, tm), slot), pl.BlockSpec((k, tm), slot),
                  pl.BlockSpec((8, N_EXPERTS), lambda i: (0, 0))],
        out_specs=pl.BlockSpec((k, tm), slot),
        compiler_params=_cparams(("arbitrary",)),
        name="dest_rows",
    )(eidx, rank, pst)


def _sc_mesh():
    return plsc.VectorSubcoreMesh(core_axis_name="c", subcore_axis_name="s")


def _sc_worker(n_workers_per_core):
    return lax.axis_index("s") * n_workers_per_core + lax.axis_index("c")


def _dispatch(dest, h2p, n_rows):
    t, w = h2p.shape
    mesh = _sc_mesh()
    n_workers = mesh.num_cores * mesh.num_subcores
    per_w = t // n_workers
    ch = min(SC_CHUNK, per_w)

    def body(dest_hbm, h_hbm, xs_hbm, idx_v, rows_v, sem):
        base_w = _sc_worker(mesh.num_cores) * per_w

        @pl.loop(0, per_w // ch)
        def _(ci):
            base = pl.multiple_of(base_w + ci * ch, ch)
            pltpu.sync_copy(dest_hbm.at[:, pl.ds(base, ch)], idx_v)
            pltpu.sync_copy(h_hbm.at[pl.ds(base, ch)], rows_v)
            copies = [pltpu.async_copy(rows_v, xs_hbm.at[idx_v.at[k]], sem) for k in range(TOP_K)]
            for cp in copies:
                cp.wait()

    return pl.kernel(
        body,
        out_type=jax.ShapeDtypeStruct((n_rows, w), U32),
        mesh=mesh,
        scratch_types=[pltpu.VMEM((TOP_K, ch), I32), pltpu.VMEM((ch, w), U32), pltpu.SemaphoreType.DMA],
        name="dispatch",
    )(dest, h2p)


def _undispatch(dest, ys, t):
    w = ys.shape[1]
    mesh = _sc_mesh()
    n_workers = mesh.num_cores * mesh.num_subcores
    per_w = t // n_workers
    ch = min(SC_CHUNK, per_w)

    def body(dest_hbm, ys_hbm, ytok_hbm, idx_v, rows_v, sem):
        base_w = _sc_worker(mesh.num_cores) * per_w

        @pl.loop(0, per_w // ch)
        def _(ci):
            base = pl.multiple_of(base_w + ci * ch, ch)
            pltpu.sync_copy(dest_hbm.at[:, pl.ds(base, ch)], idx_v)
            for k in range(TOP_K):
                pltpu.async_copy(ys_hbm.at[idx_v.at[k]], rows_v, sem).wait()
                pltpu.sync_copy(rows_v, ytok_hbm.at[k, pl.ds(base, ch)])

    return pl.kernel(
        body,
        out_type=jax.ShapeDtypeStruct((TOP_K, t, w), U32),
        mesh=mesh,
        scratch_types=[pltpu.VMEM((TOP_K, ch), I32), pltpu.VMEM((ch, w), U32), pltpu.SemaphoreType.DMA],
        name="undispatch",
    )(dest, ys)


def _expert_kernel(be_ref, nu_ref, x_ref, wg_ref, wu_ref, wd_ref, y_ref):
    b = pl.program_id(0)

    @pl.when(b < nu_ref[0])
    def _():
        wg = wg_ref[0].astype(BF16)
        wu = wu_ref[0].astype(BF16)
        wd = wd_ref[0].astype(BF16)
        for r in range(0, EXP_BLK, EXP_SUB):
            x = _unpack_bf16_pair(x_ref[r:r + EXP_SUB, :]).astype(BF16)
            act = (_silu(_dot(x, wg)) * _dot(x, wu)).astype(BF16)
            y_ref[r:r + EXP_SUB, :] = _pack_bf16_pair(_dot(act, wd))


def _experts(block_expert, n_used, xs, wg, wu, wd):
    n_rows, w = xs.shape
    n_blocks = n_rows // EXP_BLK
    d, f = wg.shape[1], wg.shape[2]
    rows = lambda b, be, nu: (jnp.minimum(b, nu[0] - 1), 0)
    wsel = lambda b, be, nu: (be[b], 0, 0)
    grid_spec = pltpu.PrefetchScalarGridSpec(
        num_scalar_prefetch=2,
        grid=(n_blocks,),
        in_specs=[pl.BlockSpec((EXP_BLK, w), rows),
                  pl.BlockSpec((1, d, f), wsel),
                  pl.BlockSpec((1, d, f), wsel),
                  pl.BlockSpec((1, f, d), wsel)],
        out_specs=pl.BlockSpec((EXP_BLK, w), rows),
    )
    return pl.pallas_call(
        _expert_kernel,
        out_shape=jax.ShapeDtypeStruct((n_rows, w), U32),
        grid_spec=grid_spec,
        compiler_params=_cparams(("arbitrary",)),
        name="experts",
    )(block_expert, n_used, xs, wg, wu, wd)


def _combine_kernel(ytok_ref, x1s_ref, w_ref, g2_ref, nf_ref, mf0_ref, mf1_ref, o_ref):
    w = w_ref[...]
    routed = w[:, 0:1] * _unpack_bf16_pair(ytok_ref[0])
    for k in range(1, TOP_K):
        routed = routed + w[:, k:k + 1] * _unpack_bf16_pair(ytok_ref[k])
    xo = x1s_ref[...] + g2_ref[0] * routed
    var = jnp.mean(xo * xo, axis=-1, keepdims=True)
    y = xo * lax.rsqrt(var + EPS) * nf_ref[...]
    o_ref[...] = y * (1.0 + mf1_ref[0]) + mf0_ref[0]


def _combine(ytok, x1s, wts_t, mod3, normf_g, modf3, seq_len):
    t, d = x1s.shape
    tm = COMB_TM
    per_b = seq_len // tm
    row = lambda i: (i, 0)
    return pl.pallas_call(
        _combine_kernel,
        out_shape=jax.ShapeDtypeStruct((t, d), F32),
        grid=(t // tm,),
        in_specs=[pl.BlockSpec((TOP_K, tm, d // 2), lambda i: (0, i, 0)),
                  pl.BlockSpec((tm, d), row),
                  pl.BlockSpec((tm, TOP_K), row),
                  pl.BlockSpec((1, 1, d), lambda i: ((i // per_b) * N_MOD + 5, 0, 0)),
                  pl.BlockSpec((1, d), lambda i: (0, 0)),
                  pl.BlockSpec((1, 1, d), lambda i: ((i // per_b) * 2 + 0, 0, 0)),
                  pl.BlockSpec((1, 1, d), lambda i: ((i // per_b) * 2 + 1, 0, 0))],
        out_specs=pl.BlockSpec((tm, d), row),
        compiler_params=_cparams(("arbitrary",)),
        name="combine",
    )(ytok, x1s, wts_t, mod3, normf_g.reshape(1, d), modf3, modf3)


def _lane_row(vec, lane0):
    return jnp.zeros((1, SMALL_W), F32).at[0, lane0:lane0 + vec.shape[0]].set(vec.astype(F32))


def _layer(x2, mod3, bsz, seq_len, norm1_g, w_in, fg_bias, conv_w, conv_b, dt_bias, a_log, d_skip,
           attn_norm_g, ssd_norm_g, w_out, norm2_g, w_router, router_bias,
           w_gate_e, w_up_e, w_down_e, w_gate_s, w_up_s, w_down_s):
    t, d = x2.shape
    o_q, o_k, o_v, o_fg = 0, ATTN_DIM, 2 * ATTN_DIM, 3 * ATTN_DIM
    o_z = o_fg + ATTN_HEADS
    o_xbc = o_z + SSD_DIM
    o_dt = o_xbc + SSD_CONV_DIM
    small = jnp.zeros((d, SMALL_W), F32)
    small = small.at[:, FG_LANE0:FG_LANE0 + ATTN_HEADS].set(w_in[:, o_fg:o_z])
    small = small.at[:, DT_LANE0:DT_LANE0 + SSD_HEADS].set(w_in[:, o_dt:o_dt + SSD_HEADS])
    w_cat = jnp.concatenate([w_in[:, o_q:o_fg], w_in[:, o_z:o_dt], small], axis=1).astype(BF16)

    q2, k2, v2, z2, xbc2, sm2 = _in_proj(x2, mod3, norm1_g, w_cat, seq_len)
    shp = lambda a: a.reshape(bsz, seq_len, a.shape[-1])

    tri_prep = jnp.tril(jnp.ones((PREP_TB, PREP_TB), F32)).astype(BF16)
    qa, ka, va = _fox_prep(shp(q2), shp(k2), shp(v2), shp(sm2), _lane_row(fg_bias, FG_LANE0), tri_prep)
    attn3 = _fox_attn(qa, ka, va, attn_norm_g.reshape(1, ATTN_DIM).astype(F32))

    tri_chunk = jnp.tril(jnp.ones((SSD_CHUNK, SSD_CHUNK), F32)).astype(BF16)
    head_of_lane = jnp.arange(SSD_DIM, dtype=I32) // SSD_HEAD_DIM
    expand = (jnp.arange(SMALL_W, dtype=I32)[:, None] == head_of_lane[None, :] + DT_LANE0).astype(BF16)
    dskip_row = jnp.repeat(d_skip.astype(F32), SSD_HEAD_DIM).reshape(1, SSD_DIM)
    ssd3 = _ssd(shp(xbc2), shp(z2), shp(sm2), conv_w.astype(F32), conv_b.reshape(1, -1).astype(F32),
                _lane_row(dt_bias, DT_LANE0), _lane_row(a_log, DT_LANE0), dskip_row,
                ssd_norm_g.reshape(1, SSD_DIM).astype(F32), tri_chunk, expand)

    upper = jnp.triu(jnp.ones((POST_TM, POST_TM), F32), 1).astype(BF16)
    x1s, h2p, eidx, wts, rank, cnt = _post_mixer(
        attn3.reshape(t, ATTN_DIM), ssd3.reshape(t, SSD_DIM), x2, mod3,
        w_out[:ATTN_DIM].astype(BF16), w_out[ATTN_DIM:].astype(BF16), norm2_g,
        w_router.T.astype(BF16), router_bias.reshape(N_EXPERTS, 1).astype(F32),
        w_gate_s.astype(BF16), w_up_s.astype(BF16), w_down_s.astype(BF16), upper, seq_len)

    counts = cnt[:, 0].astype(I32)
    padded = (counts + EXP_BLK - 1) // EXP_BLK * EXP_BLK
    pend = jnp.cumsum(padded)
    pstart = pend - padded
    n_blocks = (t * TOP_K) // EXP_BLK + N_EXPERTS
    block_start = jnp.arange(n_blocks, dtype=I32) * EXP_BLK
    block_expert = jnp.minimum(jnp.sum((pend[None, :] <= block_start[:, None]).astype(I32), axis=1),
                               N_EXPERTS - 1)
    n_used = (pend[-1] // EXP_BLK).astype(I32).reshape(1)
    dest = _dest_rows(eidx, rank, pstart)

    xs = _dispatch(dest, h2p, n_blocks * EXP_BLK)
    ys = _experts(block_expert, n_used, xs, w_gate_e, w_up_e, w_down_e)
    ytok = _undispatch(dest, ys, t)
    return ytok, x1s, wts


def kernel(x, c, norm1_g, w_ada, b_ada, w_in, fg_bias, conv_w, conv_b, dt_bias, a_log, d_skip, attn_norm_g,
           ssd_norm_g, w_out, norm2_g, w_router, router_bias, w_gate_e, w_up_e, w_down_e, w_gate_s, w_up_s,
           w_down_s, normf_g, w_ada_f, b_ada_f):
    bsz, seq_len, d = x.shape
    assert w_ada.shape[0] == 1, "single-layer kernel"
    t = bsz * seq_len
    mod3 = _modulation(c, w_ada[0], b_ada[0]).reshape(bsz * N_MOD, 1, d)
    modf3 = _modulation(c, w_ada_f, b_ada_f).reshape(bsz * 2, 1, d)
    x2 = x.reshape(t, d)
    ytok, x1s, wts = _layer(
        x2, mod3, bsz, seq_len, norm1_g[0], w_in[0], fg_bias[0], conv_w[0], conv_b[0], dt_bias[0], a_log[0],
        d_skip[0], attn_norm_g[0], ssd_norm_g[0], w_out[0], norm2_g[0], w_router[0], router_bias[0],
        w_gate_e[0], w_up_e[0], w_down_e[0], w_gate_s[0], w_up_s[0], w_down_s[0])
    out = _combine(ytok, x1s, wts.T, mod3, normf_g, modf3, seq_len)
    return out.reshape(bsz, seq_len, d)
```

```python
import functools

import jax
import jax.numpy as jnp
from jax import lax
from jax.experimental import pallas as pl
from jax.experimental.pallas import tpu as pltpu
from jax.experimental.pallas import tpu_sc as plsc

F32 = jnp.float32
BF16 = jnp.bfloat16
I32 = jnp.int32
U32 = jnp.uint32

EPS = 1e-6
D_MODEL = 1024
N_MOD = 6

ATTN_HEADS = 8
ATTN_HEAD_DIM = 64
ATTN_DIM = ATTN_HEADS * ATTN_HEAD_DIM

SSD_HEADS = 8
SSD_HEAD_DIM = 64
SSD_DIM = SSD_HEADS * SSD_HEAD_DIM
SSD_GROUPS = 2
SSD_STATE = 128
SSD_CONV = 4
SSD_CHUNK = 128
SSD_CONV_DIM = SSD_DIM + 2 * SSD_GROUPS * SSD_STATE
SSD_GROUP_DIM = SSD_DIM // SSD_GROUPS

N_EXPERTS = 256
TOP_K = 8
N_ROUTE_GROUPS = 8
TOPK_ROUTE_GROUPS = 4
GROUP_SIZE = N_EXPERTS // N_ROUTE_GROUPS
EXPERT_DIM = 256
ROUTED_SCALE = 2.5

LANES = 128
SMALL_W = LANES
FG_LANE0 = 0
DT_LANE0 = 8
AUG_W = LANES

IN_TM = 512
PREP_TB = 256
ATT_TQ = 512
ATT_TK = 512
POST_TM = 512
SC_CHUNK = 128
EXP_BLK = 512
COMB_TM = 256
VMEM_LIMIT = 56 * 1024 * 1024
NEG_BIG = -1e30
LOG2E = 1.4426950408889634
DEST_TM = 2048


def _split3(x):
    hi = x.astype(BF16)
    r1 = x - hi.astype(F32)
    mid = r1.astype(BF16)
    lo = (r1 - mid.astype(F32)).astype(BF16)
    return hi, mid, lo


def _dot(a, b):
    return jnp.dot(a, b, preferred_element_type=F32)


def _dot_nt(a, b):
    return lax.dot_general(a, b, (((1,), (1,)), ((), ())), preferred_element_type=F32)


def _dot_exact_lhs01(lhs_bf16, x, pieces=3):
    parts = _split3(x)[:pieces]
    out = _dot(lhs_bf16, parts[0])
    for p in parts[1:]:
        out = out + _dot(lhs_bf16, p)
    return out


def _dot_exact_rhs01(x, rhs_bf16, pieces=2):
    parts = _split3(x)[:pieces]
    out = _dot(parts[0], rhs_bf16)
    for p in parts[1:]:
        out = out + _dot(p, rhs_bf16)
    return out


def _sigmoid(x):
    return 1.0 / (1.0 + jnp.exp(-x))


def _silu(x):
    return x * _sigmoid(x)


def _softplus(x):
    return jnp.maximum(x, 0.0) + jnp.log(1.0 + jnp.exp(-jnp.abs(x)))


def _log_sigmoid(x):
    return jnp.minimum(x, 0.0) - jnp.log(1.0 + jnp.exp(-jnp.abs(x)))


def _pack_bf16_pair(x):
    n = x.shape[1] // 2
    lo = pltpu.bitcast(x[:, :n].astype(BF16).astype(F32), U32)
    hi = pltpu.bitcast(x[:, n:].astype(BF16).astype(F32), U32)
    return (hi & jnp.uint32(0xFFFF0000)) | (lo >> 16)


def _unpack_bf16_pair(w):
    lo = pltpu.bitcast(w << 16, F32)
    hi = pltpu.bitcast(w & jnp.uint32(0xFFFF0000), F32)
    return jnp.concatenate([lo, hi], axis=1)


def _cparams(sem):
    return pltpu.CompilerParams(dimension_semantics=sem, vmem_limit_bytes=VMEM_LIMIT)


def _mod_kernel(c_ref, w_ref, b_ref, o_ref):
    c = c_ref[...]
    o_ref[...] = jnp.dot(_silu(c), w_ref[...], preferred_element_type=F32,
                         precision=lax.Precision.HIGHEST) + b_ref[...]


def _modulation(c, w, b):
    bsz, d = c.shape
    n = w.shape[1]
    tn = 1024
    return pl.pallas_call(
        _mod_kernel,
        out_shape=jax.ShapeDtypeStruct((bsz, n), F32),
        grid=(n // tn,),
        in_specs=[pl.BlockSpec((bsz, d), lambda j: (0, 0)),
                  pl.BlockSpec((d, tn), lambda j: (0, j)),
                  pl.BlockSpec((1, tn), lambda j: (0, j))],
        out_specs=pl.BlockSpec((bsz, tn), lambda j: (0, j)),
        compiler_params=_cparams(("arbitrary",)),
        name="modulation",
    )(c, w, b.reshape(1, n))


_COL_Q, _COL_K, _COL_V, _COL_Z, _COL_XBC, _COL_SM, _COL_END = 0, 512, 1024, 1536, 2048, 3072, 3200


def _inproj_kernel(x_ref, g_ref, sc_ref, sh_ref, w_ref, q_ref, k_ref, v_ref, z_ref, xbc_ref, sm_ref):
    x = x_ref[...]
    var = jnp.mean(x * x, axis=-1, keepdims=True)
    h = x * lax.rsqrt(var + EPS) * g_ref[...]
    h = h * (1.0 + sc_ref[0]) + sh_ref[0]
    hb = h.astype(BF16)
    q_ref[...] = _dot(hb, w_ref[:, _COL_Q:_COL_K]).astype(BF16)
    k_ref[...] = _dot(hb, w_ref[:, _COL_K:_COL_V]).astype(BF16)
    v_ref[...] = _dot(hb, w_ref[:, _COL_V:_COL_Z]).astype(BF16)
    z_ref[...] = _dot(hb, w_ref[:, _COL_Z:_COL_XBC]).astype(BF16)
    xbc_ref[:, :512] = _dot(hb, w_ref[:, _COL_XBC:_COL_XBC + 512]).astype(BF16)
    xbc_ref[:, 512:] = _dot(hb, w_ref[:, _COL_XBC + 512:_COL_SM]).astype(BF16)
    sm_ref[...] = _dot(hb, w_ref[:, _COL_SM:_COL_END])


def _in_proj(x2, mod3, norm_g, w_cat, seq_len):
    t, d = x2.shape
    tm = IN_TM
    per_b = seq_len // tm
    row = lambda i: (i, 0)
    return pl.pallas_call(
        _inproj_kernel,
        out_shape=(jax.ShapeDtypeStruct((t, ATTN_DIM), BF16),) * 3
        + (jax.ShapeDtypeStruct((t, SSD_DIM), BF16),
           jax.ShapeDtypeStruct((t, SSD_CONV_DIM), BF16),
           jax.ShapeDtypeStruct((t, SMALL_W), F32)),
        grid=(t // tm,),
        in_specs=[pl.BlockSpec((tm, d), row),
                  pl.BlockSpec((1, d), lambda i: (0, 0)),
                  pl.BlockSpec((1, 1, d), lambda i: ((i // per_b) * N_MOD + 1, 0, 0)),
                  pl.BlockSpec((1, 1, d), lambda i: ((i // per_b) * N_MOD + 0, 0, 0)),
                  pl.BlockSpec((d, _COL_END), lambda i: (0, 0))],
        out_specs=(pl.BlockSpec((tm, ATTN_DIM), row),) * 3
        + (pl.BlockSpec((tm, SSD_DIM), row),
           pl.BlockSpec((tm, SSD_CONV_DIM), row),
           pl.BlockSpec((tm, SMALL_W), row)),
        compiler_params=_cparams(("arbitrary",)),
        name="in_proj",
    )(x2, norm_g.reshape(1, d), mod3, mod3, w_cat)


def _foxprep_kernel(q_ref, k_ref, v_ref, sm_ref, fgb_ref, tri_ref, qa_ref, ka_ref, va_ref, carry_ref):
    j = pl.program_id(1)

    @pl.when(j == 0)
    def _():
        carry_ref[...] = jnp.zeros_like(carry_ref)

    tb = sm_ref.shape[1]
    log_f = _log_sigmoid(sm_ref[0] + fgb_ref[...]) * LOG2E
    cum = _dot_exact_lhs01(tri_ref[...], log_f) + carry_ref[...]
    carry_ref[...] = cum[tb - 1:tb, :]
    hi, mid, lo = (p.astype(F32) for p in _split3(cum))
    qf = q_ref[0].astype(F32) * (ATTN_HEAD_DIM ** -0.5 * LOG2E)
    kf = k_ref[0].astype(F32)
    vf = v_ref[0].astype(F32)
    lane = lax.broadcasted_iota(I32, (tb, AUG_W - ATTN_HEAD_DIM), 1)
    aug_v = jnp.where(lane == 0, 1.0, 0.0)
    for h in range(ATTN_HEADS):
        c0 = FG_LANE0 + h
        chi, cmid, clo = hi[:, c0:c0 + 1], mid[:, c0:c0 + 1], lo[:, c0:c0 + 1]
        aug_q = jnp.where(lane == 0, chi, jnp.where(lane == 1, cmid, jnp.where(lane == 2, clo,
                          jnp.where(lane < 6, 1.0, 0.0))))
        aug_k = jnp.where(lane < 3, 1.0, jnp.where(lane == 3, -chi, jnp.where(lane == 4, -cmid,
                          jnp.where(lane == 5, -clo, 0.0))))
        sl = slice(h * ATTN_HEAD_DIM, (h + 1) * ATTN_HEAD_DIM)
        qa_ref[0, h] = jnp.concatenate([qf[:, sl], aug_q], axis=1).astype(BF16)
        ka_ref[0, h] = jnp.concatenate([kf[:, sl], aug_k], axis=1).astype(BF16)
        va_ref[0, h] = jnp.concatenate([vf[:, sl], aug_v], axis=1).astype(BF16)


def _fox_prep(q3, k3, v3, sm3, fgb_row, tri):
    bsz, seq_len, _ = q3.shape
    tb = PREP_TB
    blk = lambda b, j: (b, j, 0)
    aug = jax.ShapeDtypeStruct((bsz, ATTN_HEADS, seq_len, AUG_W), BF16)
    return pl.pallas_call(
        _foxprep_kernel,
        out_shape=(aug, aug, aug),
        grid=(bsz, seq_len // tb),
        in_specs=[pl.BlockSpec((1, tb, ATTN_DIM), blk),
                  pl.BlockSpec((1, tb, ATTN_DIM), blk),
                  pl.BlockSpec((1, tb, ATTN_DIM), blk),
                  pl.BlockSpec((1, tb, SMALL_W), blk),
                  pl.BlockSpec((1, SMALL_W), lambda b, j: (0, 0)),
                  pl.BlockSpec((tb, tb), lambda b, j: (0, 0))],
        out_specs=(pl.BlockSpec((1, ATTN_HEADS, tb, AUG_W), lambda b, j: (b, 0, j, 0)),) * 3,
        scratch_shapes=[pltpu.VMEM((1, SMALL_W), F32)],
        compiler_params=_cparams(("arbitrary", "arbitrary")),
        name="fox_prep",
    )(q3, k3, v3, sm3, fgb_row, tri)


def _attn_kernel(qa_ref, ka_ref, va_ref, g_ref, o_ref):
    i = pl.program_id(2)
    tq, tk = ATT_TQ, ATT_TK
    diff = lax.broadcasted_iota(I32, (tq, tk), 0) - lax.broadcasted_iota(I32, (tq, tk), 1)

    def chain_step(hh, state, j):
        m_old, acc = state
        off = pl.multiple_of(j * tk, tk)
        s = _dot_nt(qa_ref[0, hh], ka_ref[0, hh, pl.ds(off, tk), :])
        s = jnp.where(diff >= (j - i) * tk, s, NEG_BIG)
        m_new = jnp.maximum(m_old, jnp.max(s, axis=-1, keepdims=True))
        p = jnp.exp2(s - m_new).astype(BF16)
        acc = jnp.exp2(m_old - m_new) * acc + _dot(p, va_ref[0, hh, pl.ds(off, tk), :])
        return m_new, acc

    def body(j, states):
        return tuple(chain_step(hh, states[hh], j) for hh in range(2))

    init = tuple((jnp.full((tq, 1), NEG_BIG, F32), jnp.zeros((tq, LANES), F32)) for _ in range(2))
    states = lax.fori_loop(0, i + 1, body, init)
    outs = []
    sum_lane = lax.broadcasted_iota(I32, (tq, LANES), 1) == ATTN_HEAD_DIM
    for hh in range(2):
        acc = states[hh][1]
        row_sum = jnp.sum(jnp.where(sum_lane, acc, 0.0), axis=-1, keepdims=True)
        outs.append(acc / row_sum)
    lane = lax.broadcasted_iota(I32, (tq, LANES), 1)
    first = lane < ATTN_HEAD_DIM
    o = jnp.where(first, outs[0], pltpu.roll(outs[1], ATTN_HEAD_DIM, 1))
    sq = o * o
    s_all = jnp.sum(sq, axis=-1, keepdims=True)
    s0 = jnp.sum(jnp.where(first, sq, 0.0), axis=-1, keepdims=True)
    ms = jnp.where(first, s0, s_all - s0) * (1.0 / ATTN_HEAD_DIM)
    o_ref[0] = (o * lax.rsqrt(ms + EPS) * g_ref[...]).astype(BF16)


def _fox_attn(qa, ka, va, g_row):
    bsz, _, seq_len, _ = qa.shape
    tq = ATT_TQ
    kv_spec = pl.BlockSpec((1, 2, seq_len, AUG_W), lambda b, p, i: (b, p, 0, 0))
    return pl.pallas_call(
        _attn_kernel,
        out_shape=jax.ShapeDtypeStruct((bsz, seq_len, ATTN_DIM), BF16),
        grid=(bsz, ATTN_HEADS // 2, seq_len // tq),
        in_specs=[pl.BlockSpec((1, 2, tq, AUG_W), lambda b, p, i: (b, p, i, 0)),
                  kv_spec, kv_spec,
                  pl.BlockSpec((1, LANES), lambda b, p, i: (0, p))],
        out_specs=pl.BlockSpec((1, tq, LANES), lambda b, p, i: (b, i, p)),
        compiler_params=_cparams(("arbitrary", "arbitrary", "arbitrary")),
        name="fox_attn",
    )(qa, ka, va, g_row)


def _ssd_kernel(xbc_ref, z_ref, sm_ref, cw_ref, cb_ref, dtb_ref, alog_ref, dsk_ref, g_ref, tri_ref, exp_ref,
                o_ref, ext_ref, state_ref):
    c = pl.program_id(1)
    q = SSD_CHUNK

    @pl.when(c == 0)
    def _():
        ext_ref[0:8, :] = jnp.zeros((8, SSD_CONV_DIM), F32)
        state_ref[...] = jnp.zeros_like(state_ref)

    ext_ref[8:8 + q, :] = xbc_ref[0].astype(F32)
    conv = cb_ref[...] + cw_ref[0:1, :] * ext_ref[5:5 + q, :]
    for j in range(1, SSD_CONV):
        conv = conv + cw_ref[j:j + 1, :] * ext_ref[5 + j:5 + j + q, :]
    ext_ref[0:8, :] = ext_ref[q:q + 8, :]
    xc = _silu(conv)
    xs = xc[:, :SSD_DIM]

    dt = _softplus(sm_ref[0] + dtb_ref[...])
    a_dt = -jnp.exp(alog_ref[...]) * dt
    a_cs = _dot_exact_lhs01(tri_ref[...], a_dt)
    a_last = a_cs[q - 1:q, :]
    e_cs = jnp.exp(a_cs)
    dec = jnp.exp(a_last - a_cs)
    a_cs_t = a_cs.T
    expand = exp_ref[...]
    dt_x = _dot_exact_rhs01(dt, expand)
    e_x = _dot_exact_rhs01(e_cs, expand)
    dec_x = _dot_exact_rhs01(dec, expand)
    x_dt = xs * dt_x
    x_dec = (x_dt * dec_x).astype(BF16)
    x_dt_b = x_dt.astype(BF16)

    row = lax.broadcasted_iota(I32, (q, q), 0)
    col = lax.broadcasted_iota(I32, (q, q), 1)
    lower = row >= col
    lane = lax.broadcasted_iota(I32, (q, LANES), 1)
    first = lane < SSD_HEAD_DIM
    y_parts = []
    for g in range(SSD_GROUPS):
        b_g = xc[:, SSD_DIM + g * SSD_STATE:SSD_DIM + (g + 1) * SSD_STATE]
        c_g = xc[:, SSD_DIM + (SSD_GROUPS + g) * SSD_STATE:SSD_DIM + (SSD_GROUPS + g + 1) * SSD_STATE]
        c_gb = c_g.astype(BF16)
        cb = _dot_nt(c_gb, b_g.astype(BF16))
        gs = slice(g * SSD_GROUP_DIM, (g + 1) * SSD_GROUP_DIM)
        st_prev = state_ref[g]
        y_off = _dot(c_gb, st_prev.astype(BF16)) * e_x[:, gs]
        s_new = _dot(b_g.T.astype(BF16), x_dec[:, gs])
        state_ref[g] = st_prev * e_x[q - 1:q, gs] + s_new
        for pr in range(2):
            pair = []
            for hh in range(2):
                h = g * 4 + pr * 2 + hh
                a_col = a_cs[:, DT_LANE0 + h:DT_LANE0 + h + 1]
                a_row = a_cs_t[DT_LANE0 + h:DT_LANE0 + h + 1, :]
                lmat = jnp.where(lower, jnp.exp(jnp.minimum(a_col - a_row, 0.0)), 0.0)
                m_h = (cb * lmat).astype(BF16)
                ps = slice((g * 2 + pr) * LANES, (g * 2 + pr + 1) * LANES)
                pair.append(_dot(m_h, x_dt_b[:, ps]))
            y_parts.append(jnp.where(first, pair[0], pair[1]) + y_off[:, pr * LANES:(pr + 1) * LANES])
    y = jnp.concatenate(y_parts, axis=1) + dsk_ref[...] * xs
    y = y * _silu(z_ref[0].astype(F32))
    outs = []
    for g in range(SSD_GROUPS):
        yg = y[:, g * SSD_GROUP_DIM:(g + 1) * SSD_GROUP_DIM]
        ms = jnp.mean(yg * yg, axis=-1, keepdims=True)
        outs.append(yg * lax.rsqrt(ms + EPS))
    o_ref[0] = (jnp.concatenate(outs, axis=1) * g_ref[...]).astype(BF16)


def _ssd(xbc3, z3, sm3, conv_w, conv_b, dtb_row, alog_row, dskip_row, g_row, tri, expand):
    bsz, seq_len, _ = xbc3.shape
    q = SSD_CHUNK
    blk = lambda b, c: (b, c, 0)
    const = lambda b, c: (0, 0)
    return pl.pallas_call(
        _ssd_kernel,
        out_shape=jax.ShapeDtypeStruct((bsz, seq_len, SSD_DIM), BF16),
        grid=(bsz, seq_len // q),
        in_specs=[pl.BlockSpec((1, q, SSD_CONV_DIM), blk),
                  pl.BlockSpec((1, q, SSD_DIM), blk),
                  pl.BlockSpec((1, q, SMALL_W), blk),
                  pl.BlockSpec((SSD_CONV, SSD_CONV_DIM), const),
                  pl.BlockSpec((1, SSD_CONV_DIM), const),
                  pl.BlockSpec((1, SMALL_W), const),
                  pl.BlockSpec((1, SMALL_W), const),
                  pl.BlockSpec((1, SSD_DIM), const),
                  pl.BlockSpec((1, SSD_DIM), const),
                  pl.BlockSpec((q, q), const),
                  pl.BlockSpec((SMALL_W, SSD_DIM), const)],
        out_specs=pl.BlockSpec((1, q, SSD_DIM), blk),
        scratch_shapes=[pltpu.VMEM((q + 8, SSD_CONV_DIM), F32),
                        pltpu.VMEM((SSD_GROUPS, SSD_STATE, SSD_GROUP_DIM), F32)],
        compiler_params=_cparams(("arbitrary", "arbitrary")),
        name="ssd",
    )(xbc3, z3, sm3, conv_w, conv_b, dtb_row, alog_row, dskip_row, g_row, tri, expand)


def _post_kernel(attn_ref, ssd_ref, x_ref, g1_ref, sh2_ref, sc2_ref, g2_ref, woa_ref, wos_ref, n2_ref,
                 wrt_ref, rb_ref, wgs_ref, wus_ref, wds_ref, upper_ref,
                 x1s_ref, h2_ref, eidx_ref, wts_ref, rank_ref, cnt_ref):
    i = pl.program_id(0)
    tm = x_ref.shape[0]

    @pl.when(i == 0)
    def _():
        cnt_ref[...] = jnp.zeros_like(cnt_ref)

    mixed = _dot(attn_ref[...], woa_ref[...]) + _dot(ssd_ref[...], wos_ref[...])
    x1 = x_ref[...] + g1_ref[0] * mixed
    var = jnp.mean(x1 * x1, axis=-1, keepdims=True)
    h2 = x1 * lax.rsqrt(var + EPS) * n2_ref[...]
    h2 = h2 * (1.0 + sc2_ref[0]) + sh2_ref[0]
    hb = h2.astype(BF16)
    h2_ref[...] = _pack_bf16_pair(h2)
    act = _silu(_dot(hb, wgs_ref[...])) * _dot(hb, wus_ref[...])
    shared = _dot(act.astype(BF16), wds_ref[...])
    x1s_ref[...] = x1 + g2_ref[0] * shared

    scores = _sigmoid(_dot_nt(wrt_ref[...], hb))
    biased = scores + rb_ref[...]
    grp = biased.reshape(N_ROUTE_GROUPS, GROUP_SIZE, tm)
    gi = lax.broadcasted_iota(I32, grp.shape, 1)
    m1 = jnp.max(grp, axis=1, keepdims=True)
    i1 = jnp.min(jnp.where(grp == m1, gi, GROUP_SIZE), axis=1, keepdims=True)
    m2 = jnp.max(jnp.where(gi == i1, -jnp.inf, grp), axis=1, keepdims=True)
    gsc = (m1 + m2).reshape(N_ROUTE_GROUPS, tm)
    gidx = lax.broadcasted_iota(I32, gsc.shape, 0)
    beaten = jnp.zeros(gsc.shape, I32)
    for o in range(N_ROUTE_GROUPS):
        other = gsc[o:o + 1, :]
        beats = (other > gsc) | ((other == gsc) & (gidx > o))
        beaten = beaten + beats.astype(I32)
    gmask = (beaten < TOPK_ROUTE_GROUPS).astype(F32)
    emask = jnp.broadcast_to(gmask.reshape(N_ROUTE_GROUPS, 1, tm), grp.shape).reshape(N_EXPERTS, tm)
    masked = jnp.where(emask > 0.5, biased, -jnp.inf)
    eiota = lax.broadcasted_iota(I32, (N_EXPERTS, tm), 0)
    idx_rows, w_rows = [], []
    sel = jnp.zeros((N_EXPERTS, tm), F32)
    for _ in range(TOP_K):
        mk = jnp.max(masked, axis=0, keepdims=True)
        ik = jnp.min(jnp.where(masked == mk, eiota, N_EXPERTS), axis=0, keepdims=True)
        hit = eiota == ik
        w_rows.append(jnp.sum(jnp.where(hit, scores, 0.0), axis=0, keepdims=True))
        idx_rows.append(ik)
        masked = jnp.where(hit, -jnp.inf, masked)
        sel = jnp.where(hit, 1.0, sel)
    w_all = jnp.concatenate(w_rows, axis=0)
    wts_ref[...] = w_all / jnp.sum(w_all, axis=0, keepdims=True) * ROUTED_SCALE
    eidx_ref[...] = jnp.concatenate(idx_rows, axis=0)
    before = cnt_ref[...][:, 0:1] + _dot(sel.astype(BF16), upper_ref[...])
    rank_rows = [jnp.sum(jnp.where(eiota == ik, before, 0.0), axis=0, keepdims=True) for ik in idx_rows]
    rank_ref[...] = jnp.concatenate(rank_rows, axis=0).astype(I32)
    cnt_ref[...] = cnt_ref[...] + jnp.sum(sel, axis=1, keepdims=True)


def _post_mixer(attn2, ssd2, x2, mod3, w_out_a, w_out_s, norm2_g, w_router_t, rb_col,
                wgs, wus, wds, upper, seq_len):
    t, d = x2.shape
    tm = POST_TM
    per_b = seq_len // tm
    row = lambda i: (i, 0)
    const = lambda i: (0, 0)
    modspec = lambda k: pl.BlockSpec((1, 1, d), lambda i: ((i // per_b) * N_MOD + k, 0, 0))
    slot = lambda i: (0, i)
    return pl.pallas_call(
        _post_kernel,
        out_shape=(jax.ShapeDtypeStruct((t, d), F32),
                   jax.ShapeDtypeStruct((t, d // 2), U32),
                   jax.ShapeDtypeStruct((TOP_K, t), I32),
                   jax.ShapeDtypeStruct((TOP_K, t), F32),
                   jax.ShapeDtypeStruct((TOP_K, t), I32),
                   jax.ShapeDtypeStruct((N_EXPERTS, LANES), F32)),
        grid=(t // tm,),
        in_specs=[pl.BlockSpec((tm, ATTN_DIM), row),
                  pl.BlockSpec((tm, SSD_DIM), row),
                  pl.BlockSpec((tm, d), row),
                  modspec(2), modspec(3), modspec(4), modspec(5),
                  pl.BlockSpec((ATTN_DIM, d), const),
                  pl.BlockSpec((SSD_DIM, d), const),
                  pl.BlockSpec((1, d), const),
                  pl.BlockSpec((N_EXPERTS, d), const),
                  pl.BlockSpec((N_EXPERTS, 1), const),
                  pl.BlockSpec((d, EXPERT_DIM), const),
                  pl.BlockSpec((d, EXPERT_DIM), const),
                  pl.BlockSpec((EXPERT_DIM, d), const),
                  pl.BlockSpec((tm, tm), const)],
        out_specs=(pl.BlockSpec((tm, d), row),
                   pl.BlockSpec((tm, d // 2), row),
                   pl.BlockSpec((TOP_K, tm), slot),
                   pl.BlockSpec((TOP_K, tm), slot),
                   pl.BlockSpec((TOP_K, tm), slot),
                   pl.BlockSpec((N_EXPERTS, LANES), const)),
        compiler_params=_cparams(("arbitrary",)),
        name="post_mixer",
    )(attn2, ssd2, x2, mod3, mod3, mod3, mod3, w_out_a, w_out_s, norm2_g.reshape(1, d),
      w_router_t, rb_col, wgs, wus, wds, upper)


def _dest_kernel(eidx_ref, rank_ref, pst_ref, dest_ref):
    tm = eidx_ref.shape[1]
    eiota = lax.broadcasted_iota(I32, (N_EXPERTS, tm), 0)
    rows = []
    for k in range(TOP_K):
        onehot = jnp.where(eiota == eidx_ref[k:k + 1, :], 1.0, 0.0).astype(BF16)
        r = _dot(pst_ref[...], onehot)
        rows.append(r[0:1, :] + r[1:2, :] + r[2:3, :])
    dest_ref[...] = jnp.concatenate(rows, axis=0).astype(I32) + rank_ref[...]


def _dest_rows(eidx, rank, pstart):
    k, t = eidx.shape
    tm = min(DEST_TM, t)
    pieces = jnp.stack([pstart & 0xFF0000, pstart & 0xFF00, pstart & 0xFF], axis=0)
    pst = jnp.zeros((8, N_EXPERTS), F32).at[:3].set(pieces.astype(F32)).astype(BF16)
    slot = lambda i: (0, i)
    return pl.pallas_call(
        _dest_kernel,
        out_shape=jax.ShapeDtypeStruct((k, t), I32),
        grid=(t // tm,),
        in_specs=[pl.BlockSpec((k, tm), slot), pl.BlockSpec((k, tm), slot),
                  pl.BlockSpec((8, N_EXPERTS), lambda i: (0, 0))],
        out_specs=pl.BlockSpec((k, tm), slot),
        compiler_params=_cparams(("arbitrary",)),
        name="dest_rows",
    )(eidx, rank, pst)


def _sc_mesh():
    return plsc.VectorSubcoreMesh(core_axis_name="c", subcore_axis_name="s")


def _sc_worker(n_workers_per_core):
    return lax.axis_index("s") * n_workers_per_core + lax.axis_index("c")


def _dispatch(dest, h2p, n_rows):
    t, w = h2p.shape
    mesh = _sc_mesh()
    n_workers = mesh.num_cores * mesh.num_subcores
    per_w = t // n_workers
    ch = min(SC_CHUNK, per_w)

    def body(dest_hbm, h_hbm, xs_hbm, idx_v, rows_v, sem):
        base_w = _sc_worker(mesh.num_cores) * per_w

        @pl.loop(0, per_w // ch)
        def _(ci):
            base = pl.multiple_of(base_w + ci * ch, ch)
            pltpu.sync_copy(dest_hbm.at[:, pl.ds(base, ch)], idx_v)
            pltpu.sync_copy(h_hbm.at[pl.ds(base, ch)], rows_v)
            copies = [pltpu.async_copy(rows_v, xs_hbm.at[idx_v.at[k]], sem) for k in range(TOP_K)]
            for cp in copies:
                cp.wait()

    return pl.kernel(
        body,
        out_type=jax.ShapeDtypeStruct((n_rows, w), U32),
        mesh=mesh,
        scratch_types=[pltpu.VMEM((TOP_K, ch), I32), pltpu.VMEM((ch, w), U32), pltpu.SemaphoreType.DMA],
        name="dispatch",
    )(dest, h2p)


def _undispatch(dest, ys, t):
    w = ys.shape[1]
    mesh = _sc_mesh()
    n_workers = mesh.num_cores * mesh.num_subcores
    per_w = t // n_workers
    ch = min(SC_CHUNK, per_w)

    def body(dest_hbm, ys_hbm, ytok_hbm, idx_v, rows_v, sem):
        base_w = _sc_worker(mesh.num_cores) * per_w

        @pl.loop(0, per_w // ch)
        def _(ci):
            base = pl.multiple_of(base_w + ci * ch, ch)
            pltpu.sync_copy(dest_hbm.at[:, pl.ds(base, ch)], idx_v)
            for k in range(TOP_K):
                pltpu.async_copy(ys_hbm.at[idx_v.at[k]], rows_v, sem).wait()
                pltpu.sync_copy(rows_v, ytok_hbm.at[k, pl.ds(base, ch)])

    return pl.kernel(
        body,
        out_type=jax.ShapeDtypeStruct((TOP_K, t, w), U32),
        mesh=mesh,
        scratch_types=[pltpu.VMEM((TOP_K, ch), I32), pltpu.VMEM((ch, w), U32), pltpu.SemaphoreType.DMA],
        name="undispatch",
    )(dest, ys)


def _expert_kernel(be_ref, nu_ref, x_ref, wg_ref, wu_ref, wd_ref, y_ref, act_ref, wgb_ref, wub_ref, wdb_ref):
    b = pl.program_id(0)
    n_used = nu_ref[0]
    e_cur = be_ref[jnp.minimum(b, n_used - 1)]
    e_prev = be_ref[jnp.clip(b - 1, 0, n_used - 1)]
    e_prev2 = be_ref[jnp.clip(b - 2, 0, n_used - 1)]

    @pl.when((b == 0) | (e_cur != e_prev))
    def _():
        wgb_ref[...] = wg_ref[0].astype(BF16)
        wub_ref[...] = wu_ref[0].astype(BF16)

    @pl.when((b == 1) | (e_prev != e_prev2))
    def _():
        wdb_ref[...] = wd_ref[0].astype(BF16)

    def gate_up():
        x = _unpack_bf16_pair(x_ref[...]).astype(BF16)
        return (_silu(_dot(x, wgb_ref[...])) * _dot(x, wub_ref[...])).astype(BF16)

    def down():
        y_ref[...] = _pack_bf16_pair(_dot(act_ref[...], wdb_ref[...]))

    @pl.when(b == 0)
    def _():
        act_ref[...] = gate_up()

    @pl.when((b > 0) & (b < n_used))
    def _():
        down()
        act_ref[...] = gate_up()

    @pl.when(b == n_used)
    def _():
        down()


def _experts(block_expert, n_used, xs, wg, wu, wd):
    n_rows, w = xs.shape
    n_blocks = n_rows // EXP_BLK
    d, f = wg.shape[1], wg.shape[2]
    cur = lambda b, be, nu: (jnp.minimum(b, nu[0] - 1), 0)
    prev = lambda b, be, nu: (jnp.clip(b - 1, 0, nu[0] - 1), 0)
    wcur = lambda b, be, nu: (be[jnp.minimum(b, nu[0] - 1)], 0, 0)
    wprev = lambda b, be, nu: (be[jnp.clip(b - 1, 0, nu[0] - 1)], 0, 0)
    grid_spec = pltpu.PrefetchScalarGridSpec(
        num_scalar_prefetch=2,
        grid=(n_blocks + 1,),
        in_specs=[pl.BlockSpec((EXP_BLK, w), cur),
                  pl.BlockSpec((1, d, f), wcur),
                  pl.BlockSpec((1, d, f), wcur),
                  pl.BlockSpec((1, f, d), wprev)],
        out_specs=pl.BlockSpec((EXP_BLK, w), prev),
        scratch_shapes=[pltpu.VMEM((EXP_BLK, f), BF16), pltpu.VMEM((d, f), BF16), pltpu.VMEM((d, f), BF16),
                        pltpu.VMEM((f, d), BF16)],
    )
    return pl.pallas_call(
        _expert_kernel,
        out_shape=jax.ShapeDtypeStruct((n_rows, w), U32),
        grid_spec=grid_spec,
        compiler_params=_cparams(("arbitrary",)),
        name="experts",
    )(block_expert, n_used, xs, wg, wu, wd)


def _combine_kernel(ytok_ref, x1s_ref, w_ref, g2_ref, nf_ref, mf0_ref, mf1_ref, o_ref):
    w = w_ref[...]
    routed = w[:, 0:1] * _unpack_bf16_pair(ytok_ref[0])
    for k in range(1, TOP_K):
        routed = routed + w[:, k:k + 1] * _unpack_bf16_pair(ytok_ref[k])
    xo = x1s_ref[...] + g2_ref[0] * routed
    var = jnp.mean(xo * xo, axis=-1, keepdims=True)
    y = xo * lax.rsqrt(var + EPS) * nf_ref[...]
    o_ref[...] = y * (1.0 + mf1_ref[0]) + mf0_ref[0]


def _combine(ytok, x1s, wts_t, mod3, normf_g, modf3, seq_len):
    t, d = x1s.shape
    tm = COMB_TM
    per_b = seq_len // tm
    row = lambda i: (i, 0)
    return pl.pallas_call(
        _combine_kernel,
        out_shape=jax.ShapeDtypeStruct((t, d), F32),
        grid=(t // tm,),
        in_specs=[pl.BlockSpec((TOP_K, tm, d // 2), lambda i: (0, i, 0)),
                  pl.BlockSpec((tm, d), row),
                  pl.BlockSpec((tm, TOP_K), row),
                  pl.BlockSpec((1, 1, d), lambda i: ((i // per_b) * N_MOD + 5, 0, 0)),
                  pl.BlockSpec((1, d), lambda i: (0, 0)),
                  pl.BlockSpec((1, 1, d), lambda i: ((i // per_b) * 2 + 0, 0, 0)),
                  pl.BlockSpec((1, 1, d), lambda i: ((i // per_b) * 2 + 1, 0, 0))],
        out_specs=pl.BlockSpec((tm, d), row),
        compiler_params=_cparams(("arbitrary",)),
        name="combine",
    )(ytok, x1s, wts_t, mod3, normf_g.reshape(1, d), modf3, modf3)


def _lane_row(vec, lane0):
    return jnp.zeros((1, SMALL_W), F32).at[0, lane0:lane0 + vec.shape[0]].set(vec.astype(F32))


def _layer(x2, mod3, bsz, seq_len, norm1_g, w_in, fg_bias, conv_w, conv_b, dt_bias, a_log, d_skip,
           attn_norm_g, ssd_norm_g, w_out, norm2_g, w_router, router_bias,
           w_gate_e, w_up_e, w_down_e, w_gate_s, w_up_s, w_down_s):
    t, d = x2.shape
    o_q, o_k, o_v, o_fg = 0, ATTN_DIM, 2 * ATTN_DIM, 3 * ATTN_DIM
    o_z = o_fg + ATTN_HEADS
    o_xbc = o_z + SSD_DIM
    o_dt = o_xbc + SSD_CONV_DIM
    small = jnp.zeros((d, SMALL_W), F32)
    small = small.at[:, FG_LANE0:FG_LANE0 + ATTN_HEADS].set(w_in[:, o_fg:o_z])
    small = small.at[:, DT_LANE0:DT_LANE0 + SSD_HEADS].set(w_in[:, o_dt:o_dt + SSD_HEADS])
    w_cat = jnp.concatenate([w_in[:, o_q:o_fg], w_in[:, o_z:o_dt], small], axis=1).astype(BF16)

    q2, k2, v2, z2, xbc2, sm2 = _in_proj(x2, mod3, norm1_g, w_cat, seq_len)
    shp = lambda a: a.reshape(bsz, seq_len, a.shape[-1])

    tri_prep = jnp.tril(jnp.ones((PREP_TB, PREP_TB), F32)).astype(BF16)
    qa, ka, va = _fox_prep(shp(q2), shp(k2), shp(v2), shp(sm2), _lane_row(fg_bias, FG_LANE0), tri_prep)
    attn3 = _fox_attn(qa, ka, va, attn_norm_g.reshape(1, ATTN_DIM).astype(F32))

    tri_chunk = jnp.tril(jnp.ones((SSD_CHUNK, SSD_CHUNK), F32)).astype(BF16)
    head_of_lane = jnp.arange(SSD_DIM, dtype=I32) // SSD_HEAD_DIM
    expand = (jnp.arange(SMALL_W, dtype=I32)[:, None] == head_of_lane[None, :] + DT_LANE0).astype(BF16)
    dskip_row = jnp.repeat(d_skip.astype(F32), SSD_HEAD_DIM).reshape(1, SSD_DIM)
    ssd3 = _ssd(shp(xbc2), shp(z2), shp(sm2), conv_w.astype(F32), conv_b.reshape(1, -1).astype(F32),
                _lane_row(dt_bias, DT_LANE0), _lane_row(a_log, DT_LANE0), dskip_row,
                ssd_norm_g.reshape(1, SSD_DIM).astype(F32), tri_chunk, expand)

    upper = jnp.triu(jnp.ones((POST_TM, POST_TM), F32), 1).astype(BF16)
    x1s, h2p, eidx, wts, rank, cnt = _post_mixer(
        attn3.reshape(t, ATTN_DIM), ssd3.reshape(t, SSD_DIM), x2, mod3,
        w_out[:ATTN_DIM].astype(BF16), w_out[ATTN_DIM:].astype(BF16), norm2_g,
        w_router.T.astype(BF16), router_bias.reshape(N_EXPERTS, 1).astype(F32),
        w_gate_s.astype(BF16), w_up_s.astype(BF16), w_down_s.astype(BF16), upper, seq_len)

    counts = cnt[:, 0].astype(I32)
    padded = (counts + EXP_BLK - 1) // EXP_BLK * EXP_BLK
    pend = jnp.cumsum(padded)
    pstart = pend - padded
    n_blocks = (t * TOP_K) // EXP_BLK + N_EXPERTS
    block_start = jnp.arange(n_blocks, dtype=I32) * EXP_BLK
    block_expert = jnp.minimum(jnp.sum((pend[None, :] <= block_start[:, None]).astype(I32), axis=1),
                               N_EXPERTS - 1)
    n_used = (pend[-1] // EXP_BLK).astype(I32).reshape(1)
    dest = _dest_rows(eidx, rank, pstart)

    xs = _dispatch(dest, h2p, n_blocks * EXP_BLK)
    ys = _experts(block_expert, n_used, xs, w_gate_e, w_up_e, w_down_e)
    ytok = _undispatch(dest, ys, t)
    return ytok, x1s, wts


def kernel(x, c, norm1_g, w_ada, b_ada, w_in, fg_bias, conv_w, conv_b, dt_bias, a_log, d_skip, attn_norm_g,
           ssd_norm_g, w_out, norm2_g, w_router, router_bias, w_gate_e, w_up_e, w_down_e, w_gate_s, w_up_s,
           w_down_s, normf_g, w_ada_f, b_ada_f):
    bsz, seq_len, d = x.shape
    assert w_ada.shape[0] == 1, "single-layer kernel"
    t = bsz * seq_len
    mod3 = _modulation(c, w_ada[0], b_ada[0]).reshape(bsz * N_MOD, 1, d)
    modf3 = _modulation(c, w_ada_f, b_ada_f).reshape(bsz * 2, 1, d)
    x2 = x.reshape(t, d)
    ytok, x1s, wts = _layer(
        x2, mod3, bsz, seq_len, norm1_g[0], w_in[0], fg_bias[0], conv_w[0], conv_b[0], dt_bias[0], a_log[0],
        d_skip[0], attn_norm_g[0], ssd_norm_g[0], w_out[0], norm2_g[0], w_router[0], router_bias[0],
        w_gate_e[0], w_up_e[0], w_down_e[0], w_gate_s[0], w_up_s[0], w_down_s[0])
    out = _combine(ytok, x1s, wts.T, mod3, normf_g, modf3, seq_len)
    return out.reshape(bsz, seq_len, d)
```

```python
import functools

import jax
import jax.numpy as jnp
from jax import lax
from jax.experimental import pallas as pl
from jax.experimental.pallas import tpu as pltpu
from jax.experimental.pallas import tpu_sc as plsc

F32 = jnp.float32
BF16 = jnp.bfloat16
I32 = jnp.int32
U32 = jnp.uint32

EPS = 1e-6
D_MODEL = 1024
N_MOD = 6

ATTN_HEADS = 8
ATTN_HEAD_DIM = 64
ATTN_DIM = ATTN_HEADS * ATTN_HEAD_DIM

SSD_HEADS = 8
SSD_HEAD_DIM = 64
SSD_DIM = SSD_HEADS * SSD_HEAD_DIM
SSD_GROUPS = 2
SSD_STATE = 128
SSD_CONV = 4
SSD_CHUNK = 128
SSD_CONV_DIM = SSD_DIM + 2 * SSD_GROUPS * SSD_STATE
SSD_GROUP_DIM = SSD_DIM // SSD_GROUPS

N_EXPERTS = 256
TOP_K = 8
N_ROUTE_GROUPS = 8
TOPK_ROUTE_GROUPS = 4
GROUP_SIZE = N_EXPERTS // N_ROUTE_GROUPS
EXPERT_DIM = 256
ROUTED_SCALE = 2.5

LANES = 128
SMALL_W = LANES
FG_LANE0 = 0
DT_LANE0 = 8
AUG_W = LANES

IN_TM = 512
PREP_TB = 256
ATT_TQ = 512
ATT_TK = 512
POST_TM = 512
SC_CHUNK = 128
EXP_BLK = 512
COMB_TM = 256
COMB_PARTS = 4
VMEM_LIMIT = 56 * 1024 * 1024
NEG_BIG = -1e30
LOG2E = 1.4426950408889634
DEST_TM = 2048


def _split3(x):
    hi = x.astype(BF16)
    r1 = x - hi.astype(F32)
    mid = r1.astype(BF16)
    lo = (r1 - mid.astype(F32)).astype(BF16)
    return hi, mid, lo


def _dot(a, b):
    return jnp.dot(a, b, preferred_element_type=F32)


def _dot_nt(a, b):
    return lax.dot_general(a, b, (((1,), (1,)), ((), ())), preferred_element_type=F32)


def _dot_exact_lhs01(lhs_bf16, x, pieces=3):
    parts = _split3(x)[:pieces]
    out = _dot(lhs_bf16, parts[0])
    for p in parts[1:]:
        out = out + _dot(lhs_bf16, p)
    return out


def _dot_exact_rhs01(x, rhs_bf16, pieces=2):
    parts = _split3(x)[:pieces]
    out = _dot(parts[0], rhs_bf16)
    for p in parts[1:]:
        out = out + _dot(p, rhs_bf16)
    return out


def _sigmoid(x):
    return 1.0 / (1.0 + jnp.exp(-x))


def _silu(x):
    return x * _sigmoid(x)


def _softplus(x):
    return jnp.maximum(x, 0.0) + jnp.log(1.0 + jnp.exp(-jnp.abs(x)))


def _log_sigmoid(x):
    return jnp.minimum(x, 0.0) - jnp.log(1.0 + jnp.exp(-jnp.abs(x)))


def _pack_bf16_pair(x):
    n = x.shape[1] // 2
    lo = pltpu.bitcast(x[:, :n].astype(BF16).astype(F32), U32)
    hi = pltpu.bitcast(x[:, n:].astype(BF16).astype(F32), U32)
    return (hi & jnp.uint32(0xFFFF0000)) | (lo >> 16)


def _unpack_bf16_pair(w):
    lo = pltpu.bitcast(w << 16, F32)
    hi = pltpu.bitcast(w & jnp.uint32(0xFFFF0000), F32)
    return jnp.concatenate([lo, hi], axis=1)


def _cparams(sem):
    return pltpu.CompilerParams(dimension_semantics=sem, vmem_limit_bytes=VMEM_LIMIT)


def _mod_kernel(c_ref, w_ref, b_ref, o_ref):
    c = c_ref[...]
    o_ref[...] = jnp.dot(_silu(c), w_ref[...], preferred_element_type=F32,
                         precision=lax.Precision.HIGHEST) + b_ref[...]


def _modulation(c, w, b):
    bsz, d = c.shape
    n = w.shape[1]
    tn = 1024
    return pl.pallas_call(
        _mod_kernel,
        out_shape=jax.ShapeDtypeStruct((bsz, n), F32),
        grid=(n // tn,),
        in_specs=[pl.BlockSpec((bsz, d), lambda j: (0, 0)),
                  pl.BlockSpec((d, tn), lambda j: (0, j)),
                  pl.BlockSpec((1, tn), lambda j: (0, j))],
        out_specs=pl.BlockSpec((bsz, tn), lambda j: (0, j)),
        compiler_params=_cparams(("arbitrary",)),
        name="modulation",
    )(c, w, b.reshape(1, n))


_COL_Q, _COL_K, _COL_V, _COL_Z, _COL_XBC, _COL_SM, _COL_END = 0, 512, 1024, 1536, 2048, 3072, 3200


def _inproj_kernel(x_ref, g_ref, sc_ref, sh_ref, w_ref, q_ref, k_ref, v_ref, z_ref, xbc_ref, sm_ref):
    x = x_ref[...]
    var = jnp.mean(x * x, axis=-1, keepdims=True)
    h = x * lax.rsqrt(var + EPS) * g_ref[...]
    h = h * (1.0 + sc_ref[0]) + sh_ref[0]
    hb = h.astype(BF16)
    q_ref[...] = _dot(hb, w_ref[:, _COL_Q:_COL_K]).astype(BF16)
    k_ref[...] = _dot(hb, w_ref[:, _COL_K:_COL_V]).astype(BF16)
    v_ref[...] = _dot(hb, w_ref[:, _COL_V:_COL_Z]).astype(BF16)
    z_ref[...] = _dot(hb, w_ref[:, _COL_Z:_COL_XBC]).astype(BF16)
    xbc_ref[:, :512] = _dot(hb, w_ref[:, _COL_XBC:_COL_XBC + 512]).astype(BF16)
    xbc_ref[:, 512:] = _dot(hb, w_ref[:, _COL_XBC + 512:_COL_SM]).astype(BF16)
    sm_ref[...] = _dot(hb, w_ref[:, _COL_SM:_COL_END])


def _in_proj(x2, mod3, norm_g, w_cat, seq_len):
    t, d = x2.shape
    tm = IN_TM
    per_b = seq_len // tm
    row = lambda i: (i, 0)
    return pl.pallas_call(
        _inproj_kernel,
        out_shape=(jax.ShapeDtypeStruct((t, ATTN_DIM), BF16),) * 3
        + (jax.ShapeDtypeStruct((t, SSD_DIM), BF16),
           jax.ShapeDtypeStruct((t, SSD_CONV_DIM), BF16),
           jax.ShapeDtypeStruct((t, SMALL_W), F32)),
        grid=(t // tm,),
        in_specs=[pl.BlockSpec((tm, d), row),
                  pl.BlockSpec((1, d), lambda i: (0, 0)),
                  pl.BlockSpec((1, 1, d), lambda i: ((i // per_b) * N_MOD + 1, 0, 0)),
                  pl.BlockSpec((1, 1, d), lambda i: ((i // per_b) * N_MOD + 0, 0, 0)),
                  pl.BlockSpec((d, _COL_END), lambda i: (0, 0))],
        out_specs=(pl.BlockSpec((tm, ATTN_DIM), row),) * 3
        + (pl.BlockSpec((tm, SSD_DIM), row),
           pl.BlockSpec((tm, SSD_CONV_DIM), row),
           pl.BlockSpec((tm, SMALL_W), row)),
        compiler_params=_cparams(("arbitrary",)),
        name="in_proj",
    )(x2, norm_g.reshape(1, d), mod3, mod3, w_cat)


def _foxprep_kernel(q_ref, k_ref, v_ref, sm_ref, fgb_ref, tri_ref, qa_ref, ka_ref, va_ref, carry_ref):
    j = pl.program_id(1)

    @pl.when(j == 0)
    def _():
        carry_ref[...] = jnp.zeros_like(carry_ref)

    tb = sm_ref.shape[1]
    log_f = _log_sigmoid(sm_ref[0] + fgb_ref[...]) * LOG2E
    cum = _dot_exact_lhs01(tri_ref[...], log_f) + carry_ref[...]
    carry_ref[...] = cum[tb - 1:tb, :]
    hi, mid, lo = (p.astype(F32) for p in _split3(cum))
    qf = q_ref[0].astype(F32) * (ATTN_HEAD_DIM ** -0.5 * LOG2E)
    kf = k_ref[0].astype(F32)
    vf = v_ref[0].astype(F32)
    lane = lax.broadcasted_iota(I32, (tb, AUG_W - ATTN_HEAD_DIM), 1)
    aug_v = jnp.where(lane == 0, 1.0, 0.0)
    for h in range(ATTN_HEADS):
        c0 = FG_LANE0 + h
        chi, cmid, clo = hi[:, c0:c0 + 1], mid[:, c0:c0 + 1], lo[:, c0:c0 + 1]
        aug_q = jnp.where(lane == 0, chi, jnp.where(lane == 1, cmid, jnp.where(lane == 2, clo,
                          jnp.where(lane < 6, 1.0, 0.0))))
        aug_k = jnp.where(lane < 3, 1.0, jnp.where(lane == 3, -chi, jnp.where(lane == 4, -cmid,
                          jnp.where(lane == 5, -clo, 0.0))))
        sl = slice(h * ATTN_HEAD_DIM, (h + 1) * ATTN_HEAD_DIM)
        qa_ref[0, h] = jnp.concatenate([qf[:, sl], aug_q], axis=1).astype(BF16)
        ka_ref[0, h] = jnp.concatenate([kf[:, sl], aug_k], axis=1).astype(BF16)
        va_ref[0, h] = jnp.concatenate([vf[:, sl], aug_v], axis=1).astype(BF16)


def _fox_prep(q3, k3, v3, sm3, fgb_row, tri):
    bsz, seq_len, _ = q3.shape
    tb = PREP_TB
    blk = lambda b, j: (b, j, 0)
    aug = jax.ShapeDtypeStruct((bsz, ATTN_HEADS, seq_len, AUG_W), BF16)
    return pl.pallas_call(
        _foxprep_kernel,
        out_shape=(aug, aug, aug),
        grid=(bsz, seq_len // tb),
        in_specs=[pl.BlockSpec((1, tb, ATTN_DIM), blk),
                  pl.BlockSpec((1, tb, ATTN_DIM), blk),
                  pl.BlockSpec((1, tb, ATTN_DIM), blk),
                  pl.BlockSpec((1, tb, SMALL_W), blk),
                  pl.BlockSpec((1, SMALL_W), lambda b, j: (0, 0)),
                  pl.BlockSpec((tb, tb), lambda b, j: (0, 0))],
        out_specs=(pl.BlockSpec((1, ATTN_HEADS, tb, AUG_W), lambda b, j: (b, 0, j, 0)),) * 3,
        scratch_shapes=[pltpu.VMEM((1, SMALL_W), F32)],
        compiler_params=_cparams(("arbitrary", "arbitrary")),
        name="fox_prep",
    )(q3, k3, v3, sm3, fgb_row, tri)


def _attn_kernel(qa_ref, ka_ref, va_ref, g_ref, o_ref):
    i = pl.program_id(2)
    tq, tk = ATT_TQ, ATT_TK
    diff = lax.broadcasted_iota(I32, (tq, tk), 0) - lax.broadcasted_iota(I32, (tq, tk), 1)

    def chain_step(hh, state, j):
        m_old, acc = state
        off = pl.multiple_of(j * tk, tk)
        s = _dot_nt(qa_ref[0, hh], ka_ref[0, hh, pl.ds(off, tk), :])
        s = jnp.where(diff >= (j - i) * tk, s, NEG_BIG)
        m_new = jnp.maximum(m_old, jnp.max(s, axis=-1, keepdims=True))
        p = jnp.exp2(s - m_new).astype(BF16)
        acc = jnp.exp2(m_old - m_new) * acc + _dot(p, va_ref[0, hh, pl.ds(off, tk), :])
        return m_new, acc

    def body(j, states):
        return tuple(chain_step(hh, states[hh], j) for hh in range(2))

    init = tuple((jnp.full((tq, 1), NEG_BIG, F32), jnp.zeros((tq, LANES), F32)) for _ in range(2))
    states = lax.fori_loop(0, i + 1, body, init)
    outs = []
    sum_lane = lax.broadcasted_iota(I32, (tq, LANES), 1) == ATTN_HEAD_DIM
    for hh in range(2):
        acc = states[hh][1]
        row_sum = jnp.sum(jnp.where(sum_lane, acc, 0.0), axis=-1, keepdims=True)
        outs.append(acc / row_sum)
    lane = lax.broadcasted_iota(I32, (tq, LANES), 1)
    first = lane < ATTN_HEAD_DIM
    o = jnp.where(first, outs[0], pltpu.roll(outs[1], ATTN_HEAD_DIM, 1))
    sq = o * o
    s_all = jnp.sum(sq, axis=-1, keepdims=True)
    s0 = jnp.sum(jnp.where(first, sq, 0.0), axis=-1, keepdims=True)
    ms = jnp.where(first, s0, s_all - s0) * (1.0 / ATTN_HEAD_DIM)
    o_ref[0] = (o * lax.rsqrt(ms + EPS) * g_ref[...]).astype(BF16)


def _fox_attn(qa, ka, va, g_row):
    bsz, _, seq_len, _ = qa.shape
    tq = ATT_TQ
    kv_spec = pl.BlockSpec((1, 2, seq_len, AUG_W), lambda b, p, i: (b, p, 0, 0))
    return pl.pallas_call(
        _attn_kernel,
        out_shape=jax.ShapeDtypeStruct((bsz, seq_len, ATTN_DIM), BF16),
        grid=(bsz, ATTN_HEADS // 2, seq_len // tq),
        in_specs=[pl.BlockSpec((1, 2, tq, AUG_W), lambda b, p, i: (b, p, i, 0)),
                  kv_spec, kv_spec,
                  pl.BlockSpec((1, LANES), lambda b, p, i: (0, p))],
        out_specs=pl.BlockSpec((1, tq, LANES), lambda b, p, i: (b, i, p)),
        compiler_params=_cparams(("arbitrary", "arbitrary", "arbitrary")),
        name="fox_attn",
    )(qa, ka, va, g_row)


def _ssd_kernel(xbc_ref, z_ref, sm_ref, cw_ref, cb_ref, dtb_ref, alog_ref, dsk_ref, g_ref, tri_ref, exp_ref,
                o_ref, ext_ref, state_ref):
    c = pl.program_id(1)
    q = SSD_CHUNK

    @pl.when(c == 0)
    def _():
        ext_ref[0:8, :] = jnp.zeros((8, SSD_CONV_DIM), F32)
        state_ref[...] = jnp.zeros_like(state_ref)

    ext_ref[8:8 + q, :] = xbc_ref[0].astype(F32)
    conv = cb_ref[...] + cw_ref[0:1, :] * ext_ref[5:5 + q, :]
    for j in range(1, SSD_CONV):
        conv = conv + cw_ref[j:j + 1, :] * ext_ref[5 + j:5 + j + q, :]
    ext_ref[0:8, :] = ext_ref[q:q + 8, :]
    xc = _silu(conv)
    xs = xc[:, :SSD_DIM]

    dt = _softplus(sm_ref[0] + dtb_ref[...])
    a_dt = -jnp.exp(alog_ref[...]) * dt
    a_cs = _dot_exact_lhs01(tri_ref[...], a_dt)
    a_last = a_cs[q - 1:q, :]
    e_cs = jnp.exp(a_cs)
    dec = jnp.exp(a_last - a_cs)
    a_cs_t = a_cs.T
    expand = exp_ref[...]
    dt_x = _dot_exact_rhs01(dt, expand)
    e_x = _dot_exact_rhs01(e_cs, expand)
    dec_x = _dot_exact_rhs01(dec, expand)
    x_dt = xs * dt_x
    x_dec = (x_dt * dec_x).astype(BF16)
    x_dt_b = x_dt.astype(BF16)

    row = lax.broadcasted_iota(I32, (q, q), 0)
    col = lax.broadcasted_iota(I32, (q, q), 1)
    lower = row >= col
    lane = lax.broadcasted_iota(I32, (q, LANES), 1)
    first = lane < SSD_HEAD_DIM
    y_parts = []
    for g in range(SSD_GROUPS):
        b_g = xc[:, SSD_DIM + g * SSD_STATE:SSD_DIM + (g + 1) * SSD_STATE]
        c_g = xc[:, SSD_DIM + (SSD_GROUPS + g) * SSD_STATE:SSD_DIM + (SSD_GROUPS + g + 1) * SSD_STATE]
        c_gb = c_g.astype(BF16)
        cb = _dot_nt(c_gb, b_g.astype(BF16))
        gs = slice(g * SSD_GROUP_DIM, (g + 1) * SSD_GROUP_DIM)
        st_prev = state_ref[g]
        y_off = _dot(c_gb, st_prev.astype(BF16)) * e_x[:, gs]
        s_new = _dot(b_g.T.astype(BF16), x_dec[:, gs])
        state_ref[g] = st_prev * e_x[q - 1:q, gs] + s_new
        for pr in range(2):
            pair = []
            for hh in range(2):
                h = g * 4 + pr * 2 + hh
                a_col = a_cs[:, DT_LANE0 + h:DT_LANE0 + h + 1]
                a_row = a_cs_t[DT_LANE0 + h:DT_LANE0 + h + 1, :]
                lmat = jnp.where(lower, jnp.exp(jnp.minimum(a_col - a_row, 0.0)), 0.0)
                m_h = (cb * lmat).astype(BF16)
                ps = slice((g * 2 + pr) * LANES, (g * 2 + pr + 1) * LANES)
                pair.append(_dot(m_h, x_dt_b[:, ps]))
            y_parts.append(jnp.where(first, pair[0], pair[1]) + y_off[:, pr * LANES:(pr + 1) * LANES])
    y = jnp.concatenate(y_parts, axis=1) + dsk_ref[...] * xs
    y = y * _silu(z_ref[0].astype(F32))
    outs = []
    for g in range(SSD_GROUPS):
        yg = y[:, g * SSD_GROUP_DIM:(g + 1) * SSD_GROUP_DIM]
        ms = jnp.mean(yg * yg, axis=-1, keepdims=True)
        outs.append(yg * lax.rsqrt(ms + EPS))
    o_ref[0] = (jnp.concatenate(outs, axis=1) * g_ref[...]).astype(BF16)


def _ssd(xbc3, z3, sm3, conv_w, conv_b, dtb_row, alog_row, dskip_row, g_row, tri, expand):
    bsz, seq_len, _ = xbc3.shape
    q = SSD_CHUNK
    blk = lambda b, c: (b, c, 0)
    const = lambda b, c: (0, 0)
    return pl.pallas_call(
        _ssd_kernel,
        out_shape=jax.ShapeDtypeStruct((bsz, seq_len, SSD_DIM), BF16),
        grid=(bsz, seq_len // q),
        in_specs=[pl.BlockSpec((1, q, SSD_CONV_DIM), blk),
                  pl.BlockSpec((1, q, SSD_DIM), blk),
                  pl.BlockSpec((1, q, SMALL_W), blk),
                  pl.BlockSpec((SSD_CONV, SSD_CONV_DIM), const),
                  pl.BlockSpec((1, SSD_CONV_DIM), const),
                  pl.BlockSpec((1, SMALL_W), const),
                  pl.BlockSpec((1, SMALL_W), const),
                  pl.BlockSpec((1, SSD_DIM), const),
                  pl.BlockSpec((1, SSD_DIM), const),
                  pl.BlockSpec((q, q), const),
                  pl.BlockSpec((SMALL_W, SSD_DIM), const)],
        out_specs=pl.BlockSpec((1, q, SSD_DIM), blk),
        scratch_shapes=[pltpu.VMEM((q + 8, SSD_CONV_DIM), F32),
                        pltpu.VMEM((SSD_GROUPS, SSD_STATE, SSD_GROUP_DIM), F32)],
        compiler_params=_cparams(("arbitrary", "arbitrary")),
        name="ssd",
    )(xbc3, z3, sm3, conv_w, conv_b, dtb_row, alog_row, dskip_row, g_row, tri, expand)


def _post_kernel(attn_ref, ssd_ref, x_ref, g1_ref, sh2_ref, sc2_ref, g2_ref, woa_ref, wos_ref, n2_ref,
                 wrt_ref, rb_ref, wgs_ref, wus_ref, wds_ref, upper_ref,
                 x1s_ref, h2_ref, eidx_ref, wts_ref, rank_ref, cnt_ref):
    i = pl.program_id(0)
    tm = x_ref.shape[0]

    @pl.when(i == 0)
    def _():
        cnt_ref[...] = jnp.zeros_like(cnt_ref)

    mixed = _dot(attn_ref[...], woa_ref[...]) + _dot(ssd_ref[...], wos_ref[...])
    x1 = x_ref[...] + g1_ref[0] * mixed
    var = jnp.mean(x1 * x1, axis=-1, keepdims=True)
    h2 = x1 * lax.rsqrt(var + EPS) * n2_ref[...]
    h2 = h2 * (1.0 + sc2_ref[0]) + sh2_ref[0]
    hb = h2.astype(BF16)
    h2_ref[...] = _pack_bf16_pair(h2)
    act = _silu(_dot(hb, wgs_ref[...])) * _dot(hb, wus_ref[...])
    shared = _dot(act.astype(BF16), wds_ref[...])
    x1s_ref[...] = x1 + g2_ref[0] * shared

    scores = _sigmoid(_dot_nt(wrt_ref[...], hb))
    biased = scores + rb_ref[...]
    grp = biased.reshape(N_ROUTE_GROUPS, GROUP_SIZE, tm)
    gi = lax.broadcasted_iota(I32, grp.shape, 1)
    m1 = jnp.max(grp, axis=1, keepdims=True)
    i1 = jnp.min(jnp.where(grp == m1, gi, GROUP_SIZE), axis=1, keepdims=True)
    m2 = jnp.max(jnp.where(gi == i1, -jnp.inf, grp), axis=1, keepdims=True)
    gsc = (m1 + m2).reshape(N_ROUTE_GROUPS, tm)
    gidx = lax.broadcasted_iota(I32, gsc.shape, 0)
    beaten = jnp.zeros(gsc.shape, I32)
    for o in range(N_ROUTE_GROUPS):
        other = gsc[o:o + 1, :]
        beats = (other > gsc) | ((other == gsc) & (gidx > o))
        beaten = beaten + beats.astype(I32)
    gmask = (beaten < TOPK_ROUTE_GROUPS).astype(F32)
    emask = jnp.broadcast_to(gmask.reshape(N_ROUTE_GROUPS, 1, tm), grp.shape).reshape(N_EXPERTS, tm)
    masked = jnp.where(emask > 0.5, biased, -jnp.inf)
    eiota = lax.broadcasted_iota(I32, (N_EXPERTS, tm), 0)
    idx_rows, w_rows = [], []
    sel = jnp.zeros((N_EXPERTS, tm), F32)
    for _ in range(TOP_K):
        mk = jnp.max(masked, axis=0, keepdims=True)
        ik = jnp.min(jnp.where(masked == mk, eiota, N_EXPERTS), axis=0, keepdims=True)
        hit = eiota == ik
        w_rows.append(jnp.sum(jnp.where(hit, scores, 0.0), axis=0, keepdims=True))
        idx_rows.append(ik)
        masked = jnp.where(hit, -jnp.inf, masked)
        sel = jnp.where(hit, 1.0, sel)
    w_all = jnp.concatenate(w_rows, axis=0)
    wts_ref[...] = w_all / jnp.sum(w_all, axis=0, keepdims=True) * ROUTED_SCALE
    eidx_ref[...] = jnp.concatenate(idx_rows, axis=0)
    before = cnt_ref[...][:, 0:1] + _dot(sel.astype(BF16), upper_ref[...])
    rank_rows = [jnp.sum(jnp.where(eiota == ik, before, 0.0), axis=0, keepdims=True) for ik in idx_rows]
    rank_ref[...] = jnp.concatenate(rank_rows, axis=0).astype(I32)
    cnt_ref[...] = cnt_ref[...] + jnp.sum(sel, axis=1, keepdims=True)


def _post_mixer(attn2, ssd2, x2, mod3, w_out_a, w_out_s, norm2_g, w_router_t, rb_col,
                wgs, wus, wds, upper, seq_len):
    t, d = x2.shape
    tm = POST_TM
    per_b = seq_len // tm
    row = lambda i: (i, 0)
    const = lambda i: (0, 0)
    modspec = lambda k: pl.BlockSpec((1, 1, d), lambda i: ((i // per_b) * N_MOD + k, 0, 0))
    slot = lambda i: (0, i)
    return pl.pallas_call(
        _post_kernel,
        out_shape=(jax.ShapeDtypeStruct((t, d), F32),
                   jax.ShapeDtypeStruct((t, d // 2), U32),
                   jax.ShapeDtypeStruct((TOP_K, t), I32),
                   jax.ShapeDtypeStruct((TOP_K, t), F32),
                   jax.ShapeDtypeStruct((TOP_K, t), I32),
                   jax.ShapeDtypeStruct((N_EXPERTS, LANES), F32)),
        grid=(t // tm,),
        in_specs=[pl.BlockSpec((tm, ATTN_DIM), row),
                  pl.BlockSpec((tm, SSD_DIM), row),
                  pl.BlockSpec((tm, d), row),
                  modspec(2), modspec(3), modspec(4), modspec(5),
                  pl.BlockSpec((ATTN_DIM, d), const),
                  pl.BlockSpec((SSD_DIM, d), const),
                  pl.BlockSpec((1, d), const),
                  pl.BlockSpec((N_EXPERTS, d), const),
                  pl.BlockSpec((N_EXPERTS, 1), const),
                  pl.BlockSpec((d, EXPERT_DIM), const),
                  pl.BlockSpec((d, EXPERT_DIM), const),
                  pl.BlockSpec((EXPERT_DIM, d), const),
                  pl.BlockSpec((tm, tm), const)],
        out_specs=(pl.BlockSpec((tm, d), row),
                   pl.BlockSpec((tm, d // 2), row),
                   pl.BlockSpec((TOP_K, tm), slot),
                   pl.BlockSpec((TOP_K, tm), slot),
                   pl.BlockSpec((TOP_K, tm), slot),
                   pl.BlockSpec((N_EXPERTS, LANES), const)),
        compiler_params=_cparams(("arbitrary",)),
        name="post_mixer",
    )(attn2, ssd2, x2, mod3, mod3, mod3, mod3, w_out_a, w_out_s, norm2_g.reshape(1, d),
      w_router_t, rb_col, wgs, wus, wds, upper)


def _dest_kernel(eidx_ref, rank_ref, pst_ref, dest_ref):
    tm = eidx_ref.shape[1]
    eiota = lax.broadcasted_iota(I32, (N_EXPERTS, tm), 0)
    rows = []
    for k in range(TOP_K):
        onehot = jnp.where(eiota == eidx_ref[k:k + 1, :], 1.0, 0.0).astype(BF16)
        r = _dot(pst_ref[...], onehot)
        rows.append(r[0:1, :] + r[1:2, :] + r[2:3, :])
    dest_ref[...] = jnp.concatenate(rows, axis=0).astype(I32) + rank_ref[...]


def _dest_rows(eidx, rank, pstart):
    k, t = eidx.shape
    tm = min(DEST_TM, t)
    pieces = jnp.stack([pstart & 0xFF0000, pstart & 0xFF00, pstart & 0xFF], axis=0)
    pst = jnp.zeros((8, N_EXPERTS), F32).at[:3].set(pieces.astype(F32)).astype(BF16)
    slot = lambda i: (0, i)
    return pl.pallas_call(
        _dest_kernel,
        out_shape=jax.ShapeDtypeStruct((k, t), I32),
        grid=(t // tm,),
        in_specs=[pl.BlockSpec((k, tm), slot), pl.BlockSpec((k, tm), slot),
                  pl.BlockSpec((8, N_EXPERTS), lambda i: (0, 0))],
        out_specs=pl.BlockSpec((k, tm), slot),
        compiler_params=_cparams(("arbitrary",)),
        name="dest_rows",
    )(eidx, rank, pst)


def _sc_mesh():
    return plsc.VectorSubcoreMesh(core_axis_name="c", subcore_axis_name="s")


def _sc_worker(n_workers_per_core):
    return lax.axis_index("s") * n_workers_per_core + lax.axis_index("c")


def _dispatch(dest, h2p, n_rows):
    t, w = h2p.shape
    mesh = _sc_mesh()
    n_workers = mesh.num_cores * mesh.num_subcores
    per_w = t // n_workers
    ch = min(SC_CHUNK, per_w)

    def body(dest_hbm, h_hbm, xs_hbm, idx_v, rows_v, sem):
        base_w = _sc_worker(mesh.num_cores) * per_w

        @pl.loop(0, per_w // ch)
        def _(ci):
            base = pl.multiple_of(base_w + ci * ch, ch)
            pltpu.sync_copy(dest_hbm.at[:, pl.ds(base, ch)], idx_v)
            pltpu.sync_copy(h_hbm.at[pl.ds(base, ch)], rows_v)
            copies = [pltpu.async_copy(rows_v, xs_hbm.at[idx_v.at[k]], sem) for k in range(TOP_K)]
            for cp in copies:
                cp.wait()

    return pl.kernel(
        body,
        out_type=jax.ShapeDtypeStruct((n_rows, w), U32),
        mesh=mesh,
        scratch_types=[pltpu.VMEM((TOP_K, ch), I32), pltpu.VMEM((ch, w), U32), pltpu.SemaphoreType.DMA],
        name="dispatch",
    )(dest, h2p)


def _undispatch(dest, ys, t0, t):
    w = ys.shape[1]
    mesh = _sc_mesh()
    n_workers = mesh.num_cores * mesh.num_subcores
    per_w = t // n_workers
    ch = min(SC_CHUNK, per_w)

    def body(dest_hbm, ys_hbm, ytok_hbm, idx_v, rows_v, sem):
        base_w = _sc_worker(mesh.num_cores) * per_w

        @pl.loop(0, per_w // ch)
        def _(ci):
            base = pl.multiple_of(base_w + ci * ch, ch)
            pltpu.sync_copy(dest_hbm.at[:, pl.ds(t0 + base, ch)], idx_v)
            for k in range(TOP_K):
                pltpu.async_copy(ys_hbm.at[idx_v.at[k]], rows_v, sem).wait()
                pltpu.sync_copy(rows_v, ytok_hbm.at[k, pl.ds(base, ch)])

    return pl.kernel(
        body,
        out_type=jax.ShapeDtypeStruct((TOP_K, t, w), U32),
        mesh=mesh,
        scratch_types=[pltpu.VMEM((TOP_K, ch), I32), pltpu.VMEM((ch, w), U32), pltpu.SemaphoreType.DMA],
        name="undispatch",
    )(dest, ys)


def _expert_kernel(be_ref, nu_ref, x_ref, wg_ref, wu_ref, wd_ref, y_ref, act_ref, wgb_ref, wub_ref, wdb_ref):
    b = pl.program_id(0)
    n_used = nu_ref[0]
    e_cur = be_ref[jnp.minimum(b, n_used - 1)]
    e_prev = be_ref[jnp.clip(b - 1, 0, n_used - 1)]
    e_prev2 = be_ref[jnp.clip(b - 2, 0, n_used - 1)]

    @pl.when((b == 0) | (e_cur != e_prev))
    def _():
        wgb_ref[...] = wg_ref[0].astype(BF16)
        wub_ref[...] = wu_ref[0].astype(BF16)

    @pl.when((b == 1) | (e_prev != e_prev2))
    def _():
        wdb_ref[...] = wd_ref[0].astype(BF16)

    def gate_up():
        x = _unpack_bf16_pair(x_ref[...]).astype(BF16)
        return (_silu(_dot(x, wgb_ref[...])) * _dot(x, wub_ref[...])).astype(BF16)

    def down():
        y_ref[...] = _pack_bf16_pair(_dot(act_ref[...], wdb_ref[...]))

    @pl.when(b == 0)
    def _():
        act_ref[...] = gate_up()

    @pl.when((b > 0) & (b < n_used))
    def _():
        down()
        act_ref[...] = gate_up()

    @pl.when(b == n_used)
    def _():
        down()


def _experts(block_expert, n_used, xs, wg, wu, wd):
    n_rows, w = xs.shape
    n_blocks = n_rows // EXP_BLK
    d, f = wg.shape[1], wg.shape[2]
    cur = lambda b, be, nu: (jnp.minimum(b, nu[0] - 1), 0)
    prev = lambda b, be, nu: (jnp.clip(b - 1, 0, nu[0] - 1), 0)
    wcur = lambda b, be, nu: (be[jnp.minimum(b, nu[0] - 1)], 0, 0)
    wprev = lambda b, be, nu: (be[jnp.clip(b - 1, 0, nu[0] - 1)], 0, 0)
    grid_spec = pltpu.PrefetchScalarGridSpec(
        num_scalar_prefetch=2,
        grid=(n_blocks + 1,),
        in_specs=[pl.BlockSpec((EXP_BLK, w), cur),
                  pl.BlockSpec((1, d, f), wcur),
                  pl.BlockSpec((1, d, f), wcur),
                  pl.BlockSpec((1, f, d), wprev)],
        out_specs=pl.BlockSpec((EXP_BLK, w), prev),
        scratch_shapes=[pltpu.VMEM((EXP_BLK, f), BF16), pltpu.VMEM((d, f), BF16), pltpu.VMEM((d, f), BF16),
                        pltpu.VMEM((f, d), BF16)],
    )
    return pl.pallas_call(
        _expert_kernel,
        out_shape=jax.ShapeDtypeStruct((n_rows, w), U32),
        grid_spec=grid_spec,
        compiler_params=_cparams(("arbitrary",)),
        name="experts",
    )(block_expert, n_used, xs, wg, wu, wd)


def _combine_kernel(ytok_ref, x1s_ref, w_ref, g2_ref, nf_ref, mf0_ref, mf1_ref, o_ref):
    w = w_ref[...]
    routed = w[:, 0:1] * _unpack_bf16_pair(ytok_ref[0])
    for k in range(1, TOP_K):
        routed = routed + w[:, k:k + 1] * _unpack_bf16_pair(ytok_ref[k])
    xo = x1s_ref[...] + g2_ref[0] * routed
    var = jnp.mean(xo * xo, axis=-1, keepdims=True)
    y = xo * lax.rsqrt(var + EPS) * nf_ref[...]
    o_ref[...] = y * (1.0 + mf1_ref[0]) + mf0_ref[0]


def _combine_alias_kernel(prev_ref, *refs):
    del prev_ref
    _combine_kernel(*refs)


def _combine(ytok, x1s, wts_t, mod3, normf_g, modf3, seq_len, part, out_prev):
    t, d = x1s.shape
    tm = COMB_TM
    steps = ytok.shape[1] // tm
    i0 = part * steps
    per_b = seq_len // tm
    row = lambda i: (i0 + i, 0)
    in_specs = [pl.BlockSpec((TOP_K, tm, d // 2), lambda i: (0, i, 0)),
                pl.BlockSpec((tm, d), row),
                pl.BlockSpec((tm, TOP_K), row),
                pl.BlockSpec((1, 1, d), lambda i: (((i0 + i) // per_b) * N_MOD + 5, 0, 0)),
                pl.BlockSpec((1, d), lambda i: (0, 0)),
                pl.BlockSpec((1, 1, d), lambda i: (((i0 + i) // per_b) * 2 + 0, 0, 0)),
                pl.BlockSpec((1, 1, d), lambda i: (((i0 + i) // per_b) * 2 + 1, 0, 0))]
    args = (ytok, x1s, wts_t, mod3, normf_g.reshape(1, d), modf3, modf3)
    if out_prev is None:
        body, aliases = _combine_kernel, {}
    else:
        body, aliases = _combine_alias_kernel, {0: 0}
        in_specs = [pl.BlockSpec(memory_space=pl.ANY)] + in_specs
        args = (out_prev,) + args
    return pl.pallas_call(
        body,
        out_shape=jax.ShapeDtypeStruct((t, d), F32),
        grid=(steps,),
        in_specs=in_specs,
        out_specs=pl.BlockSpec((tm, d), row),
        input_output_aliases=aliases,
        compiler_params=_cparams(("arbitrary",)),
        name="combine",
    )(*args)


def _lane_row(vec, lane0):
    return jnp.zeros((1, SMALL_W), F32).at[0, lane0:lane0 + vec.shape[0]].set(vec.astype(F32))


def _layer(x2, mod3, bsz, seq_len, norm1_g, w_in, fg_bias, conv_w, conv_b, dt_bias, a_log, d_skip,
           attn_norm_g, ssd_norm_g, w_out, norm2_g, w_router, router_bias,
           w_gate_e, w_up_e, w_down_e, w_gate_s, w_up_s, w_down_s):
    t, d = x2.shape
    o_q, o_k, o_v, o_fg = 0, ATTN_DIM, 2 * ATTN_DIM, 3 * ATTN_DIM
    o_z = o_fg + ATTN_HEADS
    o_xbc = o_z + SSD_DIM
    o_dt = o_xbc + SSD_CONV_DIM
    small = jnp.zeros((d, SMALL_W), F32)
    small = small.at[:, FG_LANE0:FG_LANE0 + ATTN_HEADS].set(w_in[:, o_fg:o_z])
    small = small.at[:, DT_LANE0:DT_LANE0 + SSD_HEADS].set(w_in[:, o_dt:o_dt + SSD_HEADS])
    w_cat = jnp.concatenate([w_in[:, o_q:o_fg], w_in[:, o_z:o_dt], small], axis=1).astype(BF16)

    q2, k2, v2, z2, xbc2, sm2 = _in_proj(x2, mod3, norm1_g, w_cat, seq_len)
    shp = lambda a: a.reshape(bsz, seq_len, a.shape[-1])

    tri_prep = jnp.tril(jnp.ones((PREP_TB, PREP_TB), F32)).astype(BF16)
    qa, ka, va = _fox_prep(shp(q2), shp(k2), shp(v2), shp(sm2), _lane_row(fg_bias, FG_LANE0), tri_prep)
    attn3 = _fox_attn(qa, ka, va, attn_norm_g.reshape(1, ATTN_DIM).astype(F32))

    tri_chunk = jnp.tril(jnp.ones((SSD_CHUNK, SSD_CHUNK), F32)).astype(BF16)
    head_of_lane = jnp.arange(SSD_DIM, dtype=I32) // SSD_HEAD_DIM
    expand = (jnp.arange(SMALL_W, dtype=I32)[:, None] == head_of_lane[None, :] + DT_LANE0).astype(BF16)
    dskip_row = jnp.repeat(d_skip.astype(F32), SSD_HEAD_DIM).reshape(1, SSD_DIM)
    ssd3 = _ssd(shp(xbc2), shp(z2), shp(sm2), conv_w.astype(F32), conv_b.reshape(1, -1).astype(F32),
                _lane_row(dt_bias, DT_LANE0), _lane_row(a_log, DT_LANE0), dskip_row,
                ssd_norm_g.reshape(1, SSD_DIM).astype(F32), tri_chunk, expand)

    upper = jnp.triu(jnp.ones((POST_TM, POST_TM), F32), 1).astype(BF16)
    x1s, h2p, eidx, wts, rank, cnt = _post_mixer(
        attn3.reshape(t, ATTN_DIM), ssd3.reshape(t, SSD_DIM), x2, mod3,
        w_out[:ATTN_DIM].astype(BF16), w_out[ATTN_DIM:].astype(BF16), norm2_g,
        w_router.T.astype(BF16), router_bias.reshape(N_EXPERTS, 1).astype(F32),
        w_gate_s.astype(BF16), w_up_s.astype(BF16), w_down_s.astype(BF16), upper, seq_len)

    counts = cnt[:, 0].astype(I32)
    padded = (counts + EXP_BLK - 1) // EXP_BLK * EXP_BLK
    pend = jnp.cumsum(padded)
    pstart = pend - padded
    n_blocks = (t * TOP_K) // EXP_BLK + N_EXPERTS
    block_start = jnp.arange(n_blocks, dtype=I32) * EXP_BLK
    block_expert = jnp.minimum(jnp.sum((pend[None, :] <= block_start[:, None]).astype(I32), axis=1),
                               N_EXPERTS - 1)
    n_used = (pend[-1] // EXP_BLK).astype(I32).reshape(1)
    dest = _dest_rows(eidx, rank, pstart)

    xs = _dispatch(dest, h2p, n_blocks * EXP_BLK)
    ys = _experts(block_expert, n_used, xs, w_gate_e, w_up_e, w_down_e)
    return dest, ys, x1s, wts


def kernel(x, c, norm1_g, w_ada, b_ada, w_in, fg_bias, conv_w, conv_b, dt_bias, a_log, d_skip, attn_norm_g,
           ssd_norm_g, w_out, norm2_g, w_router, router_bias, w_gate_e, w_up_e, w_down_e, w_gate_s, w_up_s,
           w_down_s, normf_g, w_ada_f, b_ada_f):
    bsz, seq_len, d = x.shape
    assert w_ada.shape[0] == 1, "single-layer kernel"
    t = bsz * seq_len
    mod3 = _modulation(c, w_ada[0], b_ada[0]).reshape(bsz * N_MOD, 1, d)
    modf3 = _modulation(c, w_ada_f, b_ada_f).reshape(bsz * 2, 1, d)
    x2 = x.reshape(t, d)
    dest, ys, x1s, wts = _layer(
        x2, mod3, bsz, seq_len, norm1_g[0], w_in[0], fg_bias[0], conv_w[0], conv_b[0], dt_bias[0], a_log[0],
        d_skip[0], attn_norm_g[0], ssd_norm_g[0], w_out[0], norm2_g[0], w_router[0], router_bias[0],
        w_gate_e[0], w_up_e[0], w_down_e[0], w_gate_s[0], w_up_s[0], w_down_s[0])
    wts_t = wts.T
    tp = t // COMB_PARTS
    out = None
    for p in range(COMB_PARTS):
        ytok = _undispatch(dest, ys, p * tp, tp)
        out = _combine(ytok, x1s, wts_t, mod3, normf_g, modf3, seq_len, p, out)
    return out.reshape(bsz, seq_len, d)
```

```python
import functools

import jax
import jax.numpy as jnp
from jax import lax
from jax.experimental import pallas as pl
from jax.experimental.pallas import tpu as pltpu
from jax.experimental.pallas import tpu_sc as plsc

F32 = jnp.float32
BF16 = jnp.bfloat16
I32 = jnp.int32
U32 = jnp.uint32

EPS = 1e-6
D_MODEL = 1024
N_MOD = 6

ATTN_HEADS = 8
ATTN_HEAD_DIM = 64
ATTN_DIM = ATTN_HEADS * ATTN_HEAD_DIM

SSD_HEADS = 8
SSD_HEAD_DIM = 64
SSD_DIM = SSD_HEADS * SSD_HEAD_DIM
SSD_GROUPS = 2
SSD_STATE = 128
SSD_CONV = 4
SSD_CHUNK = 128
SSD_CONV_DIM = SSD_DIM + 2 * SSD_GROUPS * SSD_STATE
SSD_GROUP_DIM = SSD_DIM // SSD_GROUPS

N_EXPERTS = 256
TOP_K = 8
N_ROUTE_GROUPS = 8
TOPK_ROUTE_GROUPS = 4
GROUP_SIZE = N_EXPERTS // N_ROUTE_GROUPS
EXPERT_DIM = 256
ROUTED_SCALE = 2.5

LANES = 128
SMALL_W = LANES
FG_LANE0 = 0
DT_LANE0 = 8
AUG_W = LANES

IN_TM = 512
ATT_TQ = 512
ATT_TK = 512
POST_TM = 512
SC_CHUNK = 128
EXP_BLK = 1024
COMB_TM = 256
COMB_PARTS = 4
VMEM_LIMIT = 56 * 1024 * 1024
NEG_BIG = -1e30
LOG2E = 1.4426950408889634
DEST_TM = 2048


def _split3(x):
    hi = x.astype(BF16)
    r1 = x - hi.astype(F32)
    mid = r1.astype(BF16)
    lo = (r1 - mid.astype(F32)).astype(BF16)
    return hi, mid, lo


def _dot(a, b):
    return jnp.dot(a, b, preferred_element_type=F32)


def _dot_nt(a, b):
    return lax.dot_general(a, b, (((1,), (1,)), ((), ())), preferred_element_type=F32)


def _dot_exact_lhs01(lhs_bf16, x, pieces=3):
    parts = _split3(x)[:pieces]
    out = _dot(lhs_bf16, parts[0])
    for p in parts[1:]:
        out = out + _dot(lhs_bf16, p)
    return out


def _dot_exact_rhs01(x, rhs_bf16, pieces=2):
    parts = _split3(x)[:pieces]
    out = _dot(parts[0], rhs_bf16)
    for p in parts[1:]:
        out = out + _dot(p, rhs_bf16)
    return out


def _sigmoid(x):
    return 1.0 / (1.0 + jnp.exp(-x))


def _silu(x):
    return x * _sigmoid(x)


def _softplus(x):
    return jnp.maximum(x, 0.0) + jnp.log(1.0 + jnp.exp(-jnp.abs(x)))


def _log_sigmoid(x):
    return jnp.minimum(x, 0.0) - jnp.log(1.0 + jnp.exp(-jnp.abs(x)))


def _pack_bf16_pair(x):
    n = x.shape[1] // 2
    lo = pltpu.bitcast(x[:, :n].astype(BF16).astype(F32), U32)
    hi = pltpu.bitcast(x[:, n:].astype(BF16).astype(F32), U32)
    return (hi & jnp.uint32(0xFFFF0000)) | (lo >> 16)


def _unpack_bf16_pair(w):
    lo = pltpu.bitcast(w << 16, F32)
    hi = pltpu.bitcast(w & jnp.uint32(0xFFFF0000), F32)
    return jnp.concatenate([lo, hi], axis=1)


def _cparams(sem):
    return pltpu.CompilerParams(dimension_semantics=sem, vmem_limit_bytes=VMEM_LIMIT)


def _mod_kernel(c_ref, w_ref, b_ref, o_ref):
    c = c_ref[...]
    o_ref[...] = jnp.dot(_silu(c), w_ref[...], preferred_element_type=F32,
                         precision=lax.Precision.HIGHEST) + b_ref[...]


def _modulation(c, w, b):
    bsz, d = c.shape
    n = w.shape[1]
    tn = 1024
    return pl.pallas_call(
        _mod_kernel,
        out_shape=jax.ShapeDtypeStruct((bsz, n), F32),
        grid=(n // tn,),
        in_specs=[pl.BlockSpec((bsz, d), lambda j: (0, 0)),
                  pl.BlockSpec((d, tn), lambda j: (0, j)),
                  pl.BlockSpec((1, tn), lambda j: (0, j))],
        out_specs=pl.BlockSpec((bsz, tn), lambda j: (0, j)),
        compiler_params=_cparams(("arbitrary",)),
        name="modulation",
    )(c, w, b.reshape(1, n))


_COL_Q, _COL_K, _COL_V, _COL_Z, _COL_XBC, _COL_SM, _COL_END = 0, 512, 1024, 1536, 2048, 3072, 3200


def _inproj_kernel(per_b, x_ref, g_ref, sc_ref, sh_ref, w_ref, fgb_ref, tri_ref,
                   qa_ref, ka_ref, va_ref, z_ref, xbc_ref, sm_ref, carry_ref):
    tm = x_ref.shape[0]

    @pl.when(pl.program_id(0) % per_b == 0)
    def _():
        carry_ref[...] = jnp.zeros_like(carry_ref)

    x = x_ref[...]
    var = jnp.mean(x * x, axis=-1, keepdims=True)
    h = x * lax.rsqrt(var + EPS) * g_ref[...]
    h = h * (1.0 + sc_ref[0]) + sh_ref[0]
    hb = h.astype(BF16)
    sm = _dot(hb, w_ref[:, _COL_SM:_COL_END])
    sm_ref[...] = sm
    log_f = _log_sigmoid(sm + fgb_ref[...]) * LOG2E
    cum = _dot_exact_lhs01(tri_ref[...], log_f) + carry_ref[...]
    carry_ref[...] = cum[tm - 1:tm, :]
    hi, mid, lo = (p.astype(F32) for p in _split3(cum))
    lane = lax.broadcasted_iota(I32, (tm, AUG_W - ATTN_HEAD_DIM), 1)

    def head_rows(out_ref, col0, scale, aug_of_head):
        f = _dot(hb, w_ref[:, col0:col0 + ATTN_DIM])
        if scale is not None:
            f = f * scale
        for hd in range(ATTN_HEADS):
            sl = slice(hd * ATTN_HEAD_DIM, (hd + 1) * ATTN_HEAD_DIM)
            out_ref[0, hd] = jnp.concatenate([f[:, sl], aug_of_head(hd)], axis=1).astype(BF16)

    def cols(hd):
        c0 = FG_LANE0 + hd
        return hi[:, c0:c0 + 1], mid[:, c0:c0 + 1], lo[:, c0:c0 + 1]

    def aug_q(hd):
        chi, cmid, clo = cols(hd)
        return jnp.where(lane == 0, chi, jnp.where(lane == 1, cmid, jnp.where(lane == 2, clo,
                         jnp.where(lane < 6, 1.0, 0.0))))

    def aug_k(hd):
        chi, cmid, clo = cols(hd)
        return jnp.where(lane < 3, 1.0, jnp.where(lane == 3, -chi, jnp.where(lane == 4, -cmid,
                         jnp.where(lane == 5, -clo, 0.0))))

    ones_lane = jnp.where(lane == 0, 1.0, 0.0)
    head_rows(qa_ref, _COL_Q, ATTN_HEAD_DIM ** -0.5 * LOG2E, aug_q)
    head_rows(ka_ref, _COL_K, None, aug_k)
    head_rows(va_ref, _COL_V, None, lambda hd: ones_lane)
    z_ref[...] = _dot(hb, w_ref[:, _COL_Z:_COL_XBC]).astype(BF16)
    xbc_ref[:, :512] = _dot(hb, w_ref[:, _COL_XBC:_COL_XBC + 512]).astype(BF16)
    xbc_ref[:, 512:] = _dot(hb, w_ref[:, _COL_XBC + 512:_COL_SM]).astype(BF16)


def _in_proj(x2, mod3, norm_g, w_cat, fgb_row, tri, bsz, seq_len):
    t, d = x2.shape
    tm = IN_TM
    per_b = seq_len // tm
    row = lambda i: (i, 0)
    const = lambda i: (0, 0)
    aug = jax.ShapeDtypeStruct((bsz, ATTN_HEADS, seq_len, AUG_W), BF16)
    aug_spec = pl.BlockSpec((1, ATTN_HEADS, tm, AUG_W), lambda i: (i // per_b, 0, i % per_b, 0))
    return pl.pallas_call(
        functools.partial(_inproj_kernel, per_b),
        out_shape=(aug, aug, aug,
                   jax.ShapeDtypeStruct((t, SSD_DIM), BF16),
                   jax.ShapeDtypeStruct((t, SSD_CONV_DIM), BF16),
                   jax.ShapeDtypeStruct((t, SMALL_W), F32)),
        grid=(t // tm,),
        in_specs=[pl.BlockSpec((tm, d), row),
                  pl.BlockSpec((1, d), const),
                  pl.BlockSpec((1, 1, d), lambda i: ((i // per_b) * N_MOD + 1, 0, 0)),
                  pl.BlockSpec((1, 1, d), lambda i: ((i // per_b) * N_MOD + 0, 0, 0)),
                  pl.BlockSpec((d, _COL_END), const),
                  pl.BlockSpec((1, SMALL_W), const),
                  pl.BlockSpec((tm, tm), const)],
        out_specs=(aug_spec, aug_spec, aug_spec,
                   pl.BlockSpec((tm, SSD_DIM), row),
                   pl.BlockSpec((tm, SSD_CONV_DIM), row),
                   pl.BlockSpec((tm, SMALL_W), row)),
        scratch_shapes=[pltpu.VMEM((1, SMALL_W), F32)],
        compiler_params=_cparams(("arbitrary",)),
        name="in_proj",
    )(x2, norm_g.reshape(1, d), mod3, mod3, w_cat, fgb_row, tri)


def _attn_kernel(qa_ref, ka_ref, va_ref, g_ref, o_ref):
    i = pl.program_id(2)
    tq, tk = ATT_TQ, ATT_TK
    diff = lax.broadcasted_iota(I32, (tq, tk), 0) - lax.broadcasted_iota(I32, (tq, tk), 1)

    def chain_step(hh, state, j):
        m_old, acc = state
        off = pl.multiple_of(j * tk, tk)
        s = _dot_nt(qa_ref[0, hh], ka_ref[0, hh, pl.ds(off, tk), :])
        s = jnp.where(diff >= (j - i) * tk, s, NEG_BIG)
        m_new = jnp.maximum(m_old, jnp.max(s, axis=-1, keepdims=True))
        p = jnp.exp2(s - m_new).astype(BF16)
        acc = jnp.exp2(m_old - m_new) * acc + _dot(p, va_ref[0, hh, pl.ds(off, tk), :])
        return m_new, acc

    def body(j, states):
        return tuple(chain_step(hh, states[hh], j) for hh in range(2))

    init = tuple((jnp.full((tq, 1), NEG_BIG, F32), jnp.zeros((tq, LANES), F32)) for _ in range(2))
    states = lax.fori_loop(0, i + 1, body, init)
    outs = []
    sum_lane = lax.broadcasted_iota(I32, (tq, LANES), 1) == ATTN_HEAD_DIM
    for hh in range(2):
        acc = states[hh][1]
        row_sum = jnp.sum(jnp.where(sum_lane, acc, 0.0), axis=-1, keepdims=True)
        outs.append(acc / row_sum)
    lane = lax.broadcasted_iota(I32, (tq, LANES), 1)
    first = lane < ATTN_HEAD_DIM
    o = jnp.where(first, outs[0], pltpu.roll(outs[1], ATTN_HEAD_DIM, 1))
    sq = o * o
    s_all = jnp.sum(sq, axis=-1, keepdims=True)
    s0 = jnp.sum(jnp.where(first, sq, 0.0), axis=-1, keepdims=True)
    ms = jnp.where(first, s0, s_all - s0) * (1.0 / ATTN_HEAD_DIM)
    o_ref[0] = (o * lax.rsqrt(ms + EPS) * g_ref[...]).astype(BF16)


def _fox_attn(qa, ka, va, g_row):
    bsz, _, seq_len, _ = qa.shape
    tq = ATT_TQ
    kv_spec = pl.BlockSpec((1, 2, seq_len, AUG_W), lambda b, p, i: (b, p, 0, 0))
    return pl.pallas_call(
        _attn_kernel,
        out_shape=jax.ShapeDtypeStruct((bsz, seq_len, ATTN_DIM), BF16),
        grid=(bsz, ATTN_HEADS // 2, seq_len // tq),
        in_specs=[pl.BlockSpec((1, 2, tq, AUG_W), lambda b, p, i: (b, p, i, 0)),
                  kv_spec, kv_spec,
                  pl.BlockSpec((1, LANES), lambda b, p, i: (0, p))],
        out_specs=pl.BlockSpec((1, tq, LANES), lambda b, p, i: (b, i, p)),
        compiler_params=_cparams(("arbitrary", "arbitrary", "arbitrary")),
        name="fox_attn",
    )(qa, ka, va, g_row)


def _ssd_kernel(xbc_ref, z_ref, sm_ref, cw_ref, cb_ref, dtb_ref, alog_ref, dsk_ref, g_ref, tri_ref, exp_ref,
                o_ref, ext_ref, state_ref):
    c = pl.program_id(1)
    q = SSD_CHUNK

    @pl.when(c == 0)
    def _():
        ext_ref[0:8, :] = jnp.zeros((8, SSD_CONV_DIM), F32)
        state_ref[...] = jnp.zeros_like(state_ref)

    ext_ref[8:8 + q, :] = xbc_ref[0].astype(F32)
    conv = cb_ref[...] + cw_ref[0:1, :] * ext_ref[5:5 + q, :]
    for j in range(1, SSD_CONV):
        conv = conv + cw_ref[j:j + 1, :] * ext_ref[5 + j:5 + j + q, :]
    ext_ref[0:8, :] = ext_ref[q:q + 8, :]
    xc = _silu(conv)
    xs = xc[:, :SSD_DIM]

    dt = _softplus(sm_ref[0] + dtb_ref[...])
    a_dt = -jnp.exp(alog_ref[...]) * dt
    a_cs = _dot_exact_lhs01(tri_ref[...], a_dt)
    a_last = a_cs[q - 1:q, :]
    e_cs = jnp.exp(a_cs)
    dec = jnp.exp(a_last - a_cs)
    a_cs_t = a_cs.T
    expand = exp_ref[...]
    dt_x = _dot_exact_rhs01(dt, expand)
    e_x = _dot_exact_rhs01(e_cs, expand)
    dec_x = _dot_exact_rhs01(dec, expand)
    x_dt = xs * dt_x
    x_dec = (x_dt * dec_x).astype(BF16)
    x_dt_b = x_dt.astype(BF16)

    row = lax.broadcasted_iota(I32, (q, q), 0)
    col = lax.broadcasted_iota(I32, (q, q), 1)
    lower = row >= col
    lane = lax.broadcasted_iota(I32, (q, LANES), 1)
    first = lane < SSD_HEAD_DIM
    y_parts = []
    for g in range(SSD_GROUPS):
        b_g = xc[:, SSD_DIM + g * SSD_STATE:SSD_DIM + (g + 1) * SSD_STATE]
        c_g = xc[:, SSD_DIM + (SSD_GROUPS + g) * SSD_STATE:SSD_DIM + (SSD_GROUPS + g + 1) * SSD_STATE]
        c_gb = c_g.astype(BF16)
        cb = _dot_nt(c_gb, b_g.astype(BF16))
        gs = slice(g * SSD_GROUP_DIM, (g + 1) * SSD_GROUP_DIM)
        st_prev = state_ref[g]
        y_off = _dot(c_gb, st_prev.astype(BF16)) * e_x[:, gs]
        s_new = _dot(b_g.T.astype(BF16), x_dec[:, gs])
        state_ref[g] = st_prev * e_x[q - 1:q, gs] + s_new
        for pr in range(2):
            pair = []
            for hh in range(2):
                h = g * 4 + pr * 2 + hh
                a_col = a_cs[:, DT_LANE0 + h:DT_LANE0 + h + 1]
                a_row = a_cs_t[DT_LANE0 + h:DT_LANE0 + h + 1, :]
                lmat = jnp.where(lower, jnp.exp(jnp.minimum(a_col - a_row, 0.0)), 0.0)
                m_h = (cb * lmat).astype(BF16)
                ps = slice((g * 2 + pr) * LANES, (g * 2 + pr + 1) * LANES)
                pair.append(_dot(m_h, x_dt_b[:, ps]))
            y_parts.append(jnp.where(first, pair[0], pair[1]) + y_off[:, pr * LANES:(pr + 1) * LANES])
    y = jnp.concatenate(y_parts, axis=1) + dsk_ref[...] * xs
    y = y * _silu(z_ref[0].astype(F32))
    outs = []
    for g in range(SSD_GROUPS):
        yg = y[:, g * SSD_GROUP_DIM:(g + 1) * SSD_GROUP_DIM]
        ms = jnp.mean(yg * yg, axis=-1, keepdims=True)
        outs.append(yg * lax.rsqrt(ms + EPS))
    o_ref[0] = (jnp.concatenate(outs, axis=1) * g_ref[...]).astype(BF16)


def _ssd(xbc3, z3, sm3, conv_w, conv_b, dtb_row, alog_row, dskip_row, g_row, tri, expand):
    bsz, seq_len, _ = xbc3.shape
    q = SSD_CHUNK
    blk = lambda b, c: (b, c, 0)
    const = lambda b, c: (0, 0)
    return pl.pallas_call(
        _ssd_kernel,
        out_shape=jax.ShapeDtypeStruct((bsz, seq_len, SSD_DIM), BF16),
        grid=(bsz, seq_len // q),
        in_specs=[pl.BlockSpec((1, q, SSD_CONV_DIM), blk),
                  pl.BlockSpec((1, q, SSD_DIM), blk),
                  pl.BlockSpec((1, q, SMALL_W), blk),
                  pl.BlockSpec((SSD_CONV, SSD_CONV_DIM), const),
                  pl.BlockSpec((1, SSD_CONV_DIM), const),
                  pl.BlockSpec((1, SMALL_W), const),
                  pl.BlockSpec((1, SMALL_W), const),
                  pl.BlockSpec((1, SSD_DIM), const),
                  pl.BlockSpec((1, SSD_DIM), const),
                  pl.BlockSpec((q, q), const),
                  pl.BlockSpec((SMALL_W, SSD_DIM), const)],
        out_specs=pl.BlockSpec((1, q, SSD_DIM), blk),
        scratch_shapes=[pltpu.VMEM((q + 8, SSD_CONV_DIM), F32),
                        pltpu.VMEM((SSD_GROUPS, SSD_STATE, SSD_GROUP_DIM), F32)],
        compiler_params=_cparams(("arbitrary", "arbitrary")),
        name="ssd",
    )(xbc3, z3, sm3, conv_w, conv_b, dtb_row, alog_row, dskip_row, g_row, tri, expand)


def _post_kernel(attn_ref, ssd_ref, x_ref, g1_ref, sh2_ref, sc2_ref, g2_ref, woa_ref, wos_ref, n2_ref,
                 wrt_ref, rb_ref, wgs_ref, wus_ref, wds_ref, upper_ref,
                 x1s_ref, h2_ref, eidx_ref, wts_ref, rank_ref, cnt_ref):
    i = pl.program_id(0)
    tm = x_ref.shape[0]

    @pl.when(i == 0)
    def _():
        cnt_ref[...] = jnp.zeros_like(cnt_ref)

    mixed = _dot(attn_ref[...], woa_ref[...]) + _dot(ssd_ref[...], wos_ref[...])
    x1 = x_ref[...] + g1_ref[0] * mixed
    var = jnp.mean(x1 * x1, axis=-1, keepdims=True)
    h2 = x1 * lax.rsqrt(var + EPS) * n2_ref[...]
    h2 = h2 * (1.0 + sc2_ref[0]) + sh2_ref[0]
    hb = h2.astype(BF16)
    h2_ref[...] = _pack_bf16_pair(h2)
    act = _silu(_dot(hb, wgs_ref[...])) * _dot(hb, wus_ref[...])
    shared = _dot(act.astype(BF16), wds_ref[...])
    x1s_ref[...] = x1 + g2_ref[0] * shared

    scores = _sigmoid(_dot_nt(wrt_ref[...], hb))
    biased = scores + rb_ref[...]
    grp = biased.reshape(N_ROUTE_GROUPS, GROUP_SIZE, tm)
    gi = lax.broadcasted_iota(I32, grp.shape, 1)
    m1 = jnp.max(grp, axis=1, keepdims=True)
    i1 = jnp.min(jnp.where(grp == m1, gi, GROUP_SIZE), axis=1, keepdims=True)
    m2 = jnp.max(jnp.where(gi == i1, -jnp.inf, grp), axis=1, keepdims=True)
    gsc = (m1 + m2).reshape(N_ROUTE_GROUPS, tm)
    gidx = lax.broadcasted_iota(I32, gsc.shape, 0)
    beaten = jnp.zeros(gsc.shape, I32)
    for o in range(N_ROUTE_GROUPS):
        other = gsc[o:o + 1, :]
        beats = (other > gsc) | ((other == gsc) & (gidx > o))
        beaten = beaten + beats.astype(I32)
    gmask = (beaten < TOPK_ROUTE_GROUPS).astype(F32)
    emask = jnp.broadcast_to(gmask.reshape(N_ROUTE_GROUPS, 1, tm), grp.shape).reshape(N_EXPERTS, tm)
    masked = jnp.where(emask > 0.5, biased, -jnp.inf)
    eiota = lax.broadcasted_iota(I32, (N_EXPERTS, tm), 0)
    idx_rows, w_rows = [], []
    sel = jnp.zeros((N_EXPERTS, tm), F32)
    for _ in range(TOP_K):
        mk = jnp.max(masked, axis=0, keepdims=True)
        ik = jnp.min(jnp.where(masked == mk, eiota, N_EXPERTS), axis=0, keepdims=True)
        hit = eiota == ik
        w_rows.append(jnp.sum(jnp.where(hit, scores, 0.0), axis=0, keepdims=True))
        idx_rows.append(ik)
        masked = jnp.where(hit, -jnp.inf, masked)
        sel = jnp.where(hit, 1.0, sel)
    w_all = jnp.concatenate(w_rows, axis=0)
    wts_ref[...] = w_all / jnp.sum(w_all, axis=0, keepdims=True) * ROUTED_SCALE
    eidx_ref[...] = jnp.concatenate(idx_rows, axis=0)
    before = cnt_ref[...][:, 0:1] + _dot(sel.astype(BF16), upper_ref[...])
    rank_rows = [jnp.sum(jnp.where(eiota == ik, before, 0.0), axis=0, keepdims=True) for ik in idx_rows]
    rank_ref[...] = jnp.concatenate(rank_rows, axis=0).astype(I32)
    cnt_ref[...] = cnt_ref[...] + jnp.sum(sel, axis=1, keepdims=True)


def _post_mixer(attn2, ssd2, x2, mod3, w_out_a, w_out_s, norm2_g, w_router_t, rb_col,
                wgs, wus, wds, upper, seq_len):
    t, d = x2.shape
    tm = POST_TM
    per_b = seq_len // tm
    row = lambda i: (i, 0)
    const = lambda i: (0, 0)
    modspec = lambda k: pl.BlockSpec((1, 1, d), lambda i: ((i // per_b) * N_MOD + k, 0, 0))
    slot = lambda i: (0, i)
    return pl.pallas_call(
        _post_kernel,
        out_shape=(jax.ShapeDtypeStruct((t, d), F32),
                   jax.ShapeDtypeStruct((t, d // 2), U32),
                   jax.ShapeDtypeStruct((TOP_K, t), I32),
                   jax.ShapeDtypeStruct((TOP_K, t), F32),
                   jax.ShapeDtypeStruct((TOP_K, t), I32),
                   jax.ShapeDtypeStruct((N_EXPERTS, LANES), F32)),
        grid=(t // tm,),
        in_specs=[pl.BlockSpec((tm, ATTN_DIM), row),
                  pl.BlockSpec((tm, SSD_DIM), row),
                  pl.BlockSpec((tm, d), row),
                  modspec(2), modspec(3), modspec(4), modspec(5),
                  pl.BlockSpec((ATTN_DIM, d), const),
                  pl.BlockSpec((SSD_DIM, d), const),
                  pl.BlockSpec((1, d), const),
                  pl.BlockSpec((N_EXPERTS, d), const),
                  pl.BlockSpec((N_EXPERTS, 1), const),
                  pl.BlockSpec((d, EXPERT_DIM), const),
                  pl.BlockSpec((d, EXPERT_DIM), const),
                  pl.BlockSpec((EXPERT_DIM, d), const),
                  pl.BlockSpec((tm, tm), const)],
        out_specs=(pl.BlockSpec((tm, d), row),
                   pl.BlockSpec((tm, d // 2), row),
                   pl.BlockSpec((TOP_K, tm), slot),
                   pl.BlockSpec((TOP_K, tm), slot),
                   pl.BlockSpec((TOP_K, tm), slot),
                   pl.BlockSpec((N_EXPERTS, LANES), const)),
        compiler_params=_cparams(("arbitrary",)),
        name="post_mixer",
    )(attn2, ssd2, x2, mod3, mod3, mod3, mod3, w_out_a, w_out_s, norm2_g.reshape(1, d),
      w_router_t, rb_col, wgs, wus, wds, upper)


def _dest_kernel(eidx_ref, rank_ref, pst_ref, dest_ref):
    tm = eidx_ref.shape[1]
    eiota = lax.broadcasted_iota(I32, (N_EXPERTS, tm), 0)
    rows = []
    for k in range(TOP_K):
        onehot = jnp.where(eiota == eidx_ref[k:k + 1, :], 1.0, 0.0).astype(BF16)
        r = _dot(pst_ref[...], onehot)
        rows.append(r[0:1, :] + r[1:2, :] + r[2:3, :])
    dest_ref[...] = jnp.concatenate(rows, axis=0).astype(I32) + rank_ref[...]


def _dest_rows(eidx, rank, pstart):
    k, t = eidx.shape
    tm = min(DEST_TM, t)
    pieces = jnp.stack([pstart & 0xFF0000, pstart & 0xFF00, pstart & 0xFF], axis=0)
    pst = jnp.zeros((8, N_EXPERTS), F32).at[:3].set(pieces.astype(F32)).astype(BF16)
    slot = lambda i: (0, i)
    return pl.pallas_call(
        _dest_kernel,
        out_shape=jax.ShapeDtypeStruct((k, t), I32),
        grid=(t // tm,),
        in_specs=[pl.BlockSpec((k, tm), slot), pl.BlockSpec((k, tm), slot),
                  pl.BlockSpec((8, N_EXPERTS), lambda i: (0, 0))],
        out_specs=pl.BlockSpec((k, tm), slot),
        compiler_params=_cparams(("arbitrary",)),
        name="dest_rows",
    )(eidx, rank, pst)


def _sc_mesh():
    return plsc.VectorSubcoreMesh(core_axis_name="c", subcore_axis_name="s")


def _sc_worker(n_workers_per_core):
    return lax.axis_index("s") * n_workers_per_core + lax.axis_index("c")


def _dispatch(dest, h2p, n_rows):
    t, w = h2p.shape
    mesh = _sc_mesh()
    n_workers = mesh.num_cores * mesh.num_subcores
    per_w = t // n_workers
    ch = min(SC_CHUNK, per_w)

    def body(dest_hbm, h_hbm, xs_hbm, idx_v, rows_v, sem):
        base_w = _sc_worker(mesh.num_cores) * per_w

        @pl.loop(0, per_w // ch)
        def _(ci):
            base = pl.multiple_of(base_w + ci * ch, ch)
            pltpu.sync_copy(dest_hbm.at[:, pl.ds(base, ch)], idx_v)
            pltpu.sync_copy(h_hbm.at[pl.ds(base, ch)], rows_v)
            copies = [pltpu.async_copy(rows_v, xs_hbm.at[idx_v.at[k]], sem) for k in range(TOP_K)]
            for cp in copies:
                cp.wait()

    return pl.kernel(
        body,
        out_type=jax.ShapeDtypeStruct((n_rows, w), U32),
        mesh=mesh,
        scratch_types=[pltpu.VMEM((TOP_K, ch), I32), pltpu.VMEM((ch, w), U32), pltpu.SemaphoreType.DMA],
        name="dispatch",
    )(dest, h2p)


def _undispatch(dest, ys, t0, t):
    w = ys.shape[1]
    mesh = _sc_mesh()
    n_workers = mesh.num_cores * mesh.num_subcores
    per_w = t // n_workers
    ch = min(SC_CHUNK, per_w)

    def body(dest_hbm, ys_hbm, ytok_hbm, idx_v, rows_v, sem):
        base_w = _sc_worker(mesh.num_cores) * per_w

        @pl.loop(0, per_w // ch)
        def _(ci):
            base = pl.multiple_of(base_w + ci * ch, ch)
            pltpu.sync_copy(dest_hbm.at[:, pl.ds(t0 + base, ch)], idx_v)
            for k in range(TOP_K):
                pltpu.async_copy(ys_hbm.at[idx_v.at[k]], rows_v, sem).wait()
                pltpu.sync_copy(rows_v, ytok_hbm.at[k, pl.ds(base, ch)])

    return pl.kernel(
        body,
        out_type=jax.ShapeDtypeStruct((TOP_K, t, w), U32),
        mesh=mesh,
        scratch_types=[pltpu.VMEM((TOP_K, ch), I32), pltpu.VMEM((ch, w), U32), pltpu.SemaphoreType.DMA],
        name="undispatch",
    )(dest, ys)


def _expert_kernel(be_ref, nu_ref, x_ref, wg_ref, wu_ref, wd_ref, y_ref, act_ref, wgb_ref, wub_ref, wdb_ref):
    b = pl.program_id(0)
    n_used = nu_ref[0]
    e_cur = be_ref[jnp.minimum(b, n_used - 1)]
    e_prev = be_ref[jnp.clip(b - 1, 0, n_used - 1)]
    e_prev2 = be_ref[jnp.clip(b - 2, 0, n_used - 1)]

    @pl.when((b == 0) | (e_cur != e_prev))
    def _():
        wgb_ref[...] = wg_ref[0].astype(BF16)
        wub_ref[...] = wu_ref[0].astype(BF16)

    @pl.when((b == 1) | (e_prev != e_prev2))
    def _():
        wdb_ref[...] = wd_ref[0].astype(BF16)

    def gate_up():
        x = _unpack_bf16_pair(x_ref[...]).astype(BF16)
        return (_silu(_dot(x, wgb_ref[...])) * _dot(x, wub_ref[...])).astype(BF16)

    def down():
        y_ref[...] = _pack_bf16_pair(_dot(act_ref[...], wdb_ref[...]))

    @pl.when(b == 0)
    def _():
        act_ref[...] = gate_up()

    @pl.when((b > 0) & (b < n_used))
    def _():
        down()
        act_ref[...] = gate_up()

    @pl.when(b == n_used)
    def _():
        down()


def _experts(block_expert, n_used, xs, wg, wu, wd):
    n_rows, w = xs.shape
    n_blocks = n_rows // EXP_BLK
    d, f = wg.shape[1], wg.shape[2]
    cur = lambda b, be, nu: (jnp.minimum(b, nu[0] - 1), 0)
    prev = lambda b, be, nu: (jnp.clip(b - 1, 0, nu[0] - 1), 0)
    wcur = lambda b, be, nu: (be[jnp.minimum(b, nu[0] - 1)], 0, 0)
    wprev = lambda b, be, nu: (be[jnp.clip(b - 1, 0, nu[0] - 1)], 0, 0)
    grid_spec = pltpu.PrefetchScalarGridSpec(
        num_scalar_prefetch=2,
        grid=(n_blocks + 1,),
        in_specs=[pl.BlockSpec((EXP_BLK, w), cur),
                  pl.BlockSpec((1, d, f), wcur),
                  pl.BlockSpec((1, d, f), wcur),
                  pl.BlockSpec((1, f, d), wprev)],
        out_specs=pl.BlockSpec((EXP_BLK, w), prev),
        scratch_shapes=[pltpu.VMEM((EXP_BLK, f), BF16), pltpu.VMEM((d, f), BF16), pltpu.VMEM((d, f), BF16),
                        pltpu.VMEM((f, d), BF16)],
    )
    return pl.pallas_call(
        _expert_kernel,
        out_shape=jax.ShapeDtypeStruct((n_rows, w), U32),
        grid_spec=grid_spec,
        compiler_params=_cparams(("arbitrary",)),
        name="experts",
    )(block_expert, n_used, xs, wg, wu, wd)


def _combine_kernel(ytok_ref, x1s_ref, w_ref, g2_ref, nf_ref, mf0_ref, mf1_ref, o_ref):
    w = w_ref[...]
    routed = w[:, 0:1] * _unpack_bf16_pair(ytok_ref[0])
    for k in range(1, TOP_K):
        routed = routed + w[:, k:k + 1] * _unpack_bf16_pair(ytok_ref[k])
    xo = x1s_ref[...] + g2_ref[0] * routed
    var = jnp.mean(xo * xo, axis=-1, keepdims=True)
    y = xo * lax.rsqrt(var + EPS) * nf_ref[...]
    o_ref[...] = y * (1.0 + mf1_ref[0]) + mf0_ref[0]


def _combine_alias_kernel(prev_ref, *refs):
    del prev_ref
    _combine_kernel(*refs)


def _combine(ytok, x1s, wts_t, mod3, normf_g, modf3, seq_len, part, out_prev):
    t, d = x1s.shape
    tm = COMB_TM
    steps = ytok.shape[1] // tm
    i0 = part * steps
    per_b = seq_len // tm
    row = lambda i: (i0 + i, 0)
    in_specs = [pl.BlockSpec((TOP_K, tm, d // 2), lambda i: (0, i, 0)),
                pl.BlockSpec((tm, d), row),
                pl.BlockSpec((tm, TOP_K), row),
                pl.BlockSpec((1, 1, d), lambda i: (((i0 + i) // per_b) * N_MOD + 5, 0, 0)),
                pl.BlockSpec((1, d), lambda i: (0, 0)),
                pl.BlockSpec((1, 1, d), lambda i: (((i0 + i) // per_b) * 2 + 0, 0, 0)),
                pl.BlockSpec((1, 1, d), lambda i: (((i0 + i) // per_b) * 2 + 1, 0, 0))]
    args = (ytok, x1s, wts_t, mod3, normf_g.reshape(1, d), modf3, modf3)
    if out_prev is None:
        body, aliases = _combine_kernel, {}
    else:
        body, aliases = _combine_alias_kernel, {0: 0}
        in_specs = [pl.BlockSpec(memory_space=pl.ANY)] + in_specs
        args = (out_prev,) + args
    return pl.pallas_call(
        body,
        out_shape=jax.ShapeDtypeStruct((t, d), F32),
        grid=(steps,),
        in_specs=in_specs,
        out_specs=pl.BlockSpec((tm, d), row),
        input_output_aliases=aliases,
        compiler_params=_cparams(("arbitrary",)),
        name="combine",
    )(*args)


def _lane_row(vec, lane0):
    return jnp.zeros((1, SMALL_W), F32).at[0, lane0:lane0 + vec.shape[0]].set(vec.astype(F32))


def _layer(x2, mod3, bsz, seq_len, norm1_g, w_in, fg_bias, conv_w, conv_b, dt_bias, a_log, d_skip,
           attn_norm_g, ssd_norm_g, w_out, norm2_g, w_router, router_bias,
           w_gate_e, w_up_e, w_down_e, w_gate_s, w_up_s, w_down_s):
    t, d = x2.shape
    o_q, o_k, o_v, o_fg = 0, ATTN_DIM, 2 * ATTN_DIM, 3 * ATTN_DIM
    o_z = o_fg + ATTN_HEADS
    o_xbc = o_z + SSD_DIM
    o_dt = o_xbc + SSD_CONV_DIM
    small = jnp.zeros((d, SMALL_W), F32)
    small = small.at[:, FG_LANE0:FG_LANE0 + ATTN_HEADS].set(w_in[:, o_fg:o_z])
    small = small.at[:, DT_LANE0:DT_LANE0 + SSD_HEADS].set(w_in[:, o_dt:o_dt + SSD_HEADS])
    w_cat = jnp.concatenate([w_in[:, o_q:o_fg], w_in[:, o_z:o_dt], small], axis=1).astype(BF16)

    tri_in = jnp.tril(jnp.ones((IN_TM, IN_TM), F32)).astype(BF16)
    qa, ka, va, z2, xbc2, sm2 = _in_proj(x2, mod3, norm1_g, w_cat, _lane_row(fg_bias, FG_LANE0), tri_in,
                                         bsz, seq_len)
    shp = lambda a: a.reshape(bsz, seq_len, a.shape[-1])
    attn3 = _fox_attn(qa, ka, va, attn_norm_g.reshape(1, ATTN_DIM).astype(F32))

    tri_chunk = jnp.tril(jnp.ones((SSD_CHUNK, SSD_CHUNK), F32)).astype(BF16)
    head_of_lane = jnp.arange(SSD_DIM, dtype=I32) // SSD_HEAD_DIM
    expand = (jnp.arange(SMALL_W, dtype=I32)[:, None] == head_of_lane[None, :] + DT_LANE0).astype(BF16)
    dskip_row = jnp.repeat(d_skip.astype(F32), SSD_HEAD_DIM).reshape(1, SSD_DIM)
    ssd3 = _ssd(shp(xbc2), shp(z2), shp(sm2), conv_w.astype(F32), conv_b.reshape(1, -1).astype(F32),
                _lane_row(dt_bias, DT_LANE0), _lane_row(a_log, DT_LANE0), dskip_row,
                ssd_norm_g.reshape(1, SSD_DIM).astype(F32), tri_chunk, expand)

    upper = jnp.triu(jnp.ones((POST_TM, POST_TM), F32), 1).astype(BF16)
    x1s, h2p, eidx, wts, rank, cnt = _post_mixer(
        attn3.reshape(t, ATTN_DIM), ssd3.reshape(t, SSD_DIM), x2, mod3,
        w_out[:ATTN_DIM].astype(BF16), w_out[ATTN_DIM:].astype(BF16), norm2_g,
        w_router.T.astype(BF16), router_bias.reshape(N_EXPERTS, 1).astype(F32),
        w_gate_s.astype(BF16), w_up_s.astype(BF16), w_down_s.astype(BF16), upper, seq_len)

    counts = cnt[:, 0].astype(I32)
    padded = (counts + EXP_BLK - 1) // EXP_BLK * EXP_BLK
    pend = jnp.cumsum(padded)
    pstart = pend - padded
    n_blocks = (t * TOP_K) // EXP_BLK + N_EXPERTS
    block_start = jnp.arange(n_blocks, dtype=I32) * EXP_BLK
    block_expert = jnp.minimum(jnp.sum((pend[None, :] <= block_start[:, None]).astype(I32), axis=1),
                               N_EXPERTS - 1)
    n_used = (pend[-1] // EXP_BLK).astype(I32).reshape(1)
    dest = _dest_rows(eidx, rank, pstart)

    xs = _dispatch(dest, h2p, n_blocks * EXP_BLK)
    ys = _experts(block_expert, n_used, xs, w_gate_e, w_up_e, w_down_e)
    return dest, ys, x1s, wts


def kernel(x, c, norm1_g, w_ada, b_ada, w_in, fg_bias, conv_w, conv_b, dt_bias, a_log, d_skip, attn_norm_g,
           ssd_norm_g, w_out, norm2_g, w_router, router_bias, w_gate_e, w_up_e, w_down_e, w_gate_s, w_up_s,
           w_down_s, normf_g, w_ada_f, b_ada_f):
    bsz, seq_len, d = x.shape
    assert w_ada.shape[0] == 1, "single-layer kernel"
    t = bsz * seq_len
    mod3 = _modulation(c, w_ada[0], b_ada[0]).reshape(bsz * N_MOD, 1, d)
    modf3 = _modulation(c, w_ada_f, b_ada_f).reshape(bsz * 2, 1, d)
    x2 = x.reshape(t, d)
    dest, ys, x1s, wts = _layer(
        x2, mod3, bsz, seq_len, norm1_g[0], w_in[0], fg_bias[0], conv_w[0], conv_b[0], dt_bias[0], a_log[0],
        d_skip[0], attn_norm_g[0], ssd_norm_g[0], w_out[0], norm2_g[0], w_router[0], router_bias[0],
        w_gate_e[0], w_up_e[0], w_down_e[0], w_gate_s[0], w_up_s[0], w_down_s[0])
    wts_t = wts.T
    tp = t // COMB_PARTS
    out = None
    for p in range(COMB_PARTS):
        ytok = _undispatch(dest, ys, p * tp, tp)
        out = _combine(ytok, x1s, wts_t, mod3, normf_g, modf3, seq_len, p, out)
    return out.reshape(bsz, seq_len, d)
```

```python
import functools

import jax
import jax.numpy as jnp
from jax import lax
from jax.experimental import pallas as pl
from jax.experimental.pallas import tpu as pltpu
from jax.experimental.pallas import tpu_sc as plsc

F32 = jnp.float32
BF16 = jnp.bfloat16
I32 = jnp.int32
U32 = jnp.uint32

EPS = 1e-6
D_MODEL = 1024
N_MOD = 6

ATTN_HEADS = 8
ATTN_HEAD_DIM = 64
ATTN_DIM = ATTN_HEADS * ATTN_HEAD_DIM

SSD_HEADS = 8
SSD_HEAD_DIM = 64
SSD_DIM = SSD_HEADS * SSD_HEAD_DIM
SSD_GROUPS = 2
SSD_STATE = 128
SSD_CONV = 4
SSD_CHUNK = 128
SSD_CONV_DIM = SSD_DIM + 2 * SSD_GROUPS * SSD_STATE
SSD_GROUP_DIM = SSD_DIM // SSD_GROUPS

N_EXPERTS = 256
TOP_K = 8
N_ROUTE_GROUPS = 8
TOPK_ROUTE_GROUPS = 4
GROUP_SIZE = N_EXPERTS // N_ROUTE_GROUPS
EXPERT_DIM = 256
ROUTED_SCALE = 2.5

LANES = 128
SMALL_W = LANES
FG_LANE0 = 0
DT_LANE0 = 8
AUG_W = LANES

IN_TM = 512
ATT_TQ = 512
ATT_TK = 512
ATT_HEADS_PER_STEP = 8
POST_TM = 512
SC_CHUNK = 128
EXP_BLK = 1024
COMB_TM = 256
COMB_PARTS = 4
VMEM_LIMIT = 56 * 1024 * 1024
NEG_BIG = -1e30
LOG2E = 1.4426950408889634
DEST_TM = 2048


def _split3(x):
    hi = x.astype(BF16)
    r1 = x - hi.astype(F32)
    mid = r1.astype(BF16)
    lo = (r1 - mid.astype(F32)).astype(BF16)
    return hi, mid, lo


def _dot(a, b):
    return jnp.dot(a, b, preferred_element_type=F32)


def _dot_nt(a, b):
    return lax.dot_general(a, b, (((1,), (1,)), ((), ())), preferred_element_type=F32)


def _dot_exact_lhs01(lhs_bf16, x, pieces=3):
    parts = _split3(x)[:pieces]
    out = _dot(lhs_bf16, parts[0])
    for p in parts[1:]:
        out = out + _dot(lhs_bf16, p)
    return out


def _dot_exact_rhs01(x, rhs_bf16, pieces=2):
    parts = _split3(x)[:pieces]
    out = _dot(parts[0], rhs_bf16)
    for p in parts[1:]:
        out = out + _dot(p, rhs_bf16)
    return out


def _sigmoid(x):
    return 1.0 / (1.0 + jnp.exp(-x))


def _silu(x):
    return x * _sigmoid(x)


def _softplus(x):
    return jnp.maximum(x, 0.0) + jnp.log(1.0 + jnp.exp(-jnp.abs(x)))


def _log_sigmoid(x):
    return jnp.minimum(x, 0.0) - jnp.log(1.0 + jnp.exp(-jnp.abs(x)))


def _pack_bf16_pair(x):
    n = x.shape[1] // 2
    lo = pltpu.bitcast(x[:, :n].astype(BF16).astype(F32), U32)
    hi = pltpu.bitcast(x[:, n:].astype(BF16).astype(F32), U32)
    return (hi & jnp.uint32(0xFFFF0000)) | (lo >> 16)


def _unpack_bf16_pair(w):
    lo = pltpu.bitcast(w << 16, F32)
    hi = pltpu.bitcast(w & jnp.uint32(0xFFFF0000), F32)
    return jnp.concatenate([lo, hi], axis=1)


def _cparams(sem):
    return pltpu.CompilerParams(dimension_semantics=sem, vmem_limit_bytes=VMEM_LIMIT)


def _mod_kernel(c_ref, w_ref, b_ref, o_ref):
    c = c_ref[...]
    o_ref[...] = jnp.dot(_silu(c), w_ref[...], preferred_element_type=F32,
                         precision=lax.Precision.HIGHEST) + b_ref[...]


def _modulation(c, w, b):
    bsz, d = c.shape
    n = w.shape[1]
    tn = 1024
    return pl.pallas_call(
        _mod_kernel,
        out_shape=jax.ShapeDtypeStruct((bsz, n), F32),
        grid=(n // tn,),
        in_specs=[pl.BlockSpec((bsz, d), lambda j: (0, 0)),
                  pl.BlockSpec((d, tn), lambda j: (0, j)),
                  pl.BlockSpec((1, tn), lambda j: (0, j))],
        out_specs=pl.BlockSpec((bsz, tn), lambda j: (0, j)),
        compiler_params=_cparams(("arbitrary",)),
        name="modulation",
    )(c, w, b.reshape(1, n))


_COL_Q, _COL_K, _COL_V, _COL_Z, _COL_XBC, _COL_SM, _COL_END = 0, 512, 1024, 1536, 2048, 3072, 3200


def _inproj_kernel(per_b, x_ref, g_ref, sc_ref, sh_ref, w_ref, fgb_ref, tri_ref,
                   qa_ref, ka_ref, va_ref, z_ref, xbc_ref, sm_ref, carry_ref):
    tm = x_ref.shape[0]

    @pl.when(pl.program_id(0) % per_b == 0)
    def _():
        carry_ref[...] = jnp.zeros_like(carry_ref)

    x = x_ref[...]
    var = jnp.mean(x * x, axis=-1, keepdims=True)
    h = x * lax.rsqrt(var + EPS) * g_ref[...]
    h = h * (1.0 + sc_ref[0]) + sh_ref[0]
    hb = h.astype(BF16)
    sm = _dot(hb, w_ref[:, _COL_SM:_COL_END])
    sm_ref[...] = sm
    log_f = _log_sigmoid(sm + fgb_ref[...]) * LOG2E
    cum = _dot_exact_lhs01(tri_ref[...], log_f) + carry_ref[...]
    carry_ref[...] = cum[tm - 1:tm, :]
    hi, mid, lo = (p.astype(F32) for p in _split3(cum))
    lane = lax.broadcasted_iota(I32, (tm, AUG_W - ATTN_HEAD_DIM), 1)

    def head_rows(out_ref, col0, scale, aug_of_head):
        f = _dot(hb, w_ref[:, col0:col0 + ATTN_DIM])
        if scale is not None:
            f = f * scale
        for hd in range(ATTN_HEADS):
            sl = slice(hd * ATTN_HEAD_DIM, (hd + 1) * ATTN_HEAD_DIM)
            out_ref[0, hd] = jnp.concatenate([f[:, sl], aug_of_head(hd)], axis=1).astype(BF16)

    def cols(hd):
        c0 = FG_LANE0 + hd
        return hi[:, c0:c0 + 1], mid[:, c0:c0 + 1], lo[:, c0:c0 + 1]

    def aug_q(hd):
        chi, cmid, clo = cols(hd)
        return jnp.where(lane == 0, chi, jnp.where(lane == 1, cmid, jnp.where(lane == 2, clo,
                         jnp.where(lane < 6, 1.0, 0.0))))

    def aug_k(hd):
        chi, cmid, clo = cols(hd)
        return jnp.where(lane < 3, 1.0, jnp.where(lane == 3, -chi, jnp.where(lane == 4, -cmid,
                         jnp.where(lane == 5, -clo, 0.0))))

    ones_lane = jnp.where(lane == 0, 1.0, 0.0)
    head_rows(qa_ref, _COL_Q, ATTN_HEAD_DIM ** -0.5 * LOG2E, aug_q)
    head_rows(ka_ref, _COL_K, None, aug_k)
    head_rows(va_ref, _COL_V, None, lambda hd: ones_lane)
    z_ref[...] = _dot(hb, w_ref[:, _COL_Z:_COL_XBC]).astype(BF16)
    xbc_ref[:, :512] = _dot(hb, w_ref[:, _COL_XBC:_COL_XBC + 512]).astype(BF16)
    xbc_ref[:, 512:] = _dot(hb, w_ref[:, _COL_XBC + 512:_COL_SM]).astype(BF16)


def _in_proj(x2, mod3, norm_g, w_cat, fgb_row, tri, bsz, seq_len):
    t, d = x2.shape
    tm = IN_TM
    per_b = seq_len // tm
    row = lambda i: (i, 0)
    const = lambda i: (0, 0)
    aug = jax.ShapeDtypeStruct((bsz, ATTN_HEADS, seq_len, AUG_W), BF16)
    aug_spec = pl.BlockSpec((1, ATTN_HEADS, tm, AUG_W), lambda i: (i // per_b, 0, i % per_b, 0))
    return pl.pallas_call(
        functools.partial(_inproj_kernel, per_b),
        out_shape=(aug, aug, aug,
                   jax.ShapeDtypeStruct((t, SSD_DIM), BF16),
                   jax.ShapeDtypeStruct((t, SSD_CONV_DIM), BF16),
                   jax.ShapeDtypeStruct((t, SMALL_W), F32)),
        grid=(t // tm,),
        in_specs=[pl.BlockSpec((tm, d), row),
                  pl.BlockSpec((1, d), const),
                  pl.BlockSpec((1, 1, d), lambda i: ((i // per_b) * N_MOD + 1, 0, 0)),
                  pl.BlockSpec((1, 1, d), lambda i: ((i // per_b) * N_MOD + 0, 0, 0)),
                  pl.BlockSpec((d, _COL_END), const),
                  pl.BlockSpec((1, SMALL_W), const),
                  pl.BlockSpec((tm, tm), const)],
        out_specs=(aug_spec, aug_spec, aug_spec,
                   pl.BlockSpec((tm, SSD_DIM), row),
                   pl.BlockSpec((tm, SSD_CONV_DIM), row),
                   pl.BlockSpec((tm, SMALL_W), row)),
        scratch_shapes=[pltpu.VMEM((1, SMALL_W), F32)],
        compiler_params=_cparams(("arbitrary",)),
        name="in_proj",
    )(x2, norm_g.reshape(1, d), mod3, mod3, w_cat, fgb_row, tri)


def _attn_kernel(qa_ref, ka_ref, va_ref, g_ref, o_ref):
    i = pl.program_id(2)
    tq, tk = ATT_TQ, ATT_TK
    nh = ATT_HEADS_PER_STEP
    diff = lax.broadcasted_iota(I32, (tq, tk), 0) - lax.broadcasted_iota(I32, (tq, tk), 1)

    def scores(hh, j):
        off = pl.multiple_of(j * tk, tk)
        return _dot_nt(qa_ref[0, hh], ka_ref[0, hh, pl.ds(off, tk), :])

    def update(hh, state, s, j):
        m_old, acc = state
        off = pl.multiple_of(j * tk, tk)
        s = jnp.where(diff >= (j - i) * tk, s, NEG_BIG)
        m_new = jnp.maximum(m_old, jnp.max(s, axis=-1, keepdims=True))
        p = jnp.exp2(s - m_new).astype(BF16)
        acc = jnp.exp2(m_old - m_new) * acc + _dot(p, va_ref[0, hh, pl.ds(off, tk), :])
        return m_new, acc

    def body(j, states):
        ss = [scores(hh, j) for hh in range(nh)]
        return tuple(update(hh, states[hh], ss[hh], j) for hh in range(nh))

    init = tuple((jnp.full((tq, 1), NEG_BIG, F32), jnp.zeros((tq, LANES), F32)) for _ in range(nh))
    states = lax.fori_loop(0, i + 1, body, init)
    sum_lane = lax.broadcasted_iota(I32, (tq, LANES), 1) == ATTN_HEAD_DIM
    lane = lax.broadcasted_iota(I32, (tq, LANES), 1)
    first = lane < ATTN_HEAD_DIM
    for pr in range(nh // 2):
        outs = []
        for hh in (2 * pr, 2 * pr + 1):
            acc = states[hh][1]
            row_sum = jnp.sum(jnp.where(sum_lane, acc, 0.0), axis=-1, keepdims=True)
            outs.append(acc / row_sum)
        o = jnp.where(first, outs[0], pltpu.roll(outs[1], ATTN_HEAD_DIM, 1))
        sq = o * o
        s_all = jnp.sum(sq, axis=-1, keepdims=True)
        s0 = jnp.sum(jnp.where(first, sq, 0.0), axis=-1, keepdims=True)
        ms = jnp.where(first, s0, s_all - s0) * (1.0 / ATTN_HEAD_DIM)
        gsl = slice(pr * LANES, (pr + 1) * LANES)
        o_ref[0, :, gsl] = (o * lax.rsqrt(ms + EPS) * g_ref[:, gsl]).astype(BF16)


def _fox_attn(qa, ka, va, g_row):
    bsz, _, seq_len, _ = qa.shape
    tq = ATT_TQ
    nh = ATT_HEADS_PER_STEP
    wo = nh // 2 * LANES
    kv_spec = pl.BlockSpec((1, nh, seq_len, AUG_W), lambda b, p, i: (b, p, 0, 0))
    return pl.pallas_call(
        _attn_kernel,
        out_shape=jax.ShapeDtypeStruct((bsz, seq_len, ATTN_DIM), BF16),
        grid=(bsz, ATTN_HEADS // nh, seq_len // tq),
        in_specs=[pl.BlockSpec((1, nh, tq, AUG_W), lambda b, p, i: (b, p, i, 0)),
                  kv_spec, kv_spec,
                  pl.BlockSpec((1, wo), lambda b, p, i: (0, p))],
        out_specs=pl.BlockSpec((1, tq, wo), lambda b, p, i: (b, i, p)),
        compiler_params=_cparams(("arbitrary", "arbitrary", "arbitrary")),
        name="fox_attn",
    )(qa, ka, va, g_row)


def _ssd_kernel(xbc_ref, z_ref, sm_ref, cw_ref, cb_ref, dtb_ref, alog_ref, dsk_ref, g_ref, tri_ref, exp_ref,
                o_ref, ext_ref, state_ref):
    c = pl.program_id(1)
    q = SSD_CHUNK

    @pl.when(c == 0)
    def _():
        ext_ref[0:8, :] = jnp.zeros((8, SSD_CONV_DIM), F32)
        state_ref[...] = jnp.zeros_like(state_ref)

    ext_ref[8:8 + q, :] = xbc_ref[0].astype(F32)
    conv = cb_ref[...] + cw_ref[0:1, :] * ext_ref[5:5 + q, :]
    for j in range(1, SSD_CONV):
        conv = conv + cw_ref[j:j + 1, :] * ext_ref[5 + j:5 + j + q, :]
    ext_ref[0:8, :] = ext_ref[q:q + 8, :]
    xc = _silu(conv)
    xs = xc[:, :SSD_DIM]

    dt = _softplus(sm_ref[0] + dtb_ref[...])
    a_dt = -jnp.exp(alog_ref[...]) * dt
    a_cs = _dot_exact_lhs01(tri_ref[...], a_dt)
    a_last = a_cs[q - 1:q, :]
    e_cs = jnp.exp(a_cs)
    dec = jnp.exp(a_last - a_cs)
    a_cs_t = a_cs.T
    expand = exp_ref[...]
    dt_x = _dot_exact_rhs01(dt, expand)
    e_x = _dot_exact_rhs01(e_cs, expand)
    dec_x = _dot_exact_rhs01(dec, expand)
    x_dt = xs * dt_x
    x_dec = (x_dt * dec_x).astype(BF16)
    x_dt_b = x_dt.astype(BF16)

    row = lax.broadcasted_iota(I32, (q, q), 0)
    col = lax.broadcasted_iota(I32, (q, q), 1)
    lower = row >= col
    lane = lax.broadcasted_iota(I32, (q, LANES), 1)
    first = lane < SSD_HEAD_DIM
    y_parts = []
    for g in range(SSD_GROUPS):
        b_g = xc[:, SSD_DIM + g * SSD_STATE:SSD_DIM + (g + 1) * SSD_STATE]
        c_g = xc[:, SSD_DIM + (SSD_GROUPS + g) * SSD_STATE:SSD_DIM + (SSD_GROUPS + g + 1) * SSD_STATE]
        c_gb = c_g.astype(BF16)
        cb = _dot_nt(c_gb, b_g.astype(BF16))
        gs = slice(g * SSD_GROUP_DIM, (g + 1) * SSD_GROUP_DIM)
        st_prev = state_ref[g]
        y_off = _dot(c_gb, st_prev.astype(BF16)) * e_x[:, gs]
        s_new = _dot(b_g.T.astype(BF16), x_dec[:, gs])
        state_ref[g] = st_prev * e_x[q - 1:q, gs] + s_new
        for pr in range(2):
            pair = []
            for hh in range(2):
                h = g * 4 + pr * 2 + hh
                a_col = a_cs[:, DT_LANE0 + h:DT_LANE0 + h + 1]
                a_row = a_cs_t[DT_LANE0 + h:DT_LANE0 + h + 1, :]
                lmat = jnp.where(lower, jnp.exp(jnp.minimum(a_col - a_row, 0.0)), 0.0)
                m_h = (cb * lmat).astype(BF16)
                ps = slice((g * 2 + pr) * LANES, (g * 2 + pr + 1) * LANES)
                pair.append(_dot(m_h, x_dt_b[:, ps]))
            y_parts.append(jnp.where(first, pair[0], pair[1]) + y_off[:, pr * LANES:(pr + 1) * LANES])
    y = jnp.concatenate(y_parts, axis=1) + dsk_ref[...] * xs
    y = y * _silu(z_ref[0].astype(F32))
    outs = []
    for g in range(SSD_GROUPS):
        yg = y[:, g * SSD_GROUP_DIM:(g + 1) * SSD_GROUP_DIM]
        ms = jnp.mean(yg * yg, axis=-1, keepdims=True)
        outs.append(yg * lax.rsqrt(ms + EPS))
    o_ref[0] = (jnp.concatenate(outs, axis=1) * g_ref[...]).astype(BF16)


def _ssd(xbc3, z3, sm3, conv_w, conv_b, dtb_row, alog_row, dskip_row, g_row, tri, expand):
    bsz, seq_len, _ = xbc3.shape
    q = SSD_CHUNK
    blk = lambda b, c: (b, c, 0)
    const = lambda b, c: (0, 0)
    return pl.pallas_call(
        _ssd_kernel,
        out_shape=jax.ShapeDtypeStruct((bsz, seq_len, SSD_DIM), BF16),
        grid=(bsz, seq_len // q),
        in_specs=[pl.BlockSpec((1, q, SSD_CONV_DIM), blk),
                  pl.BlockSpec((1, q, SSD_DIM), blk),
                  pl.BlockSpec((1, q, SMALL_W), blk),
                  pl.BlockSpec((SSD_CONV, SSD_CONV_DIM), const),
                  pl.BlockSpec((1, SSD_CONV_DIM), const),
                  pl.BlockSpec((1, SMALL_W), const),
                  pl.BlockSpec((1, SMALL_W), const),
                  pl.BlockSpec((1, SSD_DIM), const),
                  pl.BlockSpec((1, SSD_DIM), const),
                  pl.BlockSpec((q, q), const),
                  pl.BlockSpec((SMALL_W, SSD_DIM), const)],
        out_specs=pl.BlockSpec((1, q, SSD_DIM), blk),
        scratch_shapes=[pltpu.VMEM((q + 8, SSD_CONV_DIM), F32),
                        pltpu.VMEM((SSD_GROUPS, SSD_STATE, SSD_GROUP_DIM), F32)],
        compiler_params=_cparams(("arbitrary", "arbitrary")),
        name="ssd",
    )(xbc3, z3, sm3, conv_w, conv_b, dtb_row, alog_row, dskip_row, g_row, tri, expand)


def _post_kernel(attn_ref, ssd_ref, x_ref, g1_ref, sh2_ref, sc2_ref, g2_ref, woa_ref, wos_ref, n2_ref,
                 wrt_ref, rb_ref, wgs_ref, wus_ref, wds_ref, upper_ref,
                 x1s_ref, h2_ref, eidx_ref, wts_ref, rank_ref, cnt_ref):
    i = pl.program_id(0)
    tm = x_ref.shape[0]

    @pl.when(i == 0)
    def _():
        cnt_ref[...] = jnp.zeros_like(cnt_ref)

    mixed = _dot(attn_ref[...], woa_ref[...]) + _dot(ssd_ref[...], wos_ref[...])
    x1 = x_ref[...] + g1_ref[0] * mixed
    var = jnp.mean(x1 * x1, axis=-1, keepdims=True)
    h2 = x1 * lax.rsqrt(var + EPS) * n2_ref[...]
    h2 = h2 * (1.0 + sc2_ref[0]) + sh2_ref[0]
    hb = h2.astype(BF16)
    h2_ref[...] = _pack_bf16_pair(h2)
    act = _silu(_dot(hb, wgs_ref[...])) * _dot(hb, wus_ref[...])
    shared = _dot(act.astype(BF16), wds_ref[...])
    x1s_ref[...] = x1 + g2_ref[0] * shared

    scores = _sigmoid(_dot_nt(wrt_ref[...], hb))
    biased = scores + rb_ref[...]
    grp = biased.reshape(N_ROUTE_GROUPS, GROUP_SIZE, tm)
    gi = lax.broadcasted_iota(I32, grp.shape, 1)
    m1 = jnp.max(grp, axis=1, keepdims=True)
    i1 = jnp.min(jnp.where(grp == m1, gi, GROUP_SIZE), axis=1, keepdims=True)
    m2 = jnp.max(jnp.where(gi == i1, -jnp.inf, grp), axis=1, keepdims=True)
    gsc = (m1 + m2).reshape(N_ROUTE_GROUPS, tm)
    gidx = lax.broadcasted_iota(I32, gsc.shape, 0)
    beaten = jnp.zeros(gsc.shape, I32)
    for o in range(N_ROUTE_GROUPS):
        other = gsc[o:o + 1, :]
        beats = (other > gsc) | ((other == gsc) & (gidx > o))
        beaten = beaten + beats.astype(I32)
    gmask = (beaten < TOPK_ROUTE_GROUPS).astype(F32)
    emask = jnp.broadcast_to(gmask.reshape(N_ROUTE_GROUPS, 1, tm), grp.shape).reshape(N_EXPERTS, tm)
    masked = jnp.where(emask > 0.5, biased, -jnp.inf)
    eiota = lax.broadcasted_iota(I32, (N_EXPERTS, tm), 0)
    idx_rows, w_rows = [], []
    sel = jnp.zeros((N_EXPERTS, tm), F32)
    for _ in range(TOP_K):
        mk = jnp.max(masked, axis=0, keepdims=True)
        ik = jnp.min(jnp.where(masked == mk, eiota, N_EXPERTS), axis=0, keepdims=True)
        hit = eiota == ik
        w_rows.append(jnp.sum(jnp.where(hit, scores, 0.0), axis=0, keepdims=True))
        idx_rows.append(ik)
        masked = jnp.where(hit, -jnp.inf, masked)
        sel = jnp.where(hit, 1.0, sel)
    w_all = jnp.concatenate(w_rows, axis=0)
    wts_ref[...] = w_all / jnp.sum(w_all, axis=0, keepdims=True) * ROUTED_SCALE
    eidx_ref[...] = jnp.concatenate(idx_rows, axis=0)
    before = cnt_ref[...][:, 0:1] + _dot(sel.astype(BF16), upper_ref[...])
    rank_rows = [jnp.sum(jnp.where(eiota == ik, before, 0.0), axis=0, keepdims=True) for ik in idx_rows]
    rank_ref[...] = jnp.concatenate(rank_rows, axis=0).astype(I32)
    cnt_ref[...] = cnt_ref[...] + jnp.sum(sel, axis=1, keepdims=True)


def _post_mixer(attn2, ssd2, x2, mod3, w_out_a, w_out_s, norm2_g, w_router_t, rb_col,
                wgs, wus, wds, upper, seq_len):
    t, d = x2.shape
    tm = POST_TM
    per_b = seq_len // tm
    row = lambda i: (i, 0)
    const = lambda i: (0, 0)
    modspec = lambda k: pl.BlockSpec((1, 1, d), lambda i: ((i // per_b) * N_MOD + k, 0, 0))
    slot = lambda i: (0, i)
    return pl.pallas_call(
        _post_kernel,
        out_shape=(jax.ShapeDtypeStruct((t, d), F32),
                   jax.ShapeDtypeStruct((t, d // 2), U32),
                   jax.ShapeDtypeStruct((TOP_K, t), I32),
                   jax.ShapeDtypeStruct((TOP_K, t), F32),
                   jax.ShapeDtypeStruct((TOP_K, t), I32),
                   jax.ShapeDtypeStruct((N_EXPERTS, LANES), F32)),
        grid=(t // tm,),
        in_specs=[pl.BlockSpec((tm, ATTN_DIM), row),
                  pl.BlockSpec((tm, SSD_DIM), row),
                  pl.BlockSpec((tm, d), row),
                  modspec(2), modspec(3), modspec(4), modspec(5),
                  pl.BlockSpec((ATTN_DIM, d), const),
                  pl.BlockSpec((SSD_DIM, d), const),
                  pl.BlockSpec((1, d), const),
                  pl.BlockSpec((N_EXPERTS, d), const),
                  pl.BlockSpec((N_EXPERTS, 1), const),
                  pl.BlockSpec((d, EXPERT_DIM), const),
                  pl.BlockSpec((d, EXPERT_DIM), const),
                  pl.BlockSpec((EXPERT_DIM, d), const),
                  pl.BlockSpec((tm, tm), const)],
        out_specs=(pl.BlockSpec((tm, d), row),
                   pl.BlockSpec((tm, d // 2), row),
                   pl.BlockSpec((TOP_K, tm), slot),
                   pl.BlockSpec((TOP_K, tm), slot),
                   pl.BlockSpec((TOP_K, tm), slot),
                   pl.BlockSpec((N_EXPERTS, LANES), const)),
        compiler_params=_cparams(("arbitrary",)),
        name="post_mixer",
    )(attn2, ssd2, x2, mod3, mod3, mod3, mod3, w_out_a, w_out_s, norm2_g.reshape(1, d),
      w_router_t, rb_col, wgs, wus, wds, upper)


def _dest_kernel(eidx_ref, rank_ref, pst_ref, dest_ref):
    tm = eidx_ref.shape[1]
    eiota = lax.broadcasted_iota(I32, (N_EXPERTS, tm), 0)
    rows = []
    for k in range(TOP_K):
        onehot = jnp.where(eiota == eidx_ref[k:k + 1, :], 1.0, 0.0).astype(BF16)
        r = _dot(pst_ref[...], onehot)
        rows.append(r[0:1, :] + r[1:2, :] + r[2:3, :])
    dest_ref[...] = jnp.concatenate(rows, axis=0).astype(I32) + rank_ref[...]


def _dest_rows(eidx, rank, pstart):
    k, t = eidx.shape
    tm = min(DEST_TM, t)
    pieces = jnp.stack([pstart & 0xFF0000, pstart & 0xFF00, pstart & 0xFF], axis=0)
    pst = jnp.zeros((8, N_EXPERTS), F32).at[:3].set(pieces.astype(F32)).astype(BF16)
    slot = lambda i: (0, i)
    return pl.pallas_call(
        _dest_kernel,
        out_shape=jax.ShapeDtypeStruct((k, t), I32),
        grid=(t // tm,),
        in_specs=[pl.BlockSpec((k, tm), slot), pl.BlockSpec((k, tm), slot),
                  pl.BlockSpec((8, N_EXPERTS), lambda i: (0, 0))],
        out_specs=pl.BlockSpec((k, tm), slot),
        compiler_params=_cparams(("arbitrary",)),
        name="dest_rows",
    )(eidx, rank, pst)


def _sc_mesh():
    return plsc.VectorSubcoreMesh(core_axis_name="c", subcore_axis_name="s")


def _sc_worker(n_workers_per_core):
    return lax.axis_index("s") * n_workers_per_core + lax.axis_index("c")


def _dispatch(dest, h2p, n_rows):
    t, w = h2p.shape
    mesh = _sc_mesh()
    n_workers = mesh.num_cores * mesh.num_subcores
    per_w = t // n_workers
    ch = min(SC_CHUNK, per_w)

    def body(dest_hbm, h_hbm, xs_hbm, idx_v, rows_v, sem):
        base_w = _sc_worker(mesh.num_cores) * per_w

        @pl.loop(0, per_w // ch)
        def _(ci):
            base = pl.multiple_of(base_w + ci * ch, ch)
            pltpu.sync_copy(dest_hbm.at[:, pl.ds(base, ch)], idx_v)
            pltpu.sync_copy(h_hbm.at[pl.ds(base, ch)], rows_v)
            copies = [pltpu.async_copy(rows_v, xs_hbm.at[idx_v.at[k]], sem) for k in range(TOP_K)]
            for cp in copies:
                cp.wait()

    return pl.kernel(
        body,
        out_type=jax.ShapeDtypeStruct((n_rows, w), U32),
        mesh=mesh,
        scratch_types=[pltpu.VMEM((TOP_K, ch), I32), pltpu.VMEM((ch, w), U32), pltpu.SemaphoreType.DMA],
        name="dispatch",
    )(dest, h2p)


def _undispatch(dest, ys, t0, t):
    w = ys.shape[1]
    mesh = _sc_mesh()
    n_workers = mesh.num_cores * mesh.num_subcores
    per_w = t // n_workers
    ch = min(SC_CHUNK, per_w)

    def body(dest_hbm, ys_hbm, ytok_hbm, idx_v, rows_v, sem):
        base_w = _sc_worker(mesh.num_cores) * per_w

        @pl.loop(0, per_w // ch)
        def _(ci):
            base = pl.multiple_of(base_w + ci * ch, ch)
            pltpu.sync_copy(dest_hbm.at[:, pl.ds(t0 + base, ch)], idx_v)
            for k in range(TOP_K):
                pltpu.async_copy(ys_hbm.at[idx_v.at[k]], rows_v, sem).wait()
                pltpu.sync_copy(rows_v, ytok_hbm.at[k, pl.ds(base, ch)])

    return pl.kernel(
        body,
        out_type=jax.ShapeDtypeStruct((TOP_K, t, w), U32),
        mesh=mesh,
        scratch_types=[pltpu.VMEM((TOP_K, ch), I32), pltpu.VMEM((ch, w), U32), pltpu.SemaphoreType.DMA],
        name="undispatch",
    )(dest, ys)


def _expert_kernel(be_ref, nu_ref, x_ref, wg_ref, wu_ref, wd_ref, y_ref, act_ref, wgb_ref, wub_ref, wdb_ref):
    b = pl.program_id(0)
    n_used = nu_ref[0]
    e_cur = be_ref[jnp.minimum(b, n_used - 1)]
    e_prev = be_ref[jnp.clip(b - 1, 0, n_used - 1)]
    e_prev2 = be_ref[jnp.clip(b - 2, 0, n_used - 1)]

    @pl.when((b == 0) | (e_cur != e_prev))
    def _():
        wgb_ref[...] = wg_ref[0].astype(BF16)
        wub_ref[...] = wu_ref[0].astype(BF16)

    @pl.when((b == 1) | (e_prev != e_prev2))
    def _():
        wdb_ref[...] = wd_ref[0].astype(BF16)

    def gate_up():
        x = _unpack_bf16_pair(x_ref[...]).astype(BF16)
        return (_silu(_dot(x, wgb_ref[...])) * _dot(x, wub_ref[...])).astype(BF16)

    def down():
        y_ref[...] = _pack_bf16_pair(_dot(act_ref[...], wdb_ref[...]))

    @pl.when(b == 0)
    def _():
        act_ref[...] = gate_up()

    @pl.when((b > 0) & (b < n_used))
    def _():
        down()
        act_ref[...] = gate_up()

    @pl.when(b == n_used)
    def _():
        down()


def _experts(block_expert, n_used, xs, wg, wu, wd):
    n_rows, w = xs.shape
    n_blocks = n_rows // EXP_BLK
    d, f = wg.shape[1], wg.shape[2]
    cur = lambda b, be, nu: (jnp.minimum(b, nu[0] - 1), 0)
    prev = lambda b, be, nu: (jnp.clip(b - 1, 0, nu[0] - 1), 0)
    wcur = lambda b, be, nu: (be[jnp.minimum(b, nu[0] - 1)], 0, 0)
    wprev = lambda b, be, nu: (be[jnp.clip(b - 1, 0, nu[0] - 1)], 0, 0)
    grid_spec = pltpu.PrefetchScalarGridSpec(
        num_scalar_prefetch=2,
        grid=(n_blocks + 1,),
        in_specs=[pl.BlockSpec((EXP_BLK, w), cur),
                  pl.BlockSpec((1, d, f), wcur),
                  pl.BlockSpec((1, d, f), wcur),
                  pl.BlockSpec((1, f, d), wprev)],
        out_specs=pl.BlockSpec((EXP_BLK, w), prev),
        scratch_shapes=[pltpu.VMEM((EXP_BLK, f), BF16), pltpu.VMEM((d, f), BF16), pltpu.VMEM((d, f), BF16),
                        pltpu.VMEM((f, d), BF16)],
    )
    return pl.pallas_call(
        _expert_kernel,
        out_shape=jax.ShapeDtypeStruct((n_rows, w), U32),
        grid_spec=grid_spec,
        compiler_params=_cparams(("arbitrary",)),
        name="experts",
    )(block_expert, n_used, xs, wg, wu, wd)


def _combine_kernel(ytok_ref, x1s_ref, w_ref, g2_ref, nf_ref, mf0_ref, mf1_ref, o_ref):
    w = w_ref[...]
    routed = w[:, 0:1] * _unpack_bf16_pair(ytok_ref[0])
    for k in range(1, TOP_K):
        routed = routed + w[:, k:k + 1] * _unpack_bf16_pair(ytok_ref[k])
    xo = x1s_ref[...] + g2_ref[0] * routed
    var = jnp.mean(xo * xo, axis=-1, keepdims=True)
    y = xo * lax.rsqrt(var + EPS) * nf_ref[...]
    o_ref[...] = y * (1.0 + mf1_ref[0]) + mf0_ref[0]


def _combine_alias_kernel(prev_ref, *refs):
    del prev_ref
    _combine_kernel(*refs)


def _combine(ytok, x1s, wts_t, mod3, normf_g, modf3, seq_len, part, out_prev):
    t, d = x1s.shape
    tm = COMB_TM
    steps = ytok.shape[1] // tm
    i0 = part * steps
    per_b = seq_len // tm
    row = lambda i: (i0 + i, 0)
    in_specs = [pl.BlockSpec((TOP_K, tm, d // 2), lambda i: (0, i, 0)),
                pl.BlockSpec((tm, d), row),
                pl.BlockSpec((tm, TOP_K), row),
                pl.BlockSpec((1, 1, d), lambda i: (((i0 + i) // per_b) * N_MOD + 5, 0, 0)),
                pl.BlockSpec((1, d), lambda i: (0, 0)),
                pl.BlockSpec((1, 1, d), lambda i: (((i0 + i) // per_b) * 2 + 0, 0, 0)),
                pl.BlockSpec((1, 1, d), lambda i: (((i0 + i) // per_b) * 2 + 1, 0, 0))]
    args = (ytok, x1s, wts_t, mod3, normf_g.reshape(1, d), modf3, modf3)
    if out_prev is None:
        body, aliases = _combine_kernel, {}
    else:
        body, aliases = _combine_alias_kernel, {0: 0}
        in_specs = [pl.BlockSpec(memory_space=pl.ANY)] + in_specs
        args = (out_prev,) + args
    return pl.pallas_call(
        body,
        out_shape=jax.ShapeDtypeStruct((t, d), F32),
        grid=(steps,),
        in_specs=in_specs,
        out_specs=pl.BlockSpec((tm, d), row),
        input_output_aliases=aliases,
        compiler_params=_cparams(("arbitrary",)),
        name="combine",
    )(*args)


def _lane_row(vec, lane0):
    return jnp.zeros((1, SMALL_W), F32).at[0, lane0:lane0 + vec.shape[0]].set(vec.astype(F32))


def _layer(x2, mod3, bsz, seq_len, norm1_g, w_in, fg_bias, conv_w, conv_b, dt_bias, a_log, d_skip,
           attn_norm_g, ssd_norm_g, w_out, norm2_g, w_router, router_bias,
           w_gate_e, w_up_e, w_down_e, w_gate_s, w_up_s, w_down_s):
    t, d = x2.shape
    o_q, o_k, o_v, o_fg = 0, ATTN_DIM, 2 * ATTN_DIM, 3 * ATTN_DIM
    o_z = o_fg + ATTN_HEADS
    o_xbc = o_z + SSD_DIM
    o_dt = o_xbc + SSD_CONV_DIM
    small = jnp.zeros((d, SMALL_W), F32)
    small = small.at[:, FG_LANE0:FG_LANE0 + ATTN_HEADS].set(w_in[:, o_fg:o_z])
    small = small.at[:, DT_LANE0:DT_LANE0 + SSD_HEADS].set(w_in[:, o_dt:o_dt + SSD_HEADS])
    w_cat = jnp.concatenate([w_in[:, o_q:o_fg], w_in[:, o_z:o_dt], small], axis=1).astype(BF16)

    tri_in = jnp.tril(jnp.ones((IN_TM, IN_TM), F32)).astype(BF16)
    qa, ka, va, z2, xbc2, sm2 = _in_proj(x2, mod3, norm1_g, w_cat, _lane_row(fg_bias, FG_LANE0), tri_in,
                                         bsz, seq_len)
    shp = lambda a: a.reshape(bsz, seq_len, a.shape[-1])
    attn3 = _fox_attn(qa, ka, va, attn_norm_g.reshape(1, ATTN_DIM).astype(F32))

    tri_chunk = jnp.tril(jnp.ones((SSD_CHUNK, SSD_CHUNK), F32)).astype(BF16)
    head_of_lane = jnp.arange(SSD_DIM, dtype=I32) // SSD_HEAD_DIM
    expand = (jnp.arange(SMALL_W, dtype=I32)[:, None] == head_of_lane[None, :] + DT_LANE0).astype(BF16)
    dskip_row = jnp.repeat(d_skip.astype(F32), SSD_HEAD_DIM).reshape(1, SSD_DIM)
    ssd3 = _ssd(shp(xbc2), shp(z2), shp(sm2), conv_w.astype(F32), conv_b.reshape(1, -1).astype(F32),
                _lane_row(dt_bias, DT_LANE0), _lane_row(a_log, DT_LANE0), dskip_row,
                ssd_norm_g.reshape(1, SSD_DIM).astype(F32), tri_chunk, expand)

    upper = jnp.triu(jnp.ones((POST_TM, POST_TM), F32), 1).astype(BF16)
    x1s, h2p, eidx, wts, rank, cnt = _post_mixer(
        attn3.reshape(t, ATTN_DIM), ssd3.reshape(t, SSD_DIM), x2, mod3,
        w_out[:ATTN_DIM].astype(BF16), w_out[ATTN_DIM:].astype(BF16), norm2_g,
        w_router.T.astype(BF16), router_bias.reshape(N_EXPERTS, 1).astype(F32),
        w_gate_s.astype(BF16), w_up_s.astype(BF16), w_down_s.astype(BF16), upper, seq_len)

    counts = cnt[:, 0].astype(I32)
    padded = (counts + EXP_BLK - 1) // EXP_BLK * EXP_BLK
    pend = jnp.cumsum(padded)
    pstart = pend - padded
    n_blocks = (t * TOP_K) // EXP_BLK + N_EXPERTS
    block_start = jnp.arange(n_blocks, dtype=I32) * EXP_BLK
    block_expert = jnp.minimum(jnp.sum((pend[None, :] <= block_start[:, None]).astype(I32), axis=1),
                               N_EXPERTS - 1)
    n_used = (pend[-1] // EXP_BLK).astype(I32).reshape(1)
    dest = _dest_rows(eidx, rank, pstart)

    xs = _dispatch(dest, h2p, n_blocks * EXP_BLK)
    ys = _experts(block_expert, n_used, xs, w_gate_e, w_up_e, w_down_e)
    return dest, ys, x1s, wts


def kernel(x, c, norm1_g, w_ada, b_ada, w_in, fg_bias, conv_w, conv_b, dt_bias, a_log, d_skip, attn_norm_g,
           ssd_norm_g, w_out, norm2_g, w_router, router_bias, w_gate_e, w_up_e, w_down_e, w_gate_s, w_up_s,
           w_down_s, normf_g, w_ada_f, b_ada_f):
    bsz, seq_len, d = x.shape
    assert w_ada.shape[0] == 1, "single-layer kernel"
    t = bsz * seq_len
    mod3 = _modulation(c, w_ada[0], b_ada[0]).reshape(bsz * N_MOD, 1, d)
    modf3 = _modulation(c, w_ada_f, b_ada_f).reshape(bsz * 2, 1, d)
    x2 = x.reshape(t, d)
    dest, ys, x1s, wts = _layer(
        x2, mod3, bsz, seq_len, norm1_g[0], w_in[0], fg_bias[0], conv_w[0], conv_b[0], dt_bias[0], a_log[0],
        d_skip[0], attn_norm_g[0], ssd_norm_g[0], w_out[0], norm2_g[0], w_router[0], router_bias[0],
        w_gate_e[0], w_up_e[0], w_down_e[0], w_gate_s[0], w_up_s[0], w_down_s[0])
    wts_t = wts.T
    tp = t // COMB_PARTS
    out = None
    for p in range(COMB_PARTS):
        ytok = _undispatch(dest, ys, p * tp, tp)
        out = _combine(ytok, x1s, wts_t, mod3, normf_g, modf3, seq_len, p, out)
    return out.reshape(bsz, seq_len, d)
```

```python
import functools

import jax
import jax.numpy as jnp
from jax import lax
from jax.experimental import pallas as pl
from jax.experimental.pallas import tpu as pltpu
from jax.experimental.pallas import tpu_sc as plsc

F32 = jnp.float32
BF16 = jnp.bfloat16
I32 = jnp.int32
U32 = jnp.uint32

EPS = 1e-6
D_MODEL = 1024
N_MOD = 6

ATTN_HEADS = 8
ATTN_HEAD_DIM = 64
ATTN_DIM = ATTN_HEADS * ATTN_HEAD_DIM

SSD_HEADS = 8
SSD_HEAD_DIM = 64
SSD_DIM = SSD_HEADS * SSD_HEAD_DIM
SSD_GROUPS = 2
SSD_STATE = 128
SSD_CONV = 4
SSD_CHUNK = 128
SSD_CONV_DIM = SSD_DIM + 2 * SSD_GROUPS * SSD_STATE
SSD_GROUP_DIM = SSD_DIM // SSD_GROUPS

N_EXPERTS = 256
TOP_K = 8
N_ROUTE_GROUPS = 8
TOPK_ROUTE_GROUPS = 4
GROUP_SIZE = N_EXPERTS // N_ROUTE_GROUPS
EXPERT_DIM = 256
ROUTED_SCALE = 2.5

LANES = 128
SMALL_W = LANES
FG_LANE0 = 0
DT_LANE0 = 8
AUG_W = LANES

IN_TM = 512
ATT_TQ = 512
ATT_TK = 512
ATT_HEADS_PER_STEP = 8
POST_TM = 512
SC_CHUNK = 128
EXP_BLK = 1024
COMB_TM = 256
COMB_PARTS = 8
VMEM_LIMIT = 56 * 1024 * 1024
NEG_BIG = -1e30
LOG2E = 1.4426950408889634
DEST_TM = 2048


def _split3(x):
    hi = x.astype(BF16)
    r1 = x - hi.astype(F32)
    mid = r1.astype(BF16)
    lo = (r1 - mid.astype(F32)).astype(BF16)
    return hi, mid, lo


def _dot(a, b):
    return jnp.dot(a, b, preferred_element_type=F32)


def _dot_nt(a, b):
    return lax.dot_general(a, b, (((1,), (1,)), ((), ())), preferred_element_type=F32)


def _dot_exact_lhs01(lhs_bf16, x, pieces=3):
    parts = _split3(x)[:pieces]
    out = _dot(lhs_bf16, parts[0])
    for p in parts[1:]:
        out = out + _dot(lhs_bf16, p)
    return out


def _dot_exact_rhs01(x, rhs_bf16, pieces=2):
    parts = _split3(x)[:pieces]
    out = _dot(parts[0], rhs_bf16)
    for p in parts[1:]:
        out = out + _dot(p, rhs_bf16)
    return out


def _sigmoid(x):
    return 1.0 / (1.0 + jnp.exp(-x))


def _silu(x):
    return x * _sigmoid(x)


def _softplus(x):
    return jnp.maximum(x, 0.0) + jnp.log(1.0 + jnp.exp(-jnp.abs(x)))


def _log_sigmoid(x):
    return jnp.minimum(x, 0.0) - jnp.log(1.0 + jnp.exp(-jnp.abs(x)))


def _pack_bf16_pair(x):
    n = x.shape[1] // 2
    lo = pltpu.bitcast(x[:, :n].astype(BF16).astype(F32), U32)
    hi = pltpu.bitcast(x[:, n:].astype(BF16).astype(F32), U32)
    return (hi & jnp.uint32(0xFFFF0000)) | (lo >> 16)


def _unpack_bf16_pair(w):
    lo = pltpu.bitcast(w << 16, F32)
    hi = pltpu.bitcast(w & jnp.uint32(0xFFFF0000), F32)
    return jnp.concatenate([lo, hi], axis=1)


def _cparams(sem):
    return pltpu.CompilerParams(dimension_semantics=sem, vmem_limit_bytes=VMEM_LIMIT)


def _mod_kernel(c_ref, w_ref, b_ref, o_ref):
    c = c_ref[...]
    o_ref[...] = jnp.dot(_silu(c), w_ref[...], preferred_element_type=F32,
                         precision=lax.Precision.HIGHEST) + b_ref[...]


def _modulation(c, w, b):
    bsz, d = c.shape
    n = w.shape[1]
    tn = 1024
    return pl.pallas_call(
        _mod_kernel,
        out_shape=jax.ShapeDtypeStruct((bsz, n), F32),
        grid=(n // tn,),
        in_specs=[pl.BlockSpec((bsz, d), lambda j: (0, 0)),
                  pl.BlockSpec((d, tn), lambda j: (0, j)),
                  pl.BlockSpec((1, tn), lambda j: (0, j))],
        out_specs=pl.BlockSpec((bsz, tn), lambda j: (0, j)),
        compiler_params=_cparams(("arbitrary",)),
        name="modulation",
    )(c, w, b.reshape(1, n))


_COL_Q, _COL_K, _COL_V, _COL_Z, _COL_XBC, _COL_SM, _COL_END = 0, 512, 1024, 1536, 2048, 3072, 3200


def _inproj_kernel(per_b, x_ref, g_ref, sc_ref, sh_ref, w_ref, fgb_ref, tri_ref,
                   qa_ref, ka_ref, va_ref, z_ref, xbc_ref, sm_ref, carry_ref):
    tm = x_ref.shape[0]

    @pl.when(pl.program_id(0) % per_b == 0)
    def _():
        carry_ref[...] = jnp.zeros_like(carry_ref)

    x = x_ref[...]
    var = jnp.mean(x * x, axis=-1, keepdims=True)
    h = x * lax.rsqrt(var + EPS) * g_ref[...]
    h = h * (1.0 + sc_ref[0]) + sh_ref[0]
    hb = h.astype(BF16)
    sm = _dot(hb, w_ref[:, _COL_SM:_COL_END])
    sm_ref[...] = sm
    log_f = _log_sigmoid(sm + fgb_ref[...]) * LOG2E
    cum = _dot_exact_lhs01(tri_ref[...], log_f) + carry_ref[...]
    carry_ref[...] = cum[tm - 1:tm, :]
    hi, mid, lo = (p.astype(F32) for p in _split3(cum))
    lane = lax.broadcasted_iota(I32, (tm, AUG_W), 1)
    left = lane < ATTN_HEAD_DIM

    def head_rows(out_ref, col0, scale, book_of_head):
        f = _dot(hb, w_ref[:, col0:col0 + ATTN_DIM])
        if scale is not None:
            f = f * scale
        for pr in range(ATTN_HEADS // 2):
            pair = f[:, pr * AUG_W:(pr + 1) * AUG_W]
            out_ref[0, 2 * pr] = jnp.where(left, pair, book_of_head(2 * pr, ATTN_HEAD_DIM)).astype(BF16)
            out_ref[0, 2 * pr + 1] = jnp.where(left, book_of_head(2 * pr + 1, 0), pair).astype(BF16)

    def cols(hd):
        c0 = FG_LANE0 + hd
        return hi[:, c0:c0 + 1], mid[:, c0:c0 + 1], lo[:, c0:c0 + 1]

    def book_q(hd, base):
        chi, cmid, clo = cols(hd)
        rel = lane - base
        return jnp.where(rel == 0, chi, jnp.where(rel == 1, cmid, jnp.where(rel == 2, clo,
                         jnp.where((rel >= 3) & (rel < 6), 1.0, 0.0))))

    def book_k(hd, base):
        chi, cmid, clo = cols(hd)
        rel = lane - base
        return jnp.where(rel == 3, -chi, jnp.where(rel == 4, -cmid, jnp.where(rel == 5, -clo,
                         jnp.where((rel >= 0) & (rel < 3), 1.0, 0.0))))

    def book_v(hd, base):
        return jnp.where(lane == base, 1.0, 0.0)

    head_rows(qa_ref, _COL_Q, ATTN_HEAD_DIM ** -0.5 * LOG2E, book_q)
    head_rows(ka_ref, _COL_K, None, book_k)
    head_rows(va_ref, _COL_V, None, book_v)
    z_ref[...] = _dot(hb, w_ref[:, _COL_Z:_COL_XBC]).astype(BF16)
    xbc_ref[:, :512] = _dot(hb, w_ref[:, _COL_XBC:_COL_XBC + 512]).astype(BF16)
    xbc_ref[:, 512:] = _dot(hb, w_ref[:, _COL_XBC + 512:_COL_SM]).astype(BF16)


def _in_proj(x2, mod3, norm_g, w_cat, fgb_row, tri, bsz, seq_len):
    t, d = x2.shape
    tm = IN_TM
    per_b = seq_len // tm
    row = lambda i: (i, 0)
    const = lambda i: (0, 0)
    aug = jax.ShapeDtypeStruct((bsz, ATTN_HEADS, seq_len, AUG_W), BF16)
    aug_spec = pl.BlockSpec((1, ATTN_HEADS, tm, AUG_W), lambda i: (i // per_b, 0, i % per_b, 0))
    return pl.pallas_call(
        functools.partial(_inproj_kernel, per_b),
        out_shape=(aug, aug, aug,
                   jax.ShapeDtypeStruct((t, SSD_DIM), BF16),
                   jax.ShapeDtypeStruct((t, SSD_CONV_DIM), BF16),
                   jax.ShapeDtypeStruct((t, SMALL_W), F32)),
        grid=(t // tm,),
        in_specs=[pl.BlockSpec((tm, d), row),
                  pl.BlockSpec((1, d), const),
                  pl.BlockSpec((1, 1, d), lambda i: ((i // per_b) * N_MOD + 1, 0, 0)),
                  pl.BlockSpec((1, 1, d), lambda i: ((i // per_b) * N_MOD + 0, 0, 0)),
                  pl.BlockSpec((d, _COL_END), const),
                  pl.BlockSpec((1, SMALL_W), const),
                  pl.BlockSpec((tm, tm), const)],
        out_specs=(aug_spec, aug_spec, aug_spec,
                   pl.BlockSpec((tm, SSD_DIM), row),
                   pl.BlockSpec((tm, SSD_CONV_DIM), row),
                   pl.BlockSpec((tm, SMALL_W), row)),
        scratch_shapes=[pltpu.VMEM((1, SMALL_W), F32)],
        compiler_params=_cparams(("arbitrary",)),
        name="in_proj",
    )(x2, norm_g.reshape(1, d), mod3, mod3, w_cat, fgb_row, tri)


def _attn_kernel(qa_ref, ka_ref, va_ref, g_ref, o_ref):
    i = pl.program_id(2)
    tq, tk = ATT_TQ, ATT_TK
    nh = ATT_HEADS_PER_STEP
    diff = lax.broadcasted_iota(I32, (tq, tk), 0) - lax.broadcasted_iota(I32, (tq, tk), 1)

    def scores(hh, j):
        off = pl.multiple_of(j * tk, tk)
        return _dot_nt(qa_ref[0, hh], ka_ref[0, hh, pl.ds(off, tk), :])

    def update(hh, state, s, j):
        m_old, acc = state
        off = pl.multiple_of(j * tk, tk)
        s = jnp.where(diff >= (j - i) * tk, s, NEG_BIG)
        m_new = jnp.maximum(m_old, jnp.max(s, axis=-1, keepdims=True))
        p = jnp.exp2(s - m_new).astype(BF16)
        acc = jnp.exp2(m_old - m_new) * acc + _dot(p, va_ref[0, hh, pl.ds(off, tk), :])
        return m_new, acc

    def body(j, states):
        ss = [scores(hh, j) for hh in range(nh)]
        return tuple(update(hh, states[hh], ss[hh], j) for hh in range(nh))

    init = tuple((jnp.full((tq, 1), NEG_BIG, F32), jnp.zeros((tq, LANES), F32)) for _ in range(nh))
    states = lax.fori_loop(0, i + 1, body, init)
    lane = lax.broadcasted_iota(I32, (tq, LANES), 1)
    first = lane < ATTN_HEAD_DIM
    for pr in range(nh // 2):
        acc_e, acc_o = states[2 * pr][1], states[2 * pr + 1][1]
        sum_e = jnp.sum(jnp.where(lane == ATTN_HEAD_DIM, acc_e, 0.0), axis=-1, keepdims=True)
        sum_o = jnp.sum(jnp.where(lane == 0, acc_o, 0.0), axis=-1, keepdims=True)
        o = jnp.where(first, acc_e / sum_e, acc_o / sum_o)
        sq = o * o
        s_all = jnp.sum(sq, axis=-1, keepdims=True)
        s0 = jnp.sum(jnp.where(first, sq, 0.0), axis=-1, keepdims=True)
        ms = jnp.where(first, s0, s_all - s0) * (1.0 / ATTN_HEAD_DIM)
        gsl = slice(pr * LANES, (pr + 1) * LANES)
        o_ref[0, :, gsl] = (o * lax.rsqrt(ms + EPS) * g_ref[:, gsl]).astype(BF16)


def _fox_attn(qa, ka, va, g_row):
    bsz, _, seq_len, _ = qa.shape
    tq = ATT_TQ
    nh = ATT_HEADS_PER_STEP
    wo = nh // 2 * LANES
    kv_spec = pl.BlockSpec((1, nh, seq_len, AUG_W), lambda b, p, i: (b, p, 0, 0))
    return pl.pallas_call(
        _attn_kernel,
        out_shape=jax.ShapeDtypeStruct((bsz, seq_len, ATTN_DIM), BF16),
        grid=(bsz, ATTN_HEADS // nh, seq_len // tq),
        in_specs=[pl.BlockSpec((1, nh, tq, AUG_W), lambda b, p, i: (b, p, i, 0)),
                  kv_spec, kv_spec,
                  pl.BlockSpec((1, wo), lambda b, p, i: (0, p))],
        out_specs=pl.BlockSpec((1, tq, wo), lambda b, p, i: (b, i, p)),
        compiler_params=_cparams(("arbitrary", "arbitrary", "arbitrary")),
        name="fox_attn",
    )(qa, ka, va, g_row)


def _ssd_kernel(xbc_ref, z_ref, sm_ref, cw_ref, cb_ref, dtb_ref, alog_ref, dsk_ref, g_ref, tri_ref, exp_ref,
                o_ref, ext_ref, state_ref):
    c = pl.program_id(1)
    q = SSD_CHUNK

    @pl.when(c == 0)
    def _():
        ext_ref[0:8, :] = jnp.zeros((8, SSD_CONV_DIM), F32)
        state_ref[...] = jnp.zeros_like(state_ref)

    ext_ref[8:8 + q, :] = xbc_ref[0].astype(F32)
    conv = cb_ref[...] + cw_ref[0:1, :] * ext_ref[5:5 + q, :]
    for j in range(1, SSD_CONV):
        conv = conv + cw_ref[j:j + 1, :] * ext_ref[5 + j:5 + j + q, :]
    ext_ref[0:8, :] = ext_ref[q:q + 8, :]
    xc = _silu(conv)
    xs = xc[:, :SSD_DIM]

    dt = _softplus(sm_ref[0] + dtb_ref[...])
    a_dt = -jnp.exp(alog_ref[...]) * dt
    a_cs = _dot_exact_lhs01(tri_ref[...], a_dt)
    a_last = a_cs[q - 1:q, :]
    e_cs = jnp.exp(a_cs)
    dec = jnp.exp(a_last - a_cs)
    a_cs_t = a_cs.T
    expand = exp_ref[...]
    dt_x = _dot_exact_rhs01(dt, expand)
    e_x = _dot_exact_rhs01(e_cs, expand)
    dec_x = _dot_exact_rhs01(dec, expand)
    x_dt = xs * dt_x
    x_dec = (x_dt * dec_x).astype(BF16)
    x_dt_b = x_dt.astype(BF16)

    row = lax.broadcasted_iota(I32, (q, q), 0)
    col = lax.broadcasted_iota(I32, (q, q), 1)
    lower = row >= col
    lane = lax.broadcasted_iota(I32, (q, LANES), 1)
    first = lane < SSD_HEAD_DIM
    y_parts = []
    for g in range(SSD_GROUPS):
        b_g = xc[:, SSD_DIM + g * SSD_STATE:SSD_DIM + (g + 1) * SSD_STATE]
        c_g = xc[:, SSD_DIM + (SSD_GROUPS + g) * SSD_STATE:SSD_DIM + (SSD_GROUPS + g + 1) * SSD_STATE]
        c_gb = c_g.astype(BF16)
        cb = _dot_nt(c_gb, b_g.astype(BF16))
        gs = slice(g * SSD_GROUP_DIM, (g + 1) * SSD_GROUP_DIM)
        st_prev = state_ref[g]
        y_off = _dot(c_gb, st_prev.astype(BF16)) * e_x[:, gs]
        s_new = _dot(b_g.T.astype(BF16), x_dec[:, gs])
        state_ref[g] = st_prev * e_x[q - 1:q, gs] + s_new
        for pr in range(2):
            pair = []
            for hh in range(2):
                h = g * 4 + pr * 2 + hh
                a_col = a_cs[:, DT_LANE0 + h:DT_LANE0 + h + 1]
                a_row = a_cs_t[DT_LANE0 + h:DT_LANE0 + h + 1, :]
                lmat = jnp.where(lower, jnp.exp(jnp.minimum(a_col - a_row, 0.0)), 0.0)
                m_h = (cb * lmat).astype(BF16)
                ps = slice((g * 2 + pr) * LANES, (g * 2 + pr + 1) * LANES)
                pair.append(_dot(m_h, x_dt_b[:, ps]))
            y_parts.append(jnp.where(first, pair[0], pair[1]) + y_off[:, pr * LANES:(pr + 1) * LANES])
    y = jnp.concatenate(y_parts, axis=1) + dsk_ref[...] * xs
    y = y * _silu(z_ref[0].astype(F32))
    outs = []
    for g in range(SSD_GROUPS):
        yg = y[:, g * SSD_GROUP_DIM:(g + 1) * SSD_GROUP_DIM]
        ms = jnp.mean(yg * yg, axis=-1, keepdims=True)
        outs.append(yg * lax.rsqrt(ms + EPS))
    o_ref[0] = (jnp.concatenate(outs, axis=1) * g_ref[...]).astype(BF16)


def _ssd(xbc3, z3, sm3, conv_w, conv_b, dtb_row, alog_row, dskip_row, g_row, tri, expand):
    bsz, seq_len, _ = xbc3.shape
    q = SSD_CHUNK
    blk = lambda b, c: (b, c, 0)
    const = lambda b, c: (0, 0)
    return pl.pallas_call(
        _ssd_kernel,
        out_shape=jax.ShapeDtypeStruct((bsz, seq_len, SSD_DIM), BF16),
        grid=(bsz, seq_len // q),
        in_specs=[pl.BlockSpec((1, q, SSD_CONV_DIM), blk),
                  pl.BlockSpec((1, q, SSD_DIM), blk),
                  pl.BlockSpec((1, q, SMALL_W), blk),
                  pl.BlockSpec((SSD_CONV, SSD_CONV_DIM), const),
                  pl.BlockSpec((1, SSD_CONV_DIM), const),
                  pl.BlockSpec((1, SMALL_W), const),
                  pl.BlockSpec((1, SMALL_W), const),
                  pl.BlockSpec((1, SSD_DIM), const),
                  pl.BlockSpec((1, SSD_DIM), const),
                  pl.BlockSpec((q, q), const),
                  pl.BlockSpec((SMALL_W, SSD_DIM), const)],
        out_specs=pl.BlockSpec((1, q, SSD_DIM), blk),
        scratch_shapes=[pltpu.VMEM((q + 8, SSD_CONV_DIM), F32),
                        pltpu.VMEM((SSD_GROUPS, SSD_STATE, SSD_GROUP_DIM), F32)],
        compiler_params=_cparams(("arbitrary", "arbitrary")),
        name="ssd",
    )(xbc3, z3, sm3, conv_w, conv_b, dtb_row, alog_row, dskip_row, g_row, tri, expand)


def _post_kernel(attn_ref, ssd_ref, x_ref, g1_ref, sh2_ref, sc2_ref, g2_ref, woa_ref, wos_ref, n2_ref,
                 wrt_ref, rb_ref, wgs_ref, wus_ref, wds_ref, upper_ref,
                 x1s_ref, h2_ref, eidx_ref, wts_ref, rank_ref, cnt_ref):
    i = pl.program_id(0)
    tm = x_ref.shape[0]

    @pl.when(i == 0)
    def _():
        cnt_ref[...] = jnp.zeros_like(cnt_ref)

    mixed = _dot(attn_ref[...], woa_ref[...]) + _dot(ssd_ref[...], wos_ref[...])
    x1 = x_ref[...] + g1_ref[0] * mixed
    var = jnp.mean(x1 * x1, axis=-1, keepdims=True)
    h2 = x1 * lax.rsqrt(var + EPS) * n2_ref[...]
    h2 = h2 * (1.0 + sc2_ref[0]) + sh2_ref[0]
    hb = h2.astype(BF16)
    h2_ref[...] = _pack_bf16_pair(h2)
    act = _silu(_dot(hb, wgs_ref[...])) * _dot(hb, wus_ref[...])
    shared = _dot(act.astype(BF16), wds_ref[...])
    x1s_ref[...] = x1 + g2_ref[0] * shared

    scores = _sigmoid(_dot_nt(wrt_ref[...], hb))
    biased = scores + rb_ref[...]
    grp = biased.reshape(N_ROUTE_GROUPS, GROUP_SIZE, tm)
    gi = lax.broadcasted_iota(I32, grp.shape, 1)
    m1 = jnp.max(grp, axis=1, keepdims=True)
    i1 = jnp.min(jnp.where(grp == m1, gi, GROUP_SIZE), axis=1, keepdims=True)
    m2 = jnp.max(jnp.where(gi == i1, -jnp.inf, grp), axis=1, keepdims=True)
    gsc = (m1 + m2).reshape(N_ROUTE_GROUPS, tm)
    gidx = lax.broadcasted_iota(I32, gsc.shape, 0)
    beaten = jnp.zeros(gsc.shape, I32)
    for o in range(N_ROUTE_GROUPS):
        other = gsc[o:o + 1, :]
        beats = (other > gsc) | ((other == gsc) & (gidx > o))
        beaten = beaten + beats.astype(I32)
    gmask = (beaten < TOPK_ROUTE_GROUPS).astype(F32)
    emask = jnp.broadcast_to(gmask.reshape(N_ROUTE_GROUPS, 1, tm), grp.shape).reshape(N_EXPERTS, tm)
    masked = jnp.where(emask > 0.5, biased, -jnp.inf)
    eiota = lax.broadcasted_iota(I32, (N_EXPERTS, tm), 0)
    idx_rows, w_rows = [], []
    sel = jnp.zeros((N_EXPERTS, tm), F32)
    for _ in range(TOP_K):
        mk = jnp.max(masked, axis=0, keepdims=True)
        ik = jnp.min(jnp.where(masked == mk, eiota, N_EXPERTS), axis=0, keepdims=True)
        hit = eiota == ik
        w_rows.append(jnp.sum(jnp.where(hit, scores, 0.0), axis=0, keepdims=True))
        idx_rows.append(ik)
        masked = jnp.where(hit, -jnp.inf, masked)
        sel = jnp.where(hit, 1.0, sel)
    w_all = jnp.concatenate(w_rows, axis=0)
    wts_ref[...] = w_all / jnp.sum(w_all, axis=0, keepdims=True) * ROUTED_SCALE
    eidx_ref[...] = jnp.concatenate(idx_rows, axis=0)
    before = cnt_ref[...][:, 0:1] + _dot(sel.astype(BF16), upper_ref[...])
    rank_rows = [jnp.sum(jnp.where(eiota == ik, before, 0.0), axis=0, keepdims=True) for ik in idx_rows]
    rank_ref[...] = jnp.concatenate(rank_rows, axis=0).astype(I32)
    cnt_ref[...] = cnt_ref[...] + jnp.sum(sel, axis=1, keepdims=True)


def _post_mixer(attn2, ssd2, x2, mod3, w_out_a, w_out_s, norm2_g, w_router_t, rb_col,
                wgs, wus, wds, upper, seq_len):
    t, d = x2.shape
    tm = POST_TM
    per_b = seq_len // tm
    row = lambda i: (i, 0)
    const = lambda i: (0, 0)
    modspec = lambda k: pl.BlockSpec((1, 1, d), lambda i: ((i // per_b) * N_MOD + k, 0, 0))
    slot = lambda i: (0, i)
    return pl.pallas_call(
        _post_kernel,
        out_shape=(jax.ShapeDtypeStruct((t, d), F32),
                   jax.ShapeDtypeStruct((t, d // 2), U32),
                   jax.ShapeDtypeStruct((TOP_K, t), I32),
                   jax.ShapeDtypeStruct((TOP_K, t), F32),
                   jax.ShapeDtypeStruct((TOP_K, t), I32),
                   jax.ShapeDtypeStruct((N_EXPERTS, LANES), F32)),
        grid=(t // tm,),
        in_specs=[pl.BlockSpec((tm, ATTN_DIM), row),
                  pl.BlockSpec((tm, SSD_DIM), row),
                  pl.BlockSpec((tm, d), row),
                  modspec(2), modspec(3), modspec(4), modspec(5),
                  pl.BlockSpec((ATTN_DIM, d), const),
                  pl.BlockSpec((SSD_DIM, d), const),
                  pl.BlockSpec((1, d), const),
                  pl.BlockSpec((N_EXPERTS, d), const),
                  pl.BlockSpec((N_EXPERTS, 1), const),
                  pl.BlockSpec((d, EXPERT_DIM), const),
                  pl.BlockSpec((d, EXPERT_DIM), const),
                  pl.BlockSpec((EXPERT_DIM, d), const),
                  pl.BlockSpec((tm, tm), const)],
        out_specs=(pl.BlockSpec((tm, d), row),
                   pl.BlockSpec((tm, d // 2), row),
                   pl.BlockSpec((TOP_K, tm), slot),
                   pl.BlockSpec((TOP_K, tm), slot),
                   pl.BlockSpec((TOP_K, tm), slot),
                   pl.BlockSpec((N_EXPERTS, LANES), const)),
        compiler_params=_cparams(("arbitrary",)),
        name="post_mixer",
    )(attn2, ssd2, x2, mod3, mod3, mod3, mod3, w_out_a, w_out_s, norm2_g.reshape(1, d),
      w_router_t, rb_col, wgs, wus, wds, upper)


def _dest_kernel(eidx_ref, rank_ref, pst_ref, dest_ref):
    tm = eidx_ref.shape[1]
    eiota = lax.broadcasted_iota(I32, (N_EXPERTS, tm), 0)
    rows = []
    for k in range(TOP_K):
        onehot = jnp.where(eiota == eidx_ref[k:k + 1, :], 1.0, 0.0).astype(BF16)
        r = _dot(pst_ref[...], onehot)
        rows.append(r[0:1, :] + r[1:2, :] + r[2:3, :])
    dest_ref[...] = jnp.concatenate(rows, axis=0).astype(I32) + rank_ref[...]


def _dest_rows(eidx, rank, pstart):
    k, t = eidx.shape
    tm = min(DEST_TM, t)
    pieces = jnp.stack([pstart & 0xFF0000, pstart & 0xFF00, pstart & 0xFF], axis=0)
    pst = jnp.zeros((8, N_EXPERTS), F32).at[:3].set(pieces.astype(F32)).astype(BF16)
    slot = lambda i: (0, i)
    return pl.pallas_call(
        _dest_kernel,
        out_shape=jax.ShapeDtypeStruct((k, t), I32),
        grid=(t // tm,),
        in_specs=[pl.BlockSpec((k, tm), slot), pl.BlockSpec((k, tm), slot),
                  pl.BlockSpec((8, N_EXPERTS), lambda i: (0, 0))],
        out_specs=pl.BlockSpec((k, tm), slot),
        compiler_params=_cparams(("arbitrary",)),
        name="dest_rows",
    )(eidx, rank, pst)


def _sc_mesh():
    return plsc.VectorSubcoreMesh(core_axis_name="c", subcore_axis_name="s")


def _sc_worker(n_workers_per_core):
    return lax.axis_index("s") * n_workers_per_core + lax.axis_index("c")


def _dispatch(dest, h2p, n_rows):
    t, w = h2p.shape
    mesh = _sc_mesh()
    n_workers = mesh.num_cores * mesh.num_subcores
    per_w = t // n_workers
    ch = min(SC_CHUNK, per_w)

    def body(dest_hbm, h_hbm, xs_hbm, idx_v, rows_v, sem):
        base_w = _sc_worker(mesh.num_cores) * per_w

        @pl.loop(0, per_w // ch)
        def _(ci):
            base = pl.multiple_of(base_w + ci * ch, ch)
            pltpu.sync_copy(dest_hbm.at[:, pl.ds(base, ch)], idx_v)
            pltpu.sync_copy(h_hbm.at[pl.ds(base, ch)], rows_v)
            copies = [pltpu.async_copy(rows_v, xs_hbm.at[idx_v.at[k]], sem) for k in range(TOP_K)]
            for cp in copies:
                cp.wait()

    return pl.kernel(
        body,
        out_type=jax.ShapeDtypeStruct((n_rows, w), U32),
        mesh=mesh,
        scratch_types=[pltpu.VMEM((TOP_K, ch), I32), pltpu.VMEM((ch, w), U32), pltpu.SemaphoreType.DMA],
        name="dispatch",
    )(dest, h2p)


def _undispatch(dest, ys, t0, t):
    w = ys.shape[1]
    mesh = _sc_mesh()
    n_workers = mesh.num_cores * mesh.num_subcores
    per_w = t // n_workers
    ch = min(SC_CHUNK, per_w)

    def body(dest_hbm, ys_hbm, ytok_hbm, idx_v, rows_v, sem):
        base_w = _sc_worker(mesh.num_cores) * per_w

        @pl.loop(0, per_w // ch)
        def _(ci):
            base = pl.multiple_of(base_w + ci * ch, ch)
            pltpu.sync_copy(dest_hbm.at[:, pl.ds(t0 + base, ch)], idx_v)
            for k in range(TOP_K):
                pltpu.async_copy(ys_hbm.at[idx_v.at[k]], rows_v, sem).wait()
                pltpu.sync_copy(rows_v, ytok_hbm.at[k, pl.ds(base, ch)])

    return pl.kernel(
        body,
        out_type=jax.ShapeDtypeStruct((TOP_K, t, w), U32),
        mesh=mesh,
        scratch_types=[pltpu.VMEM((TOP_K, ch), I32), pltpu.VMEM((ch, w), U32), pltpu.SemaphoreType.DMA],
        name="undispatch",
    )(dest, ys)


def _expert_kernel(be_ref, nu_ref, x_ref, wg_ref, wu_ref, wd_ref, y_ref, act_ref, wgb_ref, wub_ref, wdb_ref):
    b = pl.program_id(0)
    n_used = nu_ref[0]
    e_cur = be_ref[jnp.minimum(b, n_used - 1)]
    e_prev = be_ref[jnp.clip(b - 1, 0, n_used - 1)]
    e_prev2 = be_ref[jnp.clip(b - 2, 0, n_used - 1)]

    @pl.when((b == 0) | (e_cur != e_prev))
    def _():
        wgb_ref[...] = wg_ref[0].astype(BF16)
        wub_ref[...] = wu_ref[0].astype(BF16)

    @pl.when((b == 1) | (e_prev != e_prev2))
    def _():
        wdb_ref[...] = wd_ref[0].astype(BF16)

    def gate_up():
        x = _unpack_bf16_pair(x_ref[...]).astype(BF16)
        return (_silu(_dot(x, wgb_ref[...])) * _dot(x, wub_ref[...])).astype(BF16)

    def down():
        y_ref[...] = _pack_bf16_pair(_dot(act_ref[...], wdb_ref[...]))

    @pl.when(b == 0)
    def _():
        act_ref[...] = gate_up()

    @pl.when((b > 0) & (b < n_used))
    def _():
        down()
        act_ref[...] = gate_up()

    @pl.when(b == n_used)
    def _():
        down()


def _experts(block_expert, n_used, xs, wg, wu, wd):
    n_rows, w = xs.shape
    n_blocks = n_rows // EXP_BLK
    d, f = wg.shape[1], wg.shape[2]
    cur = lambda b, be, nu: (jnp.minimum(b, nu[0] - 1), 0)
    prev = lambda b, be, nu: (jnp.clip(b - 1, 0, nu[0] - 1), 0)
    wcur = lambda b, be, nu: (be[jnp.minimum(b, nu[0] - 1)], 0, 0)
    wprev = lambda b, be, nu: (be[jnp.clip(b - 1, 0, nu[0] - 1)], 0, 0)
    grid_spec = pltpu.PrefetchScalarGridSpec(
        num_scalar_prefetch=2,
        grid=(n_blocks + 1,),
        in_specs=[pl.BlockSpec((EXP_BLK, w), cur),
                  pl.BlockSpec((1, d, f), wcur),
                  pl.BlockSpec((1, d, f), wcur),
                  pl.BlockSpec((1, f, d), wprev)],
        out_specs=pl.BlockSpec((EXP_BLK, w), prev),
        scratch_shapes=[pltpu.VMEM((EXP_BLK, f), BF16), pltpu.VMEM((d, f), BF16), pltpu.VMEM((d, f), BF16),
                        pltpu.VMEM((f, d), BF16)],
    )
    return pl.pallas_call(
        _expert_kernel,
        out_shape=jax.ShapeDtypeStruct((n_rows, w), U32),
        grid_spec=grid_spec,
        compiler_params=_cparams(("arbitrary",)),
        name="experts",
    )(block_expert, n_used, xs, wg, wu, wd)


def _combine_kernel(ytok_ref, x1s_ref, w_ref, g2_ref, nf_ref, mf0_ref, mf1_ref, o_ref):
    w = w_ref[...]
    routed = w[:, 0:1] * _unpack_bf16_pair(ytok_ref[0])
    for k in range(1, TOP_K):
        routed = routed + w[:, k:k + 1] * _unpack_bf16_pair(ytok_ref[k])
    xo = x1s_ref[...] + g2_ref[0] * routed
    var = jnp.mean(xo * xo, axis=-1, keepdims=True)
    y = xo * lax.rsqrt(var + EPS) * nf_ref[...]
    o_ref[...] = y * (1.0 + mf1_ref[0]) + mf0_ref[0]


def _combine_alias_kernel(prev_ref, *refs):
    del prev_ref
    _combine_kernel(*refs)


def _combine(ytok, x1s, wts_t, mod3, normf_g, modf3, seq_len, part, out_prev):
    t, d = x1s.shape
    tm = COMB_TM
    steps = ytok.shape[1] // tm
    i0 = part * steps
    per_b = seq_len // tm
    row = lambda i: (i0 + i, 0)
    in_specs = [pl.BlockSpec((TOP_K, tm, d // 2), lambda i: (0, i, 0)),
                pl.BlockSpec((tm, d), row),
                pl.BlockSpec((tm, TOP_K), row),
                pl.BlockSpec((1, 1, d), lambda i: (((i0 + i) // per_b) * N_MOD + 5, 0, 0)),
                pl.BlockSpec((1, d), lambda i: (0, 0)),
                pl.BlockSpec((1, 1, d), lambda i: (((i0 + i) // per_b) * 2 + 0, 0, 0)),
                pl.BlockSpec((1, 1, d), lambda i: (((i0 + i) // per_b) * 2 + 1, 0, 0))]
    args = (ytok, x1s, wts_t, mod3, normf_g.reshape(1, d), modf3, modf3)
    if out_prev is None:
        body, aliases = _combine_kernel, {}
    else:
        body, aliases = _combine_alias_kernel, {0: 0}
        in_specs = [pl.BlockSpec(memory_space=pl.ANY)] + in_specs
        args = (out_prev,) + args
    return pl.pallas_call(
        body,
        out_shape=jax.ShapeDtypeStruct((t, d), F32),
        grid=(steps,),
        in_specs=in_specs,
        out_specs=pl.BlockSpec((tm, d), row),
        input_output_aliases=aliases,
        compiler_params=_cparams(("arbitrary",)),
        name="combine",
    )(*args)


def _lane_row(vec, lane0):
    return jnp.zeros((1, SMALL_W), F32).at[0, lane0:lane0 + vec.shape[0]].set(vec.astype(F32))


def _layer(x2, mod3, bsz, seq_len, norm1_g, w_in, fg_bias, conv_w, conv_b, dt_bias, a_log, d_skip,
           attn_norm_g, ssd_norm_g, w_out, norm2_g, w_router, router_bias,
           w_gate_e, w_up_e, w_down_e, w_gate_s, w_up_s, w_down_s):
    t, d = x2.shape
    o_q, o_k, o_v, o_fg = 0, ATTN_DIM, 2 * ATTN_DIM, 3 * ATTN_DIM
    o_z = o_fg + ATTN_HEADS
    o_xbc = o_z + SSD_DIM
    o_dt = o_xbc + SSD_CONV_DIM
    small = jnp.zeros((d, SMALL_W), F32)
    small = small.at[:, FG_LANE0:FG_LANE0 + ATTN_HEADS].set(w_in[:, o_fg:o_z])
    small = small.at[:, DT_LANE0:DT_LANE0 + SSD_HEADS].set(w_in[:, o_dt:o_dt + SSD_HEADS])
    w_cat = jnp.concatenate([w_in[:, o_q:o_fg], w_in[:, o_z:o_dt], small], axis=1).astype(BF16)

    tri_in = jnp.tril(jnp.ones((IN_TM, IN_TM), F32)).astype(BF16)
    qa, ka, va, z2, xbc2, sm2 = _in_proj(x2, mod3, norm1_g, w_cat, _lane_row(fg_bias, FG_LANE0), tri_in,
                                         bsz, seq_len)
    shp = lambda a: a.reshape(bsz, seq_len, a.shape[-1])
    attn3 = _fox_attn(qa, ka, va, attn_norm_g.reshape(1, ATTN_DIM).astype(F32))

    tri_chunk = jnp.tril(jnp.ones((SSD_CHUNK, SSD_CHUNK), F32)).astype(BF16)
    head_of_lane = jnp.arange(SSD_DIM, dtype=I32) // SSD_HEAD_DIM
    expand = (jnp.arange(SMALL_W, dtype=I32)[:, None] == head_of_lane[None, :] + DT_LANE0).astype(BF16)
    dskip_row = jnp.repeat(d_skip.astype(F32), SSD_HEAD_DIM).reshape(1, SSD_DIM)
    ssd3 = _ssd(shp(xbc2), shp(z2), shp(sm2), conv_w.astype(F32), conv_b.reshape(1, -1).astype(F32),
                _lane_row(dt_bias, DT_LANE0), _lane_row(a_log, DT_LANE0), dskip_row,
                ssd_norm_g.reshape(1, SSD_DIM).astype(F32), tri_chunk, expand)

    upper = jnp.triu(jnp.ones((POST_TM, POST_TM), F32), 1).astype(BF16)
    x1s, h2p, eidx, wts, rank, cnt = _post_mixer(
        attn3.reshape(t, ATTN_DIM), ssd3.reshape(t, SSD_DIM), x2, mod3,
        w_out[:ATTN_DIM].astype(BF16), w_out[ATTN_DIM:].astype(BF16), norm2_g,
        w_router.T.astype(BF16), router_bias.reshape(N_EXPERTS, 1).astype(F32),
        w_gate_s.astype(BF16), w_up_s.astype(BF16), w_down_s.astype(BF16), upper, seq_len)

    counts = cnt[:, 0].astype(I32)
    padded = (counts + EXP_BLK - 1) // EXP_BLK * EXP_BLK
    pend = jnp.cumsum(padded)
    pstart = pend - padded
    n_blocks = (t * TOP_K) // EXP_BLK + N_EXPERTS
    block_start = jnp.arange(n_blocks, dtype=I32) * EXP_BLK
    block_expert = jnp.minimum(jnp.sum((pend[None, :] <= block_start[:, None]).astype(I32), axis=1),
                               N_EXPERTS - 1)
    n_used = (pend[-1] // EXP_BLK).astype(I32).reshape(1)
    dest = _dest_rows(eidx, rank, pstart)

    xs = _dispatch(dest, h2p, n_blocks * EXP_BLK)
    ys = _experts(block_expert, n_used, xs, w_gate_e, w_up_e, w_down_e)
    return dest, ys, x1s, wts


def kernel(x, c, norm1_g, w_ada, b_ada, w_in, fg_bias, conv_w, conv_b, dt_bias, a_log, d_skip, attn_norm_g,
           ssd_norm_g, w_out, norm2_g, w_router, router_bias, w_gate_e, w_up_e, w_down_e, w_gate_s, w_up_s,
           w_down_s, normf_g, w_ada_f, b_ada_f):
    bsz, seq_len, d = x.shape
    assert w_ada.shape[0] == 1, "single-layer kernel"
    t = bsz * seq_len
    mod3 = _modulation(c, w_ada[0], b_ada[0]).reshape(bsz * N_MOD, 1, d)
    modf3 = _modulation(c, w_ada_f, b_ada_f).reshape(bsz * 2, 1, d)
    x2 = x.reshape(t, d)
    dest, ys, x1s, wts = _layer(
        x2, mod3, bsz, seq_len, norm1_g[0], w_in[0], fg_bias[0], conv_w[0], conv_b[0], dt_bias[0], a_log[0],
        d_skip[0], attn_norm_g[0], ssd_norm_g[0], w_out[0], norm2_g[0], w_router[0], router_bias[0],
        w_gate_e[0], w_up_e[0], w_down_e[0], w_gate_s[0], w_up_s[0], w_down_s[0])
    wts_t = wts.T
    tp = t // COMB_PARTS
    out = None
    for p in range(COMB_PARTS):
        ytok = _undispatch(dest, ys, p * tp, tp)
        out = _combine(ytok, x1s, wts_t, mod3, normf_g, modf3, seq_len, p, out)
    return out.reshape(bsz, seq_len, d)
```

```python
import functools

import jax
import jax.numpy as jnp
from jax import lax
from jax.experimental import pallas as pl
from jax.experimental.pallas import tpu as pltpu
from jax.experimental.pallas import tpu_sc as plsc

F32 = jnp.float32
BF16 = jnp.bfloat16
I32 = jnp.int32
U32 = jnp.uint32

EPS = 1e-6
D_MODEL = 1024
N_MOD = 6

ATTN_HEADS = 8
ATTN_HEAD_DIM = 64
ATTN_DIM = ATTN_HEADS * ATTN_HEAD_DIM

SSD_HEADS = 8
SSD_HEAD_DIM = 64
SSD_DIM = SSD_HEADS * SSD_HEAD_DIM
SSD_GROUPS = 2
SSD_STATE = 128
SSD_CONV = 4
SSD_CHUNK = 128
SSD_CONV_DIM = SSD_DIM + 2 * SSD_GROUPS * SSD_STATE
SSD_GROUP_DIM = SSD_DIM // SSD_GROUPS

N_EXPERTS = 256
TOP_K = 8
N_ROUTE_GROUPS = 8
TOPK_ROUTE_GROUPS = 4
GROUP_SIZE = N_EXPERTS // N_ROUTE_GROUPS
EXPERT_DIM = 256
ROUTED_SCALE = 2.5

LANES = 128
SMALL_W = LANES
FG_LANE0 = 0
DT_LANE0 = 8
AUG_W = LANES

IN_TM = 512
ATT_TQ = 512
ATT_TK = 512
ATT_HEADS_PER_STEP = 8
POST_TM = 512
SC_CHUNK = 128
EXP_BLK = 1024
COMB_TM = 256
COMB_PARTS = 8
VMEM_LIMIT = 56 * 1024 * 1024
NEG_BIG = -1e30
LOG2E = 1.4426950408889634
DEST_TM = 2048


def _split3(x):
    hi = x.astype(BF16)
    r1 = x - hi.astype(F32)
    mid = r1.astype(BF16)
    lo = (r1 - mid.astype(F32)).astype(BF16)
    return hi, mid, lo


def _dot(a, b):
    return jnp.dot(a, b, preferred_element_type=F32)


def _dot_nt(a, b):
    return lax.dot_general(a, b, (((1,), (1,)), ((), ())), preferred_element_type=F32)


def _dot_exact_lhs01(lhs_bf16, x, pieces=3):
    parts = _split3(x)[:pieces]
    out = _dot(lhs_bf16, parts[0])
    for p in parts[1:]:
        out = out + _dot(lhs_bf16, p)
    return out


def _dot_exact_rhs01(x, rhs_bf16, pieces=2):
    parts = _split3(x)[:pieces]
    out = _dot(parts[0], rhs_bf16)
    for p in parts[1:]:
        out = out + _dot(p, rhs_bf16)
    return out


def _sigmoid(x):
    return 1.0 / (1.0 + jnp.exp(-x))


def _silu(x):
    return x * _sigmoid(x)


def _softplus(x):
    return jnp.maximum(x, 0.0) + jnp.log(1.0 + jnp.exp(-jnp.abs(x)))


def _log_sigmoid(x):
    return jnp.minimum(x, 0.0) - jnp.log(1.0 + jnp.exp(-jnp.abs(x)))


def _pack_bf16_pair(x):
    n = x.shape[1] // 2
    lo = pltpu.bitcast(x[:, :n].astype(BF16).astype(F32), U32)
    hi = pltpu.bitcast(x[:, n:].astype(BF16).astype(F32), U32)
    return (hi & jnp.uint32(0xFFFF0000)) | (lo >> 16)


def _unpack_bf16_pair(w):
    lo = pltpu.bitcast(w << 16, F32)
    hi = pltpu.bitcast(w & jnp.uint32(0xFFFF0000), F32)
    return jnp.concatenate([lo, hi], axis=1)


def _cparams(sem):
    return pltpu.CompilerParams(dimension_semantics=sem, vmem_limit_bytes=VMEM_LIMIT)


def _mod_kernel(c_ref, w_ref, b_ref, o_ref):
    c = c_ref[...]
    o_ref[...] = jnp.dot(_silu(c), w_ref[...], preferred_element_type=F32,
                         precision=lax.Precision.HIGHEST) + b_ref[...]


def _modulation(c, w, b):
    bsz, d = c.shape
    n = w.shape[1]
    tn = 1024
    return pl.pallas_call(
        _mod_kernel,
        out_shape=jax.ShapeDtypeStruct((bsz, n), F32),
        grid=(n // tn,),
        in_specs=[pl.BlockSpec((bsz, d), lambda j: (0, 0)),
                  pl.BlockSpec((d, tn), lambda j: (0, j)),
                  pl.BlockSpec((1, tn), lambda j: (0, j))],
        out_specs=pl.BlockSpec((bsz, tn), lambda j: (0, j)),
        compiler_params=_cparams(("arbitrary",)),
        name="modulation",
    )(c, w, b.reshape(1, n))


_COL_Q, _COL_K, _COL_V, _COL_Z, _COL_XBC, _COL_SM, _COL_END = 0, 512, 1024, 1536, 2048, 3072, 3200


def _inproj_kernel(per_b, x_ref, g_ref, sc_ref, sh_ref, w_ref, fgb_ref, tri_ref,
                   qa_ref, ka_ref, va_ref, z_ref, xbc_ref, sm_ref, carry_ref):
    tm = x_ref.shape[0]

    @pl.when(pl.program_id(0) % per_b == 0)
    def _():
        carry_ref[...] = jnp.zeros_like(carry_ref)

    x = x_ref[...]
    var = jnp.mean(x * x, axis=-1, keepdims=True)
    h = x * lax.rsqrt(var + EPS) * g_ref[...]
    h = h * (1.0 + sc_ref[0]) + sh_ref[0]
    hb = h.astype(BF16)
    sm = _dot(hb, w_ref[:, _COL_SM:_COL_END])
    sm_ref[...] = sm
    log_f = _log_sigmoid(sm + fgb_ref[...]) * LOG2E
    cum = _dot_exact_lhs01(tri_ref[...], log_f) + carry_ref[...]
    carry_ref[...] = cum[tm - 1:tm, :]
    hi, mid, lo = (p.astype(F32) for p in _split3(cum))
    lane = lax.broadcasted_iota(I32, (tm, AUG_W), 1)
    left = lane < ATTN_HEAD_DIM

    def head_rows(out_ref, col0, scale, book_of_head):
        f = _dot(hb, w_ref[:, col0:col0 + ATTN_DIM])
        if scale is not None:
            f = f * scale
        for pr in range(ATTN_HEADS // 2):
            pair = f[:, pr * AUG_W:(pr + 1) * AUG_W]
            out_ref[0, 2 * pr] = jnp.where(left, pair, book_of_head(2 * pr, ATTN_HEAD_DIM)).astype(BF16)
            out_ref[0, 2 * pr + 1] = jnp.where(left, book_of_head(2 * pr + 1, 0), pair).astype(BF16)

    def cols(hd):
        c0 = FG_LANE0 + hd
        return hi[:, c0:c0 + 1], mid[:, c0:c0 + 1], lo[:, c0:c0 + 1]

    def book_q(hd, base):
        chi, cmid, clo = cols(hd)
        rel = lane - base
        return jnp.where(rel == 0, chi, jnp.where(rel == 1, cmid, jnp.where(rel == 2, clo,
                         jnp.where((rel >= 3) & (rel < 6), 1.0, 0.0))))

    def book_k(hd, base):
        chi, cmid, clo = cols(hd)
        rel = lane - base
        return jnp.where(rel == 3, -chi, jnp.where(rel == 4, -cmid, jnp.where(rel == 5, -clo,
                         jnp.where((rel >= 0) & (rel < 3), 1.0, 0.0))))

    def book_v(hd, base):
        return jnp.where(lane == base, 1.0, 0.0)

    head_rows(qa_ref, _COL_Q, ATTN_HEAD_DIM ** -0.5 * LOG2E, book_q)
    head_rows(ka_ref, _COL_K, None, book_k)
    head_rows(va_ref, _COL_V, None, book_v)
    z_ref[...] = _dot(hb, w_ref[:, _COL_Z:_COL_XBC]).astype(BF16)
    xbc_ref[:, :512] = _dot(hb, w_ref[:, _COL_XBC:_COL_XBC + 512]).astype(BF16)
    xbc_ref[:, 512:] = _dot(hb, w_ref[:, _COL_XBC + 512:_COL_SM]).astype(BF16)


def _in_proj(x2, mod3, norm_g, w_cat, fgb_row, tri, bsz, seq_len):
    t, d = x2.shape
    tm = IN_TM
    per_b = seq_len // tm
    row = lambda i: (i, 0)
    const = lambda i: (0, 0)
    aug = jax.ShapeDtypeStruct((bsz, ATTN_HEADS, seq_len, AUG_W), BF16)
    aug_spec = pl.BlockSpec((1, ATTN_HEADS, tm, AUG_W), lambda i: (i // per_b, 0, i % per_b, 0))
    return pl.pallas_call(
        functools.partial(_inproj_kernel, per_b),
        out_shape=(aug, aug, aug,
                   jax.ShapeDtypeStruct((t, SSD_DIM), BF16),
                   jax.ShapeDtypeStruct((t, SSD_CONV_DIM), BF16),
                   jax.ShapeDtypeStruct((t, SMALL_W), F32)),
        grid=(t // tm,),
        in_specs=[pl.BlockSpec((tm, d), row),
                  pl.BlockSpec((1, d), const),
                  pl.BlockSpec((1, 1, d), lambda i: ((i // per_b) * N_MOD + 1, 0, 0)),
                  pl.BlockSpec((1, 1, d), lambda i: ((i // per_b) * N_MOD + 0, 0, 0)),
                  pl.BlockSpec((d, _COL_END), const),
                  pl.BlockSpec((1, SMALL_W), const),
                  pl.BlockSpec((tm, tm), const)],
        out_specs=(aug_spec, aug_spec, aug_spec,
                   pl.BlockSpec((tm, SSD_DIM), row),
                   pl.BlockSpec((tm, SSD_CONV_DIM), row),
                   pl.BlockSpec((tm, SMALL_W), row)),
        scratch_shapes=[pltpu.VMEM((1, SMALL_W), F32)],
        compiler_params=_cparams(("arbitrary",)),
        name="in_proj",
    )(x2, norm_g.reshape(1, d), mod3, mod3, w_cat, fgb_row, tri)


def _attn_kernel(qa_ref, ka_ref, va_ref, g_ref, o_ref):
    i = pl.program_id(2)
    tq, tk = ATT_TQ, ATT_TK
    nh = ATT_HEADS_PER_STEP
    diff = lax.broadcasted_iota(I32, (tq, tk), 0) - lax.broadcasted_iota(I32, (tq, tk), 1)

    def scores(hh, j):
        off = pl.multiple_of(j * tk, tk)
        return _dot_nt(qa_ref[0, hh], ka_ref[0, hh, pl.ds(off, tk), :])

    def update(hh, state, s, j):
        m_old, acc = state
        off = pl.multiple_of(j * tk, tk)
        s = jnp.where(diff >= (j - i) * tk, s, NEG_BIG)
        m_new = jnp.maximum(m_old, jnp.max(s, axis=-1, keepdims=True))
        p = jnp.exp2(s - m_new).astype(BF16)
        acc = jnp.exp2(m_old - m_new) * acc + _dot(p, va_ref[0, hh, pl.ds(off, tk), :])
        return m_new, acc

    def body(j, states):
        ss = [scores(hh, j) for hh in range(nh)]
        return tuple(update(hh, states[hh], ss[hh], j) for hh in range(nh))

    init = tuple((jnp.full((tq, 1), NEG_BIG, F32), jnp.zeros((tq, LANES), F32)) for _ in range(nh))
    states = lax.fori_loop(0, i + 1, body, init)
    lane = lax.broadcasted_iota(I32, (tq, LANES), 1)
    first = lane < ATTN_HEAD_DIM
    for pr in range(nh // 2):
        acc_e, acc_o = states[2 * pr][1], states[2 * pr + 1][1]
        sum_e = jnp.sum(jnp.where(lane == ATTN_HEAD_DIM, acc_e, 0.0), axis=-1, keepdims=True)
        sum_o = jnp.sum(jnp.where(lane == 0, acc_o, 0.0), axis=-1, keepdims=True)
        o = jnp.where(first, acc_e / sum_e, acc_o / sum_o)
        sq = o * o
        s_all = jnp.sum(sq, axis=-1, keepdims=True)
        s0 = jnp.sum(jnp.where(first, sq, 0.0), axis=-1, keepdims=True)
        ms = jnp.where(first, s0, s_all - s0) * (1.0 / ATTN_HEAD_DIM)
        gsl = slice(pr * LANES, (pr + 1) * LANES)
        o_ref[0, :, gsl] = (o * lax.rsqrt(ms + EPS) * g_ref[:, gsl]).astype(BF16)


def _fox_attn(qa, ka, va, g_row):
    bsz, _, seq_len, _ = qa.shape
    tq = ATT_TQ
    nh = ATT_HEADS_PER_STEP
    wo = nh // 2 * LANES
    kv_spec = pl.BlockSpec((1, nh, seq_len, AUG_W), lambda b, p, i: (b, p, 0, 0))
    return pl.pallas_call(
        _attn_kernel,
        out_shape=jax.ShapeDtypeStruct((bsz, seq_len, ATTN_DIM), BF16),
        grid=(bsz, ATTN_HEADS // nh, seq_len // tq),
        in_specs=[pl.BlockSpec((1, nh, tq, AUG_W), lambda b, p, i: (b, p, i, 0)),
                  kv_spec, kv_spec,
                  pl.BlockSpec((1, wo), lambda b, p, i: (0, p))],
        out_specs=pl.BlockSpec((1, tq, wo), lambda b, p, i: (b, i, p)),
        compiler_params=_cparams(("arbitrary", "arbitrary", "arbitrary")),
        name="fox_attn",
    )(qa, ka, va, g_row)


def _ssd_kernel(xbc_ref, z_ref, sm_ref, cw_ref, cb_ref, dtb_ref, alog_ref, dsk_ref, g_ref, tri_ref, exp_ref,
                o_ref, ext_ref, state_ref):
    c = pl.program_id(1)
    q = SSD_CHUNK

    @pl.when(c == 0)
    def _():
        ext_ref[0:8, :] = jnp.zeros((8, SSD_CONV_DIM), F32)
        state_ref[...] = jnp.zeros_like(state_ref)

    ext_ref[8:8 + q, :] = xbc_ref[0].astype(F32)
    conv = cb_ref[...] + cw_ref[0:1, :] * ext_ref[5:5 + q, :]
    for j in range(1, SSD_CONV):
        conv = conv + cw_ref[j:j + 1, :] * ext_ref[5 + j:5 + j + q, :]
    ext_ref[0:8, :] = ext_ref[q:q + 8, :]
    xc = _silu(conv)
    xs = xc[:, :SSD_DIM]

    dt = _softplus(sm_ref[0] + dtb_ref[...])
    a_dt = -jnp.exp(alog_ref[...]) * dt
    a_cs = _dot_exact_lhs01(tri_ref[...], a_dt)
    a_last = a_cs[q - 1:q, :]
    e_cs = jnp.exp(a_cs)
    dec = jnp.exp(a_last - a_cs)
    a_cs_t = a_cs.T
    expand = exp_ref[...]
    dt_x = _dot_exact_rhs01(dt, expand)
    e_x = _dot_exact_rhs01(e_cs, expand)
    dec_x = _dot_exact_rhs01(dec, expand)
    x_dt = xs * dt_x
    x_dec = (x_dt * dec_x).astype(BF16)
    x_dt_b = x_dt.astype(BF16)

    row = lax.broadcasted_iota(I32, (q, q), 0)
    col = lax.broadcasted_iota(I32, (q, q), 1)
    lower = row >= col
    lane = lax.broadcasted_iota(I32, (q, LANES), 1)
    first = lane < SSD_HEAD_DIM
    y_parts = []
    for g in range(SSD_GROUPS):
        b_g = xc[:, SSD_DIM + g * SSD_STATE:SSD_DIM + (g + 1) * SSD_STATE]
        c_g = xc[:, SSD_DIM + (SSD_GROUPS + g) * SSD_STATE:SSD_DIM + (SSD_GROUPS + g + 1) * SSD_STATE]
        c_gb = c_g.astype(BF16)
        cb = _dot_nt(c_gb, b_g.astype(BF16))
        gs = slice(g * SSD_GROUP_DIM, (g + 1) * SSD_GROUP_DIM)
        st_prev = state_ref[g]
        y_off = _dot(c_gb, st_prev.astype(BF16)) * e_x[:, gs]
        s_new = _dot(b_g.T.astype(BF16), x_dec[:, gs])
        state_ref[g] = st_prev * e_x[q - 1:q, gs] + s_new
        for pr in range(2):
            pair = []
            for hh in range(2):
                h = g * 4 + pr * 2 + hh
                a_col = a_cs[:, DT_LANE0 + h:DT_LANE0 + h + 1]
                a_row = a_cs_t[DT_LANE0 + h:DT_LANE0 + h + 1, :]
                lmat = jnp.where(lower, jnp.exp(jnp.minimum(a_col - a_row, 0.0)), 0.0)
                m_h = (cb * lmat).astype(BF16)
                ps = slice((g * 2 + pr) * LANES, (g * 2 + pr + 1) * LANES)
                pair.append(_dot(m_h, x_dt_b[:, ps]))
            y_parts.append(jnp.where(first, pair[0], pair[1]) + y_off[:, pr * LANES:(pr + 1) * LANES])
    y = jnp.concatenate(y_parts, axis=1) + dsk_ref[...] * xs
    y = y * _silu(z_ref[0].astype(F32))
    outs = []
    for g in range(SSD_GROUPS):
        yg = y[:, g * SSD_GROUP_DIM:(g + 1) * SSD_GROUP_DIM]
        ms = jnp.mean(yg * yg, axis=-1, keepdims=True)
        outs.append(yg * lax.rsqrt(ms + EPS))
    o_ref[0] = (jnp.concatenate(outs, axis=1) * g_ref[...]).astype(BF16)


def _ssd(xbc3, z3, sm3, conv_w, conv_b, dtb_row, alog_row, dskip_row, g_row, tri, expand):
    bsz, seq_len, _ = xbc3.shape
    q = SSD_CHUNK
    blk = lambda b, c: (b, c, 0)
    const = lambda b, c: (0, 0)
    return pl.pallas_call(
        _ssd_kernel,
        out_shape=jax.ShapeDtypeStruct((bsz, seq_len, SSD_DIM), BF16),
        grid=(bsz, seq_len // q),
        in_specs=[pl.BlockSpec((1, q, SSD_CONV_DIM), blk),
                  pl.BlockSpec((1, q, SSD_DIM), blk),
                  pl.BlockSpec((1, q, SMALL_W), blk),
                  pl.BlockSpec((SSD_CONV, SSD_CONV_DIM), const),
                  pl.BlockSpec((1, SSD_CONV_DIM), const),
                  pl.BlockSpec((1, SMALL_W), const),
                  pl.BlockSpec((1, SMALL_W), const),
                  pl.BlockSpec((1, SSD_DIM), const),
                  pl.BlockSpec((1, SSD_DIM), const),
                  pl.BlockSpec((q, q), const),
                  pl.BlockSpec((SMALL_W, SSD_DIM), const)],
        out_specs=pl.BlockSpec((1, q, SSD_DIM), blk),
        scratch_shapes=[pltpu.VMEM((q + 8, SSD_CONV_DIM), F32),
                        pltpu.VMEM((SSD_GROUPS, SSD_STATE, SSD_GROUP_DIM), F32)],
        compiler_params=_cparams(("arbitrary", "arbitrary")),
        name="ssd",
    )(xbc3, z3, sm3, conv_w, conv_b, dtb_row, alog_row, dskip_row, g_row, tri, expand)


def _post_kernel(attn_ref, ssd_ref, x_ref, g1_ref, sh2_ref, sc2_ref, g2_ref, woa_ref, wos_ref, n2_ref,
                 wrt_ref, rb_ref, wgs_ref, wus_ref, wds_ref, upper_ref,
                 x1s_ref, h2_ref, eidx_ref, wts_ref, rank_ref, cnt_ref):
    i = pl.program_id(0)
    tm = x_ref.shape[0]

    @pl.when(i == 0)
    def _():
        cnt_ref[...] = jnp.zeros_like(cnt_ref)

    mixed = _dot(attn_ref[...], woa_ref[...]) + _dot(ssd_ref[...], wos_ref[...])
    x1 = x_ref[...] + g1_ref[0] * mixed
    var = jnp.mean(x1 * x1, axis=-1, keepdims=True)
    h2 = x1 * lax.rsqrt(var + EPS) * n2_ref[...]
    h2 = h2 * (1.0 + sc2_ref[0]) + sh2_ref[0]
    hb = h2.astype(BF16)
    h2_ref[...] = _pack_bf16_pair(h2)
    act = _silu(_dot(hb, wgs_ref[...])) * _dot(hb, wus_ref[...])
    shared = _dot(act.astype(BF16), wds_ref[...])
    x1s_ref[...] = x1 + g2_ref[0] * shared

    scores = _sigmoid(_dot_nt(wrt_ref[...], hb))
    biased = scores + rb_ref[...]
    grp = biased.reshape(N_ROUTE_GROUPS, GROUP_SIZE, tm)
    gi = lax.broadcasted_iota(I32, grp.shape, 1)
    m1 = jnp.max(grp, axis=1, keepdims=True)
    i1 = jnp.min(jnp.where(grp == m1, gi, GROUP_SIZE), axis=1, keepdims=True)
    m2 = jnp.max(jnp.where(gi == i1, -jnp.inf, grp), axis=1, keepdims=True)
    gsc = (m1 + m2).reshape(N_ROUTE_GROUPS, tm)
    gidx = lax.broadcasted_iota(I32, gsc.shape, 0)
    beaten = jnp.zeros(gsc.shape, I32)
    for o in range(N_ROUTE_GROUPS):
        other = gsc[o:o + 1, :]
        beats = (other > gsc) | ((other == gsc) & (gidx > o))
        beaten = beaten + beats.astype(I32)
    gmask = (beaten < TOPK_ROUTE_GROUPS).astype(F32)
    emask = jnp.broadcast_to(gmask.reshape(N_ROUTE_GROUPS, 1, tm), grp.shape).reshape(N_EXPERTS, tm)
    masked = jnp.where(emask > 0.5, biased, -jnp.inf)
    eiota = lax.broadcasted_iota(I32, (N_EXPERTS, tm), 0)
    idx_rows, w_rows = [], []
    sel = jnp.zeros((N_EXPERTS, tm), F32)
    for _ in range(TOP_K):
        mk = jnp.max(masked, axis=0, keepdims=True)
        ik = jnp.min(jnp.where(masked == mk, eiota, N_EXPERTS), axis=0, keepdims=True)
        hit = eiota == ik
        w_rows.append(jnp.sum(jnp.where(hit, scores, 0.0), axis=0, keepdims=True))
        idx_rows.append(ik)
        masked = jnp.where(hit, -jnp.inf, masked)
        sel = jnp.where(hit, 1.0, sel)
    w_all = jnp.concatenate(w_rows, axis=0)
    wts_ref[...] = w_all / jnp.sum(w_all, axis=0, keepdims=True) * ROUTED_SCALE
    eidx_ref[...] = jnp.concatenate(idx_rows, axis=0)
    before = cnt_ref[...][:, 0:1] + _dot(sel.astype(BF16), upper_ref[...])
    rank_rows = [jnp.sum(jnp.where(eiota == ik, before, 0.0), axis=0, keepdims=True) for ik in idx_rows]
    rank_ref[...] = jnp.concatenate(rank_rows, axis=0).astype(I32)
    cnt_ref[...] = cnt_ref[...] + jnp.sum(sel, axis=1, keepdims=True)


def _post_mixer(attn2, ssd2, x2, mod3, w_out_a, w_out_s, norm2_g, w_router_t, rb_col,
                wgs, wus, wds, upper, seq_len):
    t, d = x2.shape
    tm = POST_TM
    per_b = seq_len // tm
    row = lambda i: (i, 0)
    const = lambda i: (0, 0)
    modspec = lambda k: pl.BlockSpec((1, 1, d), lambda i: ((i // per_b) * N_MOD + k, 0, 0))
    slot = lambda i: (0, i)
    return pl.pallas_call(
        _post_kernel,
        out_shape=(jax.ShapeDtypeStruct((t, d), F32),
                   jax.ShapeDtypeStruct((t, d // 2), U32),
                   jax.ShapeDtypeStruct((TOP_K, t), I32),
                   jax.ShapeDtypeStruct((TOP_K, t), F32),
                   jax.ShapeDtypeStruct((TOP_K, t), I32),
                   jax.ShapeDtypeStruct((N_EXPERTS, LANES), F32)),
        grid=(t // tm,),
        in_specs=[pl.BlockSpec((tm, ATTN_DIM), row),
                  pl.BlockSpec((tm, SSD_DIM), row),
                  pl.BlockSpec((tm, d), row),
                  modspec(2), modspec(3), modspec(4), modspec(5),
                  pl.BlockSpec((ATTN_DIM, d), const),
                  pl.BlockSpec((SSD_DIM, d), const),
                  pl.BlockSpec((1, d), const),
                  pl.BlockSpec((N_EXPERTS, d), const),
                  pl.BlockSpec((N_EXPERTS, 1), const),
                  pl.BlockSpec((d, EXPERT_DIM), const),
                  pl.BlockSpec((d, EXPERT_DIM), const),
                  pl.BlockSpec((EXPERT_DIM, d), const),
                  pl.BlockSpec((tm, tm), const)],
        out_specs=(pl.BlockSpec((tm, d), row),
                   pl.BlockSpec((tm, d // 2), row),
                   pl.BlockSpec((TOP_K, tm), slot),
                   pl.BlockSpec((TOP_K, tm), slot),
                   pl.BlockSpec((TOP_K, tm), slot),
                   pl.BlockSpec((N_EXPERTS, LANES), const)),
        compiler_params=_cparams(("arbitrary",)),
        name="post_mixer",
    )(attn2, ssd2, x2, mod3, mod3, mod3, mod3, w_out_a, w_out_s, norm2_g.reshape(1, d),
      w_router_t, rb_col, wgs, wus, wds, upper)


def _dest_kernel(eidx_ref, rank_ref, pst_ref, dest_ref):
    tm = eidx_ref.shape[1]
    eiota = lax.broadcasted_iota(I32, (N_EXPERTS, tm), 0)
    rows = []
    for k in range(TOP_K):
        onehot = jnp.where(eiota == eidx_ref[k:k + 1, :], 1.0, 0.0).astype(BF16)
        r = _dot(pst_ref[...], onehot)
        rows.append(r[0:1, :] + r[1:2, :] + r[2:3, :])
    dest_ref[...] = jnp.concatenate(rows, axis=0).astype(I32) + rank_ref[...]


def _dest_rows(eidx, rank, pstart):
    k, t = eidx.shape
    tm = min(DEST_TM, t)
    pieces = jnp.stack([pstart & 0xFF0000, pstart & 0xFF00, pstart & 0xFF], axis=0)
    pst = jnp.zeros((8, N_EXPERTS), F32).at[:3].set(pieces.astype(F32)).astype(BF16)
    slot = lambda i: (0, i)
    return pl.pallas_call(
        _dest_kernel,
        out_shape=jax.ShapeDtypeStruct((k, t), I32),
        grid=(t // tm,),
        in_specs=[pl.BlockSpec((k, tm), slot), pl.BlockSpec((k, tm), slot),
                  pl.BlockSpec((8, N_EXPERTS), lambda i: (0, 0))],
        out_specs=pl.BlockSpec((k, tm), slot),
        compiler_params=_cparams(("arbitrary",)),
        name="dest_rows",
    )(eidx, rank, pst)


def _sc_mesh():
    return plsc.VectorSubcoreMesh(core_axis_name="c", subcore_axis_name="s")


def _sc_worker(n_workers_per_core):
    return lax.axis_index("s") * n_workers_per_core + lax.axis_index("c")


def _dispatch(dest, h2p, n_rows):
    t, w = h2p.shape
    mesh = _sc_mesh()
    n_workers = mesh.num_cores * mesh.num_subcores
    per_w = t // n_workers
    ch = min(SC_CHUNK, per_w)

    def body(dest_hbm, h_hbm, xs_hbm, idx_v, rows_v, sem):
        base_w = _sc_worker(mesh.num_cores) * per_w

        @pl.loop(0, per_w // ch)
        def _(ci):
            base = pl.multiple_of(base_w + ci * ch, ch)
            pltpu.sync_copy(dest_hbm.at[:, pl.ds(base, ch)], idx_v)
            pltpu.sync_copy(h_hbm.at[pl.ds(base, ch)], rows_v)
            copies = [pltpu.async_copy(rows_v, xs_hbm.at[idx_v.at[k]], sem) for k in range(TOP_K)]
            for cp in copies:
                cp.wait()

    return pl.kernel(
        body,
        out_type=jax.ShapeDtypeStruct((n_rows, w), U32),
        mesh=mesh,
        scratch_types=[pltpu.VMEM((TOP_K, ch), I32), pltpu.VMEM((ch, w), U32), pltpu.SemaphoreType.DMA],
        name="dispatch",
    )(dest, h2p)


def _undispatch(dest, ys, t0, t):
    w = ys.shape[1]
    mesh = _sc_mesh()
    n_workers = mesh.num_cores * mesh.num_subcores
    per_w = t // n_workers
    ch = min(SC_CHUNK, per_w)

    def body(dest_hbm, ys_hbm, ytok_hbm, idx_v, rows_v, sem):
        base_w = _sc_worker(mesh.num_cores) * per_w

        @pl.loop(0, per_w // ch)
        def _(ci):
            base = pl.multiple_of(base_w + ci * ch, ch)
            pltpu.sync_copy(dest_hbm.at[:, pl.ds(t0 + base, ch)], idx_v)
            for k in range(TOP_K):
                pltpu.async_copy(ys_hbm.at[idx_v.at[k]], rows_v, sem).wait()
                pltpu.sync_copy(rows_v, ytok_hbm.at[k, pl.ds(base, ch)])

    return pl.kernel(
        body,
        out_type=jax.ShapeDtypeStruct((TOP_K, t, w), U32),
        mesh=mesh,
        scratch_types=[pltpu.VMEM((TOP_K, ch), I32), pltpu.VMEM((ch, w), U32), pltpu.SemaphoreType.DMA],
        name="undispatch",
    )(dest, ys)


def _expert_kernel(be_ref, nu_ref, x_ref, wg_ref, wu_ref, wd_ref, y_ref, act_ref):
    b = pl.program_id(0)
    n_used = nu_ref[0]

    def gate_up():
        x = _unpack_bf16_pair(x_ref[...]).astype(BF16)
        return (_silu(_dot(x, wg_ref[0].astype(BF16))) * _dot(x, wu_ref[0].astype(BF16))).astype(BF16)

    def down():
        y_ref[...] = _pack_bf16_pair(_dot(act_ref[...], wd_ref[0].astype(BF16)))

    @pl.when(b == 0)
    def _():
        act_ref[...] = gate_up()

    @pl.when((b > 0) & (b < n_used))
    def _():
        down()
        act_ref[...] = gate_up()

    @pl.when(b == n_used)
    def _():
        down()


def _experts(block_expert, n_used, xs, wg, wu, wd):
    n_rows, w = xs.shape
    n_blocks = n_rows // EXP_BLK
    d, f = wg.shape[1], wg.shape[2]
    cur = lambda b, be, nu: (jnp.minimum(b, nu[0] - 1), 0)
    prev = lambda b, be, nu: (jnp.clip(b - 1, 0, nu[0] - 1), 0)
    wcur = lambda b, be, nu: (be[jnp.minimum(b, nu[0] - 1)], 0, 0)
    wprev = lambda b, be, nu: (be[jnp.clip(b - 1, 0, nu[0] - 1)], 0, 0)
    grid_spec = pltpu.PrefetchScalarGridSpec(
        num_scalar_prefetch=2,
        grid=(n_blocks + 1,),
        in_specs=[pl.BlockSpec((EXP_BLK, w), cur),
                  pl.BlockSpec((1, d, f), wcur),
                  pl.BlockSpec((1, d, f), wcur),
                  pl.BlockSpec((1, f, d), wprev)],
        out_specs=pl.BlockSpec((EXP_BLK, w), prev),
        scratch_shapes=[pltpu.VMEM((EXP_BLK, f), BF16)],
    )
    return pl.pallas_call(
        _expert_kernel,
        out_shape=jax.ShapeDtypeStruct((n_rows, w), U32),
        grid_spec=grid_spec,
        compiler_params=_cparams(("arbitrary",)),
        name="experts",
    )(block_expert, n_used, xs, wg, wu, wd)


def _combine_kernel(ytok_ref, x1s_ref, w_ref, g2_ref, nf_ref, mf0_ref, mf1_ref, o_ref):
    w = w_ref[...]
    routed = w[:, 0:1] * _unpack_bf16_pair(ytok_ref[0])
    for k in range(1, TOP_K):
        routed = routed + w[:, k:k + 1] * _unpack_bf16_pair(ytok_ref[k])
    xo = x1s_ref[...] + g2_ref[0] * routed
    var = jnp.mean(xo * xo, axis=-1, keepdims=True)
    y = xo * lax.rsqrt(var + EPS) * nf_ref[...]
    o_ref[...] = y * (1.0 + mf1_ref[0]) + mf0_ref[0]


def _combine_alias_kernel(prev_ref, *refs):
    del prev_ref
    _combine_kernel(*refs)


def _combine(ytok, x1s, wts_t, mod3, normf_g, modf3, seq_len, part, out_prev):
    t, d = x1s.shape
    tm = COMB_TM
    steps = ytok.shape[1] // tm
    i0 = part * steps
    per_b = seq_len // tm
    row = lambda i: (i0 + i, 0)
    in_specs = [pl.BlockSpec((TOP_K, tm, d // 2), lambda i: (0, i, 0)),
                pl.BlockSpec((tm, d), row),
                pl.BlockSpec((tm, TOP_K), row),
                pl.BlockSpec((1, 1, d), lambda i: (((i0 + i) // per_b) * N_MOD + 5, 0, 0)),
                pl.BlockSpec((1, d), lambda i: (0, 0)),
                pl.BlockSpec((1, 1, d), lambda i: (((i0 + i) // per_b) * 2 + 0, 0, 0)),
                pl.BlockSpec((1, 1, d), lambda i: (((i0 + i) // per_b) * 2 + 1, 0, 0))]
    args = (ytok, x1s, wts_t, mod3, normf_g.reshape(1, d), modf3, modf3)
    if out_prev is None:
        body, aliases = _combine_kernel, {}
    else:
        body, aliases = _combine_alias_kernel, {0: 0}
        in_specs = [pl.BlockSpec(memory_space=pl.ANY)] + in_specs
        args = (out_prev,) + args
    return pl.pallas_call(
        body,
        out_shape=jax.ShapeDtypeStruct((t, d), F32),
        grid=(steps,),
        in_specs=in_specs,
        out_specs=pl.BlockSpec((tm, d), row),
        input_output_aliases=aliases,
        compiler_params=_cparams(("arbitrary",)),
        name="combine",
    )(*args)


def _lane_row(vec, lane0):
    return jnp.zeros((1, SMALL_W), F32).at[0, lane0:lane0 + vec.shape[0]].set(vec.astype(F32))


def _layer(x2, mod3, bsz, seq_len, norm1_g, w_in, fg_bias, conv_w, conv_b, dt_bias, a_log, d_skip,
           attn_norm_g, ssd_norm_g, w_out, norm2_g, w_router, router_bias,
           w_gate_e, w_up_e, w_down_e, w_gate_s, w_up_s, w_down_s):
    t, d = x2.shape
    o_q, o_k, o_v, o_fg = 0, ATTN_DIM, 2 * ATTN_DIM, 3 * ATTN_DIM
    o_z = o_fg + ATTN_HEADS
    o_xbc = o_z + SSD_DIM
    o_dt = o_xbc + SSD_CONV_DIM
    small = jnp.zeros((d, SMALL_W), F32)
    small = small.at[:, FG_LANE0:FG_LANE0 + ATTN_HEADS].set(w_in[:, o_fg:o_z])
    small = small.at[:, DT_LANE0:DT_LANE0 + SSD_HEADS].set(w_in[:, o_dt:o_dt + SSD_HEADS])
    w_cat = jnp.concatenate([w_in[:, o_q:o_fg], w_in[:, o_z:o_dt], small], axis=1).astype(BF16)

    tri_in = jnp.tril(jnp.ones((IN_TM, IN_TM), F32)).astype(BF16)
    qa, ka, va, z2, xbc2, sm2 = _in_proj(x2, mod3, norm1_g, w_cat, _lane_row(fg_bias, FG_LANE0), tri_in,
                                         bsz, seq_len)
    shp = lambda a: a.reshape(bsz, seq_len, a.shape[-1])
    attn3 = _fox_attn(qa, ka, va, attn_norm_g.reshape(1, ATTN_DIM).astype(F32))

    tri_chunk = jnp.tril(jnp.ones((SSD_CHUNK, SSD_CHUNK), F32)).astype(BF16)
    head_of_lane = jnp.arange(SSD_DIM, dtype=I32) // SSD_HEAD_DIM
    expand = (jnp.arange(SMALL_W, dtype=I32)[:, None] == head_of_lane[None, :] + DT_LANE0).astype(BF16)
    dskip_row = jnp.repeat(d_skip.astype(F32), SSD_HEAD_DIM).reshape(1, SSD_DIM)
    ssd3 = _ssd(shp(xbc2), shp(z2), shp(sm2), conv_w.astype(F32), conv_b.reshape(1, -1).astype(F32),
                _lane_row(dt_bias, DT_LANE0), _lane_row(a_log, DT_LANE0), dskip_row,
                ssd_norm_g.reshape(1, SSD_DIM).astype(F32), tri_chunk, expand)

    upper = jnp.triu(jnp.ones((POST_TM, POST_TM), F32), 1).astype(BF16)
    x1s, h2p, eidx, wts, rank, cnt = _post_mixer(
        attn3.reshape(t, ATTN_DIM), ssd3.reshape(t, SSD_DIM), x2, mod3,
        w_out[:ATTN_DIM].astype(BF16), w_out[ATTN_DIM:].astype(BF16), norm2_g,
        w_router.T.astype(BF16), router_bias.reshape(N_EXPERTS, 1).astype(F32),
        w_gate_s.astype(BF16), w_up_s.astype(BF16), w_down_s.astype(BF16), upper, seq_len)

    counts = cnt[:, 0].astype(I32)
    padded = (counts + EXP_BLK - 1) // EXP_BLK * EXP_BLK
    pend = jnp.cumsum(padded)
    pstart = pend - padded
    n_blocks = (t * TOP_K) // EXP_BLK + N_EXPERTS
    block_start = jnp.arange(n_blocks, dtype=I32) * EXP_BLK
    block_expert = jnp.minimum(jnp.sum((pend[None, :] <= block_start[:, None]).astype(I32), axis=1),
                               N_EXPERTS - 1)
    n_used = (pend[-1] // EXP_BLK).astype(I32).reshape(1)
    dest = _dest_rows(eidx, rank, pstart)

    xs = _dispatch(dest, h2p, n_blocks * EXP_BLK)
    ys = _experts(block_expert, n_used, xs, w_gate_e, w_up_e, w_down_e)
    return dest, ys, x1s, wts


def kernel(x, c, norm1_g, w_ada, b_ada, w_in, fg_bias, conv_w, conv_b, dt_bias, a_log, d_skip, attn_norm_g,
           ssd_norm_g, w_out, norm2_g, w_router, router_bias, w_gate_e, w_up_e, w_down_e, w_gate_s, w_up_s,
           w_down_s, normf_g, w_ada_f, b_ada_f):
    bsz, seq_len, d = x.shape
    assert w_ada.shape[0] == 1, "single-layer kernel"
    t = bsz * seq_len
    mod3 = _modulation(c, w_ada[0], b_ada[0]).reshape(bsz * N_MOD, 1, d)
    modf3 = _modulation(c, w_ada_f, b_ada_f).reshape(bsz * 2, 1, d)
    x2 = x.reshape(t, d)
    dest, ys, x1s, wts = _layer(
        x2, mod3, bsz, seq_len, norm1_g[0], w_in[0], fg_bias[0], conv_w[0], conv_b[0], dt_bias[0], a_log[0],
        d_skip[0], attn_norm_g[0], ssd_norm_g[0], w_out[0], norm2_g[0], w_router[0], router_bias[0],
        w_gate_e[0], w_up_e[0], w_down_e[0], w_gate_s[0], w_up_s[0], w_down_s[0])
    wts_t = wts.T
    tp = t // COMB_PARTS
    out = None
    for p in range(COMB_PARTS):
        ytok = _undispatch(dest, ys, p * tp, tp)
        out = _combine(ytok, x1s, wts_t, mod3, normf_g, modf3, seq_len, p, out)
    return out.reshape(bsz, seq_len, d)
```

```python
import functools

import jax
import jax.numpy as jnp
from jax import lax
from jax.experimental import pallas as pl
from jax.experimental.pallas import tpu as pltpu
from jax.experimental.pallas import tpu_sc as plsc

F32 = jnp.float32
BF16 = jnp.bfloat16
I32 = jnp.int32
U32 = jnp.uint32

EPS = 1e-6
D_MODEL = 1024
N_MOD = 6

ATTN_HEADS = 8
ATTN_HEAD_DIM = 64
ATTN_DIM = ATTN_HEADS * ATTN_HEAD_DIM

SSD_HEADS = 8
SSD_HEAD_DIM = 64
SSD_DIM = SSD_HEADS * SSD_HEAD_DIM
SSD_GROUPS = 2
SSD_STATE = 128
SSD_CONV = 4
SSD_CHUNK = 128
SSD_CONV_DIM = SSD_DIM + 2 * SSD_GROUPS * SSD_STATE
SSD_GROUP_DIM = SSD_DIM // SSD_GROUPS

N_EXPERTS = 256
TOP_K = 8
N_ROUTE_GROUPS = 8
TOPK_ROUTE_GROUPS = 4
GROUP_SIZE = N_EXPERTS // N_ROUTE_GROUPS
EXPERT_DIM = 256
ROUTED_SCALE = 2.5

LANES = 128
SMALL_W = LANES
FG_LANE0 = 0
DT_LANE0 = 8
AUG_W = LANES

IN_TM = 1024
IN_SUB = 512
ATT_TQ = 512
ATT_TK = 512
ATT_HEADS_PER_STEP = 8
POST_TM = 512
SC_CHUNK = 128
EXP_BLK = 1024
COMB_TM = 256
COMB_PARTS = 8
VMEM_LIMIT = 56 * 1024 * 1024
NEG_BIG = -1e30
LOG2E = 1.4426950408889634
DEST_TM = 2048


def _split3(x):
    hi = x.astype(BF16)
    r1 = x - hi.astype(F32)
    mid = r1.astype(BF16)
    lo = (r1 - mid.astype(F32)).astype(BF16)
    return hi, mid, lo


def _dot(a, b):
    return jnp.dot(a, b, preferred_element_type=F32)


def _dot_nt(a, b):
    return lax.dot_general(a, b, (((1,), (1,)), ((), ())), preferred_element_type=F32)


def _dot_exact_lhs01(lhs_bf16, x, pieces=3):
    parts = _split3(x)[:pieces]
    out = _dot(lhs_bf16, parts[0])
    for p in parts[1:]:
        out = out + _dot(lhs_bf16, p)
    return out


def _dot_exact_rhs01(x, rhs_bf16, pieces=2):
    parts = _split3(x)[:pieces]
    out = _dot(parts[0], rhs_bf16)
    for p in parts[1:]:
        out = out + _dot(p, rhs_bf16)
    return out


def _sigmoid(x):
    return 1.0 / (1.0 + jnp.exp(-x))


def _silu(x):
    return x * _sigmoid(x)


def _softplus(x):
    return jnp.maximum(x, 0.0) + jnp.log(1.0 + jnp.exp(-jnp.abs(x)))


def _log_sigmoid(x):
    return jnp.minimum(x, 0.0) - jnp.log(1.0 + jnp.exp(-jnp.abs(x)))


def _pack_bf16_pair(x):
    n = x.shape[1] // 2
    lo = pltpu.bitcast(x[:, :n].astype(BF16).astype(F32), U32)
    hi = pltpu.bitcast(x[:, n:].astype(BF16).astype(F32), U32)
    return (hi & jnp.uint32(0xFFFF0000)) | (lo >> 16)


def _unpack_bf16_pair(w):
    lo = pltpu.bitcast(w << 16, F32)
    hi = pltpu.bitcast(w & jnp.uint32(0xFFFF0000), F32)
    return jnp.concatenate([lo, hi], axis=1)


def _cparams(sem):
    return pltpu.CompilerParams(dimension_semantics=sem, vmem_limit_bytes=VMEM_LIMIT)


def _mod_kernel(c_ref, w_ref, b_ref, o_ref):
    c = c_ref[...]
    o_ref[...] = jnp.dot(_silu(c), w_ref[...], preferred_element_type=F32,
                         precision=lax.Precision.HIGHEST) + b_ref[...]


def _modulation(c, w, b):
    bsz, d = c.shape
    n = w.shape[1]
    tn = 1024
    return pl.pallas_call(
        _mod_kernel,
        out_shape=jax.ShapeDtypeStruct((bsz, n), F32),
        grid=(n // tn,),
        in_specs=[pl.BlockSpec((bsz, d), lambda j: (0, 0)),
                  pl.BlockSpec((d, tn), lambda j: (0, j)),
                  pl.BlockSpec((1, tn), lambda j: (0, j))],
        out_specs=pl.BlockSpec((bsz, tn), lambda j: (0, j)),
        compiler_params=_cparams(("arbitrary",)),
        name="modulation",
    )(c, w, b.reshape(1, n))


_COL_Q, _COL_K, _COL_V, _COL_Z, _COL_XBC, _COL_SM, _COL_END = 0, 512, 1024, 1536, 2048, 3072, 3200


def _inproj_kernel(per_b, x_ref, g_ref, sc_ref, sh_ref, w_ref, fgb_ref, tri_ref,
                   qa_ref, ka_ref, va_ref, z_ref, xbc_ref, sm_ref, carry_ref):
    @pl.when(pl.program_id(0) % per_b == 0)
    def _():
        carry_ref[...] = jnp.zeros_like(carry_ref)

    carry = carry_ref[...]
    for r0 in range(0, x_ref.shape[0], IN_SUB):
        carry = _inproj_rows(r0, carry, x_ref, g_ref, sc_ref, sh_ref, w_ref, fgb_ref, tri_ref,
                             qa_ref, ka_ref, va_ref, z_ref, xbc_ref, sm_ref)
    carry_ref[...] = carry


def _inproj_rows(r0, carry, x_ref, g_ref, sc_ref, sh_ref, w_ref, fgb_ref, tri_ref,
                 qa_ref, ka_ref, va_ref, z_ref, xbc_ref, sm_ref):
    tm = IN_SUB
    rows = slice(r0, r0 + tm)
    x = x_ref[rows, :]
    var = jnp.mean(x * x, axis=-1, keepdims=True)
    h = x * lax.rsqrt(var + EPS) * g_ref[...]
    h = h * (1.0 + sc_ref[0]) + sh_ref[0]
    hb = h.astype(BF16)
    sm = _dot(hb, w_ref[:, _COL_SM:_COL_END])
    sm_ref[rows, :] = sm
    log_f = _log_sigmoid(sm + fgb_ref[...]) * LOG2E
    cum = _dot_exact_lhs01(tri_ref[...], log_f) + carry
    hi, mid, lo = (p.astype(F32) for p in _split3(cum))
    lane = lax.broadcasted_iota(I32, (tm, AUG_W), 1)
    left = lane < ATTN_HEAD_DIM

    def head_rows(out_ref, col0, scale, book_of_head):
        f = _dot(hb, w_ref[:, col0:col0 + ATTN_DIM])
        if scale is not None:
            f = f * scale
        for pr in range(ATTN_HEADS // 2):
            pair = f[:, pr * AUG_W:(pr + 1) * AUG_W]
            out_ref[0, 2 * pr, rows, :] = jnp.where(left, pair, book_of_head(2 * pr, ATTN_HEAD_DIM)).astype(BF16)
            out_ref[0, 2 * pr + 1, rows, :] = jnp.where(left, book_of_head(2 * pr + 1, 0), pair).astype(BF16)

    def cols(hd):
        c0 = FG_LANE0 + hd
        return hi[:, c0:c0 + 1], mid[:, c0:c0 + 1], lo[:, c0:c0 + 1]

    def book_q(hd, base):
        chi, cmid, clo = cols(hd)
        rel = lane - base
        return jnp.where(rel == 0, chi, jnp.where(rel == 1, cmid, jnp.where(rel == 2, clo,
                         jnp.where((rel >= 3) & (rel < 6), 1.0, 0.0))))

    def book_k(hd, base):
        chi, cmid, clo = cols(hd)
        rel = lane - base
        return jnp.where(rel == 3, -chi, jnp.where(rel == 4, -cmid, jnp.where(rel == 5, -clo,
                         jnp.where((rel >= 0) & (rel < 3), 1.0, 0.0))))

    def book_v(hd, base):
        return jnp.where(lane == base, 1.0, 0.0)

    head_rows(qa_ref, _COL_Q, ATTN_HEAD_DIM ** -0.5 * LOG2E, book_q)
    head_rows(ka_ref, _COL_K, None, book_k)
    head_rows(va_ref, _COL_V, None, book_v)
    z_ref[rows, :] = _dot(hb, w_ref[:, _COL_Z:_COL_XBC]).astype(BF16)
    xbc_ref[rows, :512] = _dot(hb, w_ref[:, _COL_XBC:_COL_XBC + 512]).astype(BF16)
    xbc_ref[rows, 512:] = _dot(hb, w_ref[:, _COL_XBC + 512:_COL_SM]).astype(BF16)
    return cum[tm - 1:tm, :]


def _in_proj(x2, mod3, norm_g, w_cat, fgb_row, tri, bsz, seq_len):
    t, d = x2.shape
    tm = IN_TM
    per_b = seq_len // tm
    row = lambda i: (i, 0)
    const = lambda i: (0, 0)
    aug = jax.ShapeDtypeStruct((bsz, ATTN_HEADS, seq_len, AUG_W), BF16)
    aug_spec = pl.BlockSpec((1, ATTN_HEADS, tm, AUG_W), lambda i: (i // per_b, 0, i % per_b, 0))
    return pl.pallas_call(
        functools.partial(_inproj_kernel, per_b),
        out_shape=(aug, aug, aug,
                   jax.ShapeDtypeStruct((t, SSD_DIM), BF16),
                   jax.ShapeDtypeStruct((t, SSD_CONV_DIM), BF16),
                   jax.ShapeDtypeStruct((t, SMALL_W), F32)),
        grid=(t // tm,),
        in_specs=[pl.BlockSpec((tm, d), row),
                  pl.BlockSpec((1, d), const),
                  pl.BlockSpec((1, 1, d), lambda i: ((i // per_b) * N_MOD + 1, 0, 0)),
                  pl.BlockSpec((1, 1, d), lambda i: ((i // per_b) * N_MOD + 0, 0, 0)),
                  pl.BlockSpec((d, _COL_END), const),
                  pl.BlockSpec((1, SMALL_W), const),
                  pl.BlockSpec((IN_SUB, IN_SUB), const)],
        out_specs=(aug_spec, aug_spec, aug_spec,
                   pl.BlockSpec((tm, SSD_DIM), row),
                   pl.BlockSpec((tm, SSD_CONV_DIM), row),
                   pl.BlockSpec((tm, SMALL_W), row)),
        scratch_shapes=[pltpu.VMEM((1, SMALL_W), F32)],
        compiler_params=_cparams(("arbitrary",)),
        name="in_proj",
    )(x2, norm_g.reshape(1, d), mod3, mod3, w_cat, fgb_row, tri)


def _attn_kernel(qa_ref, ka_ref, va_ref, g_ref, o_ref):
    i = pl.program_id(2)
    tq, tk = ATT_TQ, ATT_TK
    nh = ATT_HEADS_PER_STEP
    diff = lax.broadcasted_iota(I32, (tq, tk), 0) - lax.broadcasted_iota(I32, (tq, tk), 1)

    def scores(hh, j):
        off = pl.multiple_of(j * tk, tk)
        return _dot_nt(qa_ref[0, hh], ka_ref[0, hh, pl.ds(off, tk), :])

    def update(hh, state, s, j):
        m_old, acc = state
        off = pl.multiple_of(j * tk, tk)
        s = jnp.where(diff >= (j - i) * tk, s, NEG_BIG)
        m_new = jnp.maximum(m_old, jnp.max(s, axis=-1, keepdims=True))
        p = jnp.exp2(s - m_new).astype(BF16)
        acc = jnp.exp2(m_old - m_new) * acc + _dot(p, va_ref[0, hh, pl.ds(off, tk), :])
        return m_new, acc

    def body(j, states):
        ss = [scores(hh, j) for hh in range(nh)]
        return tuple(update(hh, states[hh], ss[hh], j) for hh in range(nh))

    init = tuple((jnp.full((tq, 1), NEG_BIG, F32), jnp.zeros((tq, LANES), F32)) for _ in range(nh))
    states = lax.fori_loop(0, i + 1, body, init)
    lane = lax.broadcasted_iota(I32, (tq, LANES), 1)
    first = lane < ATTN_HEAD_DIM
    for pr in range(nh // 2):
        acc_e, acc_o = states[2 * pr][1], states[2 * pr + 1][1]
        sum_e = jnp.sum(jnp.where(lane == ATTN_HEAD_DIM, acc_e, 0.0), axis=-1, keepdims=True)
        sum_o = jnp.sum(jnp.where(lane == 0, acc_o, 0.0), axis=-1, keepdims=True)
        o = jnp.where(first, acc_e / sum_e, acc_o / sum_o)
        sq = o * o
        s_all = jnp.sum(sq, axis=-1, keepdims=True)
        s0 = jnp.sum(jnp.where(first, sq, 0.0), axis=-1, keepdims=True)
        ms = jnp.where(first, s0, s_all - s0) * (1.0 / ATTN_HEAD_DIM)
        gsl = slice(pr * LANES, (pr + 1) * LANES)
        o_ref[0, :, gsl] = (o * lax.rsqrt(ms + EPS) * g_ref[:, gsl]).astype(BF16)


def _fox_attn(qa, ka, va, g_row):
    bsz, _, seq_len, _ = qa.shape
    tq = ATT_TQ
    nh = ATT_HEADS_PER_STEP
    wo = nh // 2 * LANES
    kv_spec = pl.BlockSpec((1, nh, seq_len, AUG_W), lambda b, p, i: (b, p, 0, 0))
    return pl.pallas_call(
        _attn_kernel,
        out_shape=jax.ShapeDtypeStruct((bsz, seq_len, ATTN_DIM), BF16),
        grid=(bsz, ATTN_HEADS // nh, seq_len // tq),
        in_specs=[pl.BlockSpec((1, nh, tq, AUG_W), lambda b, p, i: (b, p, i, 0)),
                  kv_spec, kv_spec,
                  pl.BlockSpec((1, wo), lambda b, p, i: (0, p))],
        out_specs=pl.BlockSpec((1, tq, wo), lambda b, p, i: (b, i, p)),
        compiler_params=_cparams(("arbitrary", "arbitrary", "arbitrary")),
        name="fox_attn",
    )(qa, ka, va, g_row)


def _ssd_kernel(xbc_ref, z_ref, sm_ref, cw_ref, cb_ref, dtb_ref, alog_ref, dsk_ref, g_ref, tri_ref, exp_ref,
                o_ref, ext_ref, state_ref):
    c = pl.program_id(1)
    q = SSD_CHUNK

    @pl.when(c == 0)
    def _():
        ext_ref[0:8, :] = jnp.zeros((8, SSD_CONV_DIM), F32)
        state_ref[...] = jnp.zeros_like(state_ref)

    ext_ref[8:8 + q, :] = xbc_ref[0].astype(F32)
    conv = cb_ref[...] + cw_ref[0:1, :] * ext_ref[5:5 + q, :]
    for j in range(1, SSD_CONV):
        conv = conv + cw_ref[j:j + 1, :] * ext_ref[5 + j:5 + j + q, :]
    ext_ref[0:8, :] = ext_ref[q:q + 8, :]
    xc = _silu(conv)
    xs = xc[:, :SSD_DIM]

    dt = _softplus(sm_ref[0] + dtb_ref[...])
    a_dt = -jnp.exp(alog_ref[...]) * dt
    a_cs = _dot_exact_lhs01(tri_ref[...], a_dt)
    a_last = a_cs[q - 1:q, :]
    e_cs = jnp.exp(a_cs)
    dec = jnp.exp(a_last - a_cs)
    a_cs_t = a_cs.T
    expand = exp_ref[...]
    dt_x = _dot_exact_rhs01(dt, expand)
    e_x = _dot_exact_rhs01(e_cs, expand)
    dec_x = _dot_exact_rhs01(dec, expand)
    x_dt = xs * dt_x
    x_dec = (x_dt * dec_x).astype(BF16)
    x_dt_b = x_dt.astype(BF16)

    row = lax.broadcasted_iota(I32, (q, q), 0)
    col = lax.broadcasted_iota(I32, (q, q), 1)
    lower = row >= col
    lane = lax.broadcasted_iota(I32, (q, LANES), 1)
    first = lane < SSD_HEAD_DIM
    y_parts = []
    for g in range(SSD_GROUPS):
        b_g = xc[:, SSD_DIM + g * SSD_STATE:SSD_DIM + (g + 1) * SSD_STATE]
        c_g = xc[:, SSD_DIM + (SSD_GROUPS + g) * SSD_STATE:SSD_DIM + (SSD_GROUPS + g + 1) * SSD_STATE]
        c_gb = c_g.astype(BF16)
        cb = _dot_nt(c_gb, b_g.astype(BF16))
        gs = slice(g * SSD_GROUP_DIM, (g + 1) * SSD_GROUP_DIM)
        st_prev = state_ref[g]
        y_off = _dot(c_gb, st_prev.astype(BF16)) * e_x[:, gs]
        s_new = _dot(b_g.T.astype(BF16), x_dec[:, gs])
        state_ref[g] = st_prev * e_x[q - 1:q, gs] + s_new
        for pr in range(2):
            pair = []
            for hh in range(2):
                h = g * 4 + pr * 2 + hh
                a_col = a_cs[:, DT_LANE0 + h:DT_LANE0 + h + 1]
                a_row = a_cs_t[DT_LANE0 + h:DT_LANE0 + h + 1, :]
                lmat = jnp.where(lower, jnp.exp(jnp.minimum(a_col - a_row, 0.0)), 0.0)
                m_h = (cb * lmat).astype(BF16)
                ps = slice((g * 2 + pr) * LANES, (g * 2 + pr + 1) * LANES)
                pair.append(_dot(m_h, x_dt_b[:, ps]))
            y_parts.append(jnp.where(first, pair[0], pair[1]) + y_off[:, pr * LANES:(pr + 1) * LANES])
    y = jnp.concatenate(y_parts, axis=1) + dsk_ref[...] * xs
    y = y * _silu(z_ref[0].astype(F32))
    outs = []
    for g in range(SSD_GROUPS):
        yg = y[:, g * SSD_GROUP_DIM:(g + 1) * SSD_GROUP_DIM]
        ms = jnp.mean(yg * yg, axis=-1, keepdims=True)
        outs.append(yg * lax.rsqrt(ms + EPS))
    o_ref[0] = (jnp.concatenate(outs, axis=1) * g_ref[...]).astype(BF16)


def _ssd(xbc3, z3, sm3, conv_w, conv_b, dtb_row, alog_row, dskip_row, g_row, tri, expand):
    bsz, seq_len, _ = xbc3.shape
    q = SSD_CHUNK
    blk = lambda b, c: (b, c, 0)
    const = lambda b, c: (0, 0)
    return pl.pallas_call(
        _ssd_kernel,
        out_shape=jax.ShapeDtypeStruct((bsz, seq_len, SSD_DIM), BF16),
        grid=(bsz, seq_len // q),
        in_specs=[pl.BlockSpec((1, q, SSD_CONV_DIM), blk),
                  pl.BlockSpec((1, q, SSD_DIM), blk),
                  pl.BlockSpec((1, q, SMALL_W), blk),
                  pl.BlockSpec((SSD_CONV, SSD_CONV_DIM), const),
                  pl.BlockSpec((1, SSD_CONV_DIM), const),
                  pl.BlockSpec((1, SMALL_W), const),
                  pl.BlockSpec((1, SMALL_W), const),
                  pl.BlockSpec((1, SSD_DIM), const),
                  pl.BlockSpec((1, SSD_DIM), const),
                  pl.BlockSpec((q, q), const),
                  pl.BlockSpec((SMALL_W, SSD_DIM), const)],
        out_specs=pl.BlockSpec((1, q, SSD_DIM), blk),
        scratch_shapes=[pltpu.VMEM((q + 8, SSD_CONV_DIM), F32),
                        pltpu.VMEM((SSD_GROUPS, SSD_STATE, SSD_GROUP_DIM), F32)],
        compiler_params=_cparams(("arbitrary", "arbitrary")),
        name="ssd",
    )(xbc3, z3, sm3, conv_w, conv_b, dtb_row, alog_row, dskip_row, g_row, tri, expand)


def _post_kernel(attn_ref, ssd_ref, x_ref, g1_ref, sh2_ref, sc2_ref, g2_ref, woa_ref, wos_ref, n2_ref,
                 wrt_ref, rb_ref, wgs_ref, wus_ref, wds_ref, upper_ref,
                 x1s_ref, h2_ref, eidx_ref, wts_ref, rank_ref, cnt_ref):
    i = pl.program_id(0)
    tm = x_ref.shape[0]

    @pl.when(i == 0)
    def _():
        cnt_ref[...] = jnp.zeros_like(cnt_ref)

    mixed = _dot(attn_ref[...], woa_ref[...]) + _dot(ssd_ref[...], wos_ref[...])
    x1 = x_ref[...] + g1_ref[0] * mixed
    var = jnp.mean(x1 * x1, axis=-1, keepdims=True)
    h2 = x1 * lax.rsqrt(var + EPS) * n2_ref[...]
    h2 = h2 * (1.0 + sc2_ref[0]) + sh2_ref[0]
    hb = h2.astype(BF16)
    h2_ref[...] = _pack_bf16_pair(h2)
    act = _silu(_dot(hb, wgs_ref[...])) * _dot(hb, wus_ref[...])
    shared = _dot(act.astype(BF16), wds_ref[...])
    x1s_ref[...] = x1 + g2_ref[0] * shared

    scores = _sigmoid(_dot_nt(wrt_ref[...], hb))
    biased = scores + rb_ref[...]
    grp = biased.reshape(N_ROUTE_GROUPS, GROUP_SIZE, tm)
    gi = lax.broadcasted_iota(I32, grp.shape, 1)
    m1 = jnp.max(grp, axis=1, keepdims=True)
    i1 = jnp.min(jnp.where(grp == m1, gi, GROUP_SIZE), axis=1, keepdims=True)
    m2 = jnp.max(jnp.where(gi == i1, -jnp.inf, grp), axis=1, keepdims=True)
    gsc = (m1 + m2).reshape(N_ROUTE_GROUPS, tm)
    gidx = lax.broadcasted_iota(I32, gsc.shape, 0)
    beaten = jnp.zeros(gsc.shape, I32)
    for o in range(N_ROUTE_GROUPS):
        other = gsc[o:o + 1, :]
        beats = (other > gsc) | ((other == gsc) & (gidx > o))
        beaten = beaten + beats.astype(I32)
    gmask = (beaten < TOPK_ROUTE_GROUPS).astype(F32)
    emask = jnp.broadcast_to(gmask.reshape(N_ROUTE_GROUPS, 1, tm), grp.shape).reshape(N_EXPERTS, tm)
    masked = jnp.where(emask > 0.5, biased, -jnp.inf)
    eiota = lax.broadcasted_iota(I32, (N_EXPERTS, tm), 0)
    idx_rows, w_rows = [], []
    sel = jnp.zeros((N_EXPERTS, tm), F32)
    for _ in range(TOP_K):
        mk = jnp.max(masked, axis=0, keepdims=True)
        ik = jnp.min(jnp.where(masked == mk, eiota, N_EXPERTS), axis=0, keepdims=True)
        hit = eiota == ik
        w_rows.append(jnp.sum(jnp.where(hit, scores, 0.0), axis=0, keepdims=True))
        idx_rows.append(ik)
        masked = jnp.where(hit, -jnp.inf, masked)
        sel = jnp.where(hit, 1.0, sel)
    w_all = jnp.concatenate(w_rows, axis=0)
    wts_ref[...] = w_all / jnp.sum(w_all, axis=0, keepdims=True) * ROUTED_SCALE
    eidx_ref[...] = jnp.concatenate(idx_rows, axis=0)
    before = cnt_ref[...][:, 0:1] + _dot(sel.astype(BF16), upper_ref[...])
    rank_rows = [jnp.sum(jnp.where(eiota == ik, before, 0.0), axis=0, keepdims=True) for ik in idx_rows]
    rank_ref[...] = jnp.concatenate(rank_rows, axis=0).astype(I32)
    cnt_ref[...] = cnt_ref[...] + jnp.sum(sel, axis=1, keepdims=True)


def _post_mixer(attn2, ssd2, x2, mod3, w_out_a, w_out_s, norm2_g, w_router_t, rb_col,
                wgs, wus, wds, upper, seq_len):
    t, d = x2.shape
    tm = POST_TM
    per_b = seq_len // tm
    row = lambda i: (i, 0)
    const = lambda i: (0, 0)
    modspec = lambda k: pl.BlockSpec((1, 1, d), lambda i: ((i // per_b) * N_MOD + k, 0, 0))
    slot = lambda i: (0, i)
    return pl.pallas_call(
        _post_kernel,
        out_shape=(jax.ShapeDtypeStruct((t, d), F32),
                   jax.ShapeDtypeStruct((t, d // 2), U32),
                   jax.ShapeDtypeStruct((TOP_K, t), I32),
                   jax.ShapeDtypeStruct((TOP_K, t), F32),
                   jax.ShapeDtypeStruct((TOP_K, t), I32),
                   jax.ShapeDtypeStruct((N_EXPERTS, LANES), F32)),
        grid=(t // tm,),
        in_specs=[pl.BlockSpec((tm, ATTN_DIM), row),
                  pl.BlockSpec((tm, SSD_DIM), row),
                  pl.BlockSpec((tm, d), row),
                  modspec(2), modspec(3), modspec(4), modspec(5),
                  pl.BlockSpec((ATTN_DIM, d), const),
                  pl.BlockSpec((SSD_DIM, d), const),
                  pl.BlockSpec((1, d), const),
                  pl.BlockSpec((N_EXPERTS, d), const),
                  pl.BlockSpec((N_EXPERTS, 1), const),
                  pl.BlockSpec((d, EXPERT_DIM), const),
                  pl.BlockSpec((d, EXPERT_DIM), const),
                  pl.BlockSpec((EXPERT_DIM, d), const),
                  pl.BlockSpec((tm, tm), const)],
        out_specs=(pl.BlockSpec((tm, d), row),
                   pl.BlockSpec((tm, d // 2), row),
                   pl.BlockSpec((TOP_K, tm), slot),
                   pl.BlockSpec((TOP_K, tm), slot),
                   pl.BlockSpec((TOP_K, tm), slot),
                   pl.BlockSpec((N_EXPERTS, LANES), const)),
        compiler_params=_cparams(("arbitrary",)),
        name="post_mixer",
    )(attn2, ssd2, x2, mod3, mod3, mod3, mod3, w_out_a, w_out_s, norm2_g.reshape(1, d),
      w_router_t, rb_col, wgs, wus, wds, upper)


def _dest_kernel(eidx_ref, rank_ref, pst_ref, dest_ref):
    tm = eidx_ref.shape[1]
    eiota = lax.broadcasted_iota(I32, (N_EXPERTS, tm), 0)
    rows = []
    for k in range(TOP_K):
        onehot = jnp.where(eiota == eidx_ref[k:k + 1, :], 1.0, 0.0).astype(BF16)
        r = _dot(pst_ref[...], onehot)
        rows.append(r[0:1, :] + r[1:2, :] + r[2:3, :])
    dest_ref[...] = jnp.concatenate(rows, axis=0).astype(I32) + rank_ref[...]


def _dest_rows(eidx, rank, pstart):
    k, t = eidx.shape
    tm = min(DEST_TM, t)
    pieces = jnp.stack([pstart & 0xFF0000, pstart & 0xFF00, pstart & 0xFF], axis=0)
    pst = jnp.zeros((8, N_EXPERTS), F32).at[:3].set(pieces.astype(F32)).astype(BF16)
    slot = lambda i: (0, i)
    return pl.pallas_call(
        _dest_kernel,
        out_shape=jax.ShapeDtypeStruct((k, t), I32),
        grid=(t // tm,),
        in_specs=[pl.BlockSpec((k, tm), slot), pl.BlockSpec((k, tm), slot),
                  pl.BlockSpec((8, N_EXPERTS), lambda i: (0, 0))],
        out_specs=pl.BlockSpec((k, tm), slot),
        compiler_params=_cparams(("arbitrary",)),
        name="dest_rows",
    )(eidx, rank, pst)


def _sc_mesh():
    return plsc.VectorSubcoreMesh(core_axis_name="c", subcore_axis_name="s")


def _sc_worker(n_workers_per_core):
    return lax.axis_index("s") * n_workers_per_core + lax.axis_index("c")


def _dispatch(dest, h2p, n_rows):
    t, w = h2p.shape
    mesh = _sc_mesh()
    n_workers = mesh.num_cores * mesh.num_subcores
    per_w = t // n_workers
    ch = min(SC_CHUNK, per_w)

    def body(dest_hbm, h_hbm, xs_hbm, idx_v, rows_v, sem):
        base_w = _sc_worker(mesh.num_cores) * per_w

        @pl.loop(0, per_w // ch)
        def _(ci):
            base = pl.multiple_of(base_w + ci * ch, ch)
            pltpu.sync_copy(dest_hbm.at[:, pl.ds(base, ch)], idx_v)
            pltpu.sync_copy(h_hbm.at[pl.ds(base, ch)], rows_v)
            copies = [pltpu.async_copy(rows_v, xs_hbm.at[idx_v.at[k]], sem) for k in range(TOP_K)]
            for cp in copies:
                cp.wait()

    return pl.kernel(
        body,
        out_type=jax.ShapeDtypeStruct((n_rows, w), U32),
        mesh=mesh,
        scratch_types=[pltpu.VMEM((TOP_K, ch), I32), pltpu.VMEM((ch, w), U32), pltpu.SemaphoreType.DMA],
        name="dispatch",
    )(dest, h2p)


def _undispatch(dest, ys, t0, t):
    w = ys.shape[1]
    mesh = _sc_mesh()
    n_workers = mesh.num_cores * mesh.num_subcores
    per_w = t // n_workers
    ch = min(SC_CHUNK, per_w)

    def body(dest_hbm, ys_hbm, ytok_hbm, idx_v, rows_v, sem):
        base_w = _sc_worker(mesh.num_cores) * per_w

        @pl.loop(0, per_w // ch)
        def _(ci):
            base = pl.multiple_of(base_w + ci * ch, ch)
            pltpu.sync_copy(dest_hbm.at[:, pl.ds(t0 + base, ch)], idx_v)
            for k in range(TOP_K):
                pltpu.async_copy(ys_hbm.at[idx_v.at[k]], rows_v, sem).wait()
                pltpu.sync_copy(rows_v, ytok_hbm.at[k, pl.ds(base, ch)])

    return pl.kernel(
        body,
        out_type=jax.ShapeDtypeStruct((TOP_K, t, w), U32),
        mesh=mesh,
        scratch_types=[pltpu.VMEM((TOP_K, ch), I32), pltpu.VMEM((ch, w), U32), pltpu.SemaphoreType.DMA],
        name="undispatch",
    )(dest, ys)


def _expert_kernel(be_ref, nu_ref, x_ref, wg_ref, wu_ref, wd_ref, y_ref, act_ref):
    b = pl.program_id(0)
    n_used = nu_ref[0]

    def gate_up():
        x = _unpack_bf16_pair(x_ref[...]).astype(BF16)
        return (_silu(_dot(x, wg_ref[0].astype(BF16))) * _dot(x, wu_ref[0].astype(BF16))).astype(BF16)

    def down():
        y_ref[...] = _pack_bf16_pair(_dot(act_ref[...], wd_ref[0].astype(BF16)))

    @pl.when(b == 0)
    def _():
        act_ref[...] = gate_up()

    @pl.when((b > 0) & (b < n_used))
    def _():
        down()
        act_ref[...] = gate_up()

    @pl.when(b == n_used)
    def _():
        down()


def _experts(block_expert, n_used, xs, wg, wu, wd):
    n_rows, w = xs.shape
    n_blocks = n_rows // EXP_BLK
    d, f = wg.shape[1], wg.shape[2]
    cur = lambda b, be, nu: (jnp.minimum(b, nu[0] - 1), 0)
    prev = lambda b, be, nu: (jnp.clip(b - 1, 0, nu[0] - 1), 0)
    wcur = lambda b, be, nu: (be[jnp.minimum(b, nu[0] - 1)], 0, 0)
    wprev = lambda b, be, nu: (be[jnp.clip(b - 1, 0, nu[0] - 1)], 0, 0)
    grid_spec = pltpu.PrefetchScalarGridSpec(
        num_scalar_prefetch=2,
        grid=(n_blocks + 1,),
        in_specs=[pl.BlockSpec((EXP_BLK, w), cur),
                  pl.BlockSpec((1, d, f), wcur),
                  pl.BlockSpec((1, d, f), wcur),
                  pl.BlockSpec((1, f, d), wprev)],
        out_specs=pl.BlockSpec((EXP_BLK, w), prev),
        scratch_shapes=[pltpu.VMEM((EXP_BLK, f), BF16)],
    )
    return pl.pallas_call(
        _expert_kernel,
        out_shape=jax.ShapeDtypeStruct((n_rows, w), U32),
        grid_spec=grid_spec,
        compiler_params=_cparams(("arbitrary",)),
        name="experts",
    )(block_expert, n_used, xs, wg, wu, wd)


def _combine_kernel(ytok_ref, x1s_ref, w_ref, g2_ref, nf_ref, mf0_ref, mf1_ref, o_ref):
    w = w_ref[...]
    routed = w[:, 0:1] * _unpack_bf16_pair(ytok_ref[0])
    for k in range(1, TOP_K):
        routed = routed + w[:, k:k + 1] * _unpack_bf16_pair(ytok_ref[k])
    xo = x1s_ref[...] + g2_ref[0] * routed
    var = jnp.mean(xo * xo, axis=-1, keepdims=True)
    y = xo * lax.rsqrt(var + EPS) * nf_ref[...]
    o_ref[...] = y * (1.0 + mf1_ref[0]) + mf0_ref[0]


def _combine_alias_kernel(prev_ref, *refs):
    del prev_ref
    _combine_kernel(*refs)


def _combine(ytok, x1s, wts_t, mod3, normf_g, modf3, seq_len, part, out_prev):
    t, d = x1s.shape
    tm = COMB_TM
    steps = ytok.shape[1] // tm
    i0 = part * steps
    per_b = seq_len // tm
    row = lambda i: (i0 + i, 0)
    in_specs = [pl.BlockSpec((TOP_K, tm, d // 2), lambda i: (0, i, 0)),
                pl.BlockSpec((tm, d), row),
                pl.BlockSpec((tm, TOP_K), row),
                pl.BlockSpec((1, 1, d), lambda i: (((i0 + i) // per_b) * N_MOD + 5, 0, 0)),
                pl.BlockSpec((1, d), lambda i: (0, 0)),
                pl.BlockSpec((1, 1, d), lambda i: (((i0 + i) // per_b) * 2 + 0, 0, 0)),
                pl.BlockSpec((1, 1, d), lambda i: (((i0 + i) // per_b) * 2 + 1, 0, 0))]
    args = (ytok, x1s, wts_t, mod3, normf_g.reshape(1, d), modf3, modf3)
    if out_prev is None:
        body, aliases = _combine_kernel, {}
    else:
        body, aliases = _combine_alias_kernel, {0: 0}
        in_specs = [pl.BlockSpec(memory_space=pl.ANY)] + in_specs
        args = (out_prev,) + args
    return pl.pallas_call(
        body,
        out_shape=jax.ShapeDtypeStruct((t, d), F32),
        grid=(steps,),
        in_specs=in_specs,
        out_specs=pl.BlockSpec((tm, d), row),
        input_output_aliases=aliases,
        compiler_params=_cparams(("arbitrary",)),
        name="combine",
    )(*args)


def _lane_row(vec, lane0):
    return jnp.zeros((1, SMALL_W), F32).at[0, lane0:lane0 + vec.shape[0]].set(vec.astype(F32))


def _layer(x2, mod3, bsz, seq_len, norm1_g, w_in, fg_bias, conv_w, conv_b, dt_bias, a_log, d_skip,
           attn_norm_g, ssd_norm_g, w_out, norm2_g, w_router, router_bias,
           w_gate_e, w_up_e, w_down_e, w_gate_s, w_up_s, w_down_s):
    t, d = x2.shape
    o_q, o_k, o_v, o_fg = 0, ATTN_DIM, 2 * ATTN_DIM, 3 * ATTN_DIM
    o_z = o_fg + ATTN_HEADS
    o_xbc = o_z + SSD_DIM
    o_dt = o_xbc + SSD_CONV_DIM
    small = jnp.zeros((d, SMALL_W), F32)
    small = small.at[:, FG_LANE0:FG_LANE0 + ATTN_HEADS].set(w_in[:, o_fg:o_z])
    small = small.at[:, DT_LANE0:DT_LANE0 + SSD_HEADS].set(w_in[:, o_dt:o_dt + SSD_HEADS])
    w_cat = jnp.concatenate([w_in[:, o_q:o_fg], w_in[:, o_z:o_dt], small], axis=1).astype(BF16)

    tri_in = jnp.tril(jnp.ones((IN_SUB, IN_SUB), F32)).astype(BF16)
    qa, ka, va, z2, xbc2, sm2 = _in_proj(x2, mod3, norm1_g, w_cat, _lane_row(fg_bias, FG_LANE0), tri_in,
                                         bsz, seq_len)
    shp = lambda a: a.reshape(bsz, seq_len, a.shape[-1])
    attn3 = _fox_attn(qa, ka, va, attn_norm_g.reshape(1, ATTN_DIM).astype(F32))

    tri_chunk = jnp.tril(jnp.ones((SSD_CHUNK, SSD_CHUNK), F32)).astype(BF16)
    head_of_lane = jnp.arange(SSD_DIM, dtype=I32) // SSD_HEAD_DIM
    expand = (jnp.arange(SMALL_W, dtype=I32)[:, None] == head_of_lane[None, :] + DT_LANE0).astype(BF16)
    dskip_row = jnp.repeat(d_skip.astype(F32), SSD_HEAD_DIM).reshape(1, SSD_DIM)
    ssd3 = _ssd(shp(xbc2), shp(z2), shp(sm2), conv_w.astype(F32), conv_b.reshape(1, -1).astype(F32),
                _lane_row(dt_bias, DT_LANE0), _lane_row(a_log, DT_LANE0), dskip_row,
                ssd_norm_g.reshape(1, SSD_DIM).astype(F32), tri_chunk, expand)

    upper = jnp.triu(jnp.ones((POST_TM, POST_TM), F32), 1).astype(BF16)
    x1s, h2p, eidx, wts, rank, cnt = _post_mixer(
        attn3.reshape(t, ATTN_DIM), ssd3.reshape(t, SSD_DIM), x2, mod3,
        w_out[:ATTN_DIM].astype(BF16), w_out[ATTN_DIM:].astype(BF16), norm2_g,
        w_router.T.astype(BF16), router_bias.reshape(N_EXPERTS, 1).astype(F32),
        w_gate_s.astype(BF16), w_up_s.astype(BF16), w_down_s.astype(BF16), upper, seq_len)

    counts = cnt[:, 0].astype(I32)
    padded = (counts + EXP_BLK - 1) // EXP_BLK * EXP_BLK
    pend = jnp.cumsum(padded)
    pstart = pend - padded
    n_blocks = (t * TOP_K) // EXP_BLK + N_EXPERTS
    block_start = jnp.arange(n_blocks, dtype=I32) * EXP_BLK
    block_expert = jnp.minimum(jnp.sum((pend[None, :] <= block_start[:, None]).astype(I32), axis=1),
                               N_EXPERTS - 1)
    n_used = (pend[-1] // EXP_BLK).astype(I32).reshape(1)
    dest = _dest_rows(eidx, rank, pstart)

    xs = _dispatch(dest, h2p, n_blocks * EXP_BLK)
    ys = _experts(block_expert, n_used, xs, w_gate_e, w_up_e, w_down_e)
    return dest, ys, x1s, wts


def kernel(x, c, norm1_g, w_ada, b_ada, w_in, fg_bias, conv_w, conv_b, dt_bias, a_log, d_skip, attn_norm_g,
           ssd_norm_g, w_out, norm2_g, w_router, router_bias, w_gate_e, w_up_e, w_down_e, w_gate_s, w_up_s,
           w_down_s, normf_g, w_ada_f, b_ada_f):
    bsz, seq_len, d = x.shape
    assert w_ada.shape[0] == 1, "single-layer kernel"
    t = bsz * seq_len
    mod3 = _modulation(c, w_ada[0], b_ada[0]).reshape(bsz * N_MOD, 1, d)
    modf3 = _modulation(c, w_ada_f, b_ada_f).reshape(bsz * 2, 1, d)
    x2 = x.reshape(t, d)
    dest, ys, x1s, wts = _layer(
        x2, mod3, bsz, seq_len, norm1_g[0], w_in[0], fg_bias[0], conv_w[0], conv_b[0], dt_bias[0], a_log[0],
        d_skip[0], attn_norm_g[0], ssd_norm_g[0], w_out[0], norm2_g[0], w_router[0], router_bias[0],
        w_gate_e[0], w_up_e[0], w_down_e[0], w_gate_s[0], w_up_s[0], w_down_s[0])
    wts_t = wts.T
    tp = t // COMB_PARTS
    out = None
    for p in range(COMB_PARTS):
        ytok = _undispatch(dest, ys, p * tp, tp)
        out = _combine(ytok, x1s, wts_t, mod3, normf_g, modf3, seq_len, p, out)
    return out.reshape(bsz, seq_len, d)
```

```python
import functools

import jax
import jax.numpy as jnp
from jax import lax
from jax.experimental import pallas as pl
from jax.experimental.pallas import tpu as pltpu
from jax.experimental.pallas import tpu_sc as plsc

F32 = jnp.float32
BF16 = jnp.bfloat16
I32 = jnp.int32
U32 = jnp.uint32

EPS = 1e-6
D_MODEL = 1024
N_MOD = 6

ATTN_HEADS = 8
ATTN_HEAD_DIM = 64
ATTN_DIM = ATTN_HEADS * ATTN_HEAD_DIM

SSD_HEADS = 8
SSD_HEAD_DIM = 64
SSD_DIM = SSD_HEADS * SSD_HEAD_DIM
SSD_GROUPS = 2
SSD_STATE = 128
SSD_CONV = 4
SSD_CHUNK = 128
SSD_CONV_DIM = SSD_DIM + 2 * SSD_GROUPS * SSD_STATE
SSD_GROUP_DIM = SSD_DIM // SSD_GROUPS

N_EXPERTS = 256
TOP_K = 8
N_ROUTE_GROUPS = 8
TOPK_ROUTE_GROUPS = 4
GROUP_SIZE = N_EXPERTS // N_ROUTE_GROUPS
EXPERT_DIM = 256
ROUTED_SCALE = 2.5

LANES = 128
SMALL_W = LANES
FG_LANE0 = 0
DT_LANE0 = 8
AUG_W = LANES

IN_TM = 1024
IN_SUB = 512
ATT_TQ = 512
ATT_TK = 512
ATT_MASK_BLK = 128
ATT_MASK_LANES = 32
ATT_HEADS_PER_STEP = 8
POST_TM = 512
SC_CHUNK = 128
EXP_BLK = 1024
COMB_TM = 256
COMB_PARTS = 8
VMEM_LIMIT = 56 * 1024 * 1024
NEG_BIG = -1e30
LOG2E = 1.4426950408889634
DEST_TM = 2048


def _split3(x):
    hi = x.astype(BF16)
    r1 = x - hi.astype(F32)
    mid = r1.astype(BF16)
    lo = (r1 - mid.astype(F32)).astype(BF16)
    return hi, mid, lo


def _dot(a, b):
    return jnp.dot(a, b, preferred_element_type=F32)


def _dot_nt(a, b):
    return lax.dot_general(a, b, (((1,), (1,)), ((), ())), preferred_element_type=F32)


def _dot_exact_lhs01(lhs_bf16, x, pieces=3):
    parts = _split3(x)[:pieces]
    out = _dot(lhs_bf16, parts[0])
    for p in parts[1:]:
        out = out + _dot(lhs_bf16, p)
    return out


def _dot_exact_rhs01(x, rhs_bf16, pieces=2):
    parts = _split3(x)[:pieces]
    out = _dot(parts[0], rhs_bf16)
    for p in parts[1:]:
        out = out + _dot(p, rhs_bf16)
    return out


def _sigmoid(x):
    return 1.0 / (1.0 + jnp.exp(-x))


def _silu(x):
    return x * _sigmoid(x)


def _softplus(x):
    return jnp.maximum(x, 0.0) + jnp.log(1.0 + jnp.exp(-jnp.abs(x)))


def _log_sigmoid(x):
    return jnp.minimum(x, 0.0) - jnp.log(1.0 + jnp.exp(-jnp.abs(x)))


def _pack_bf16_pair(x):
    n = x.shape[1] // 2
    lo = pltpu.bitcast(x[:, :n].astype(BF16).astype(F32), U32)
    hi = pltpu.bitcast(x[:, n:].astype(BF16).astype(F32), U32)
    return (hi & jnp.uint32(0xFFFF0000)) | (lo >> 16)


def _unpack_bf16_pair(w):
    lo = pltpu.bitcast(w << 16, F32)
    hi = pltpu.bitcast(w & jnp.uint32(0xFFFF0000), F32)
    return jnp.concatenate([lo, hi], axis=1)


def _cparams(sem):
    return pltpu.CompilerParams(dimension_semantics=sem, vmem_limit_bytes=VMEM_LIMIT)


def _mod_kernel(c_ref, w_ref, b_ref, o_ref):
    c = c_ref[...]
    o_ref[...] = jnp.dot(_silu(c), w_ref[...], preferred_element_type=F32,
                         precision=lax.Precision.HIGHEST) + b_ref[...]


def _modulation(c, w, b):
    bsz, d = c.shape
    n = w.shape[1]
    tn = 1024
    return pl.pallas_call(
        _mod_kernel,
        out_shape=jax.ShapeDtypeStruct((bsz, n), F32),
        grid=(n // tn,),
        in_specs=[pl.BlockSpec((bsz, d), lambda j: (0, 0)),
                  pl.BlockSpec((d, tn), lambda j: (0, j)),
                  pl.BlockSpec((1, tn), lambda j: (0, j))],
        out_specs=pl.BlockSpec((bsz, tn), lambda j: (0, j)),
        compiler_params=_cparams(("arbitrary",)),
        name="modulation",
    )(c, w, b.reshape(1, n))


_COL_Q, _COL_K, _COL_V, _COL_Z, _COL_XBC, _COL_SM, _COL_END = 0, 512, 1024, 1536, 2048, 3072, 3200


def _inproj_kernel(per_b, x_ref, g_ref, sc_ref, sh_ref, w_ref, fgb_ref, tri_ref,
                   qa_ref, ka_ref, va_ref, z_ref, xbc_ref, sm_ref, carry_ref):
    @pl.when(pl.program_id(0) % per_b == 0)
    def _():
        carry_ref[...] = jnp.zeros_like(carry_ref)

    carry = carry_ref[...]
    for r0 in range(0, x_ref.shape[0], IN_SUB):
        pos0 = (pl.program_id(0) % per_b) * x_ref.shape[0] + r0
        carry = _inproj_rows(r0, pos0, carry, x_ref, g_ref, sc_ref, sh_ref, w_ref, fgb_ref, tri_ref,
                             qa_ref, ka_ref, va_ref, z_ref, xbc_ref, sm_ref)
    carry_ref[...] = carry


def _inproj_rows(r0, pos0, carry, x_ref, g_ref, sc_ref, sh_ref, w_ref, fgb_ref, tri_ref,
                 qa_ref, ka_ref, va_ref, z_ref, xbc_ref, sm_ref):
    tm = IN_SUB
    rows = slice(r0, r0 + tm)
    x = x_ref[rows, :]
    var = jnp.mean(x * x, axis=-1, keepdims=True)
    h = x * lax.rsqrt(var + EPS) * g_ref[...]
    h = h * (1.0 + sc_ref[0]) + sh_ref[0]
    hb = h.astype(BF16)
    sm = _dot(hb, w_ref[:, _COL_SM:_COL_END])
    sm_ref[rows, :] = sm
    log_f = _log_sigmoid(sm + fgb_ref[...]) * LOG2E
    cum = _dot_exact_lhs01(tri_ref[...], log_f) + carry
    hi, mid, lo = (p.astype(F32) for p in _split3(cum))
    nb = ATT_MASK_BLK
    lane1 = lax.broadcasted_iota(I32, (1, AUG_W), 1)
    left1 = lane1 < ATTN_HEAD_DIM

    def block_lanes(base, blk):
        rel = lane1 - (base + 6)
        in_range = (rel >= 0) & (rel < ATT_MASK_LANES)
        return jnp.where(in_range & (rel == blk), 1.0, 0.0), jnp.where(in_range & (rel < blk), NEG_BIG, 0.0)

    def head_rows(out_ref, col0, scale, book_of_head):
        f = _dot(hb, w_ref[:, col0:col0 + ATTN_DIM])
        if scale is not None:
            f = f * scale
        for a in range(tm // nb):
            band = slice(a * nb, (a + 1) * nb)
            blk = pos0 // nb + a
            for pr in range(ATTN_HEADS // 2):
                pair = f[band, pr * AUG_W:(pr + 1) * AUG_W]
                dst = slice(r0 + a * nb, r0 + (a + 1) * nb)
                out_ref[0, 2 * pr, dst, :] = jnp.where(
                    left1, pair, book_of_head(2 * pr, ATTN_HEAD_DIM, band, blk)).astype(BF16)
                out_ref[0, 2 * pr + 1, dst, :] = jnp.where(
                    left1, book_of_head(2 * pr + 1, 0, band, blk), pair).astype(BF16)

    def cols(hd, band):
        c0 = FG_LANE0 + hd
        return hi[band, c0:c0 + 1], mid[band, c0:c0 + 1], lo[band, c0:c0 + 1]

    def book_q(hd, base, band, blk):
        chi, cmid, clo = cols(hd, band)
        rel = lane1 - base
        return jnp.where(rel == 0, chi, jnp.where(rel == 1, cmid, jnp.where(rel == 2, clo,
                         jnp.where((rel >= 3) & (rel < 6), 1.0, block_lanes(base, blk)[0]))))

    def book_k(hd, base, band, blk):
        chi, cmid, clo = cols(hd, band)
        rel = lane1 - base
        return jnp.where(rel == 3, -chi, jnp.where(rel == 4, -cmid, jnp.where(rel == 5, -clo,
                         jnp.where((rel >= 0) & (rel < 3), 1.0, block_lanes(base, blk)[1]))))

    def book_v(hd, base, band, blk):
        return jnp.where(lane1 == base, 1.0, 0.0)

    head_rows(qa_ref, _COL_Q, ATTN_HEAD_DIM ** -0.5 * LOG2E, book_q)
    head_rows(ka_ref, _COL_K, None, book_k)
    head_rows(va_ref, _COL_V, None, book_v)
    z_ref[rows, :] = _dot(hb, w_ref[:, _COL_Z:_COL_XBC]).astype(BF16)
    xbc_ref[rows, :512] = _dot(hb, w_ref[:, _COL_XBC:_COL_XBC + 512]).astype(BF16)
    xbc_ref[rows, 512:] = _dot(hb, w_ref[:, _COL_XBC + 512:_COL_SM]).astype(BF16)
    return cum[tm - 1:tm, :]


def _in_proj(x2, mod3, norm_g, w_cat, fgb_row, tri, bsz, seq_len):
    assert seq_len // ATT_MASK_BLK <= ATT_MASK_LANES, "position blocks must fit the bookkeeping lanes"
    t, d = x2.shape
    tm = IN_TM
    per_b = seq_len // tm
    row = lambda i: (i, 0)
    const = lambda i: (0, 0)
    aug = jax.ShapeDtypeStruct((bsz, ATTN_HEADS, seq_len, AUG_W), BF16)
    aug_spec = pl.BlockSpec((1, ATTN_HEADS, tm, AUG_W), lambda i: (i // per_b, 0, i % per_b, 0))
    return pl.pallas_call(
        functools.partial(_inproj_kernel, per_b),
        out_shape=(aug, aug, aug,
                   jax.ShapeDtypeStruct((t, SSD_DIM), BF16),
                   jax.ShapeDtypeStruct((t, SSD_CONV_DIM), BF16),
                   jax.ShapeDtypeStruct((t, SMALL_W), F32)),
        grid=(t // tm,),
        in_specs=[pl.BlockSpec((tm, d), row),
                  pl.BlockSpec((1, d), const),
                  pl.BlockSpec((1, 1, d), lambda i: ((i // per_b) * N_MOD + 1, 0, 0)),
                  pl.BlockSpec((1, 1, d), lambda i: ((i // per_b) * N_MOD + 0, 0, 0)),
                  pl.BlockSpec((d, _COL_END), const),
                  pl.BlockSpec((1, SMALL_W), const),
                  pl.BlockSpec((IN_SUB, IN_SUB), const)],
        out_specs=(aug_spec, aug_spec, aug_spec,
                   pl.BlockSpec((tm, SSD_DIM), row),
                   pl.BlockSpec((tm, SSD_CONV_DIM), row),
                   pl.BlockSpec((tm, SMALL_W), row)),
        scratch_shapes=[pltpu.VMEM((1, SMALL_W), F32)],
        compiler_params=_cparams(("arbitrary",)),
        name="in_proj",
    )(x2, norm_g.reshape(1, d), mod3, mod3, w_cat, fgb_row, tri)


def _attn_kernel(qa_ref, ka_ref, va_ref, g_ref, o_ref):
    i = pl.program_id(2)
    tq, tk = ATT_TQ, ATT_TK
    nh = ATT_HEADS_PER_STEP
    sub = ATT_MASK_BLK
    diff = lax.broadcasted_iota(I32, (sub, sub), 0) - lax.broadcasted_iota(I32, (sub, sub), 1)

    def fine_mask(s, j):
        keep = diff >= (j - i) * tk
        bands = []
        for a in range(tq // sub):
            band = s[a * sub:(a + 1) * sub, :]
            parts = [band[:, :a * sub]] if a else []
            parts.append(jnp.where(keep, band[:, a * sub:(a + 1) * sub], NEG_BIG))
            if (a + 1) * sub < tk:
                parts.append(band[:, (a + 1) * sub:])
            bands.append(jnp.concatenate(parts, axis=1))
        return jnp.concatenate(bands, axis=0)

    def scores(hh, j):
        off = pl.multiple_of(j * tk, tk)
        return _dot_nt(qa_ref[0, hh], ka_ref[0, hh, pl.ds(off, tk), :])

    def update(hh, state, s, j):
        m_old, acc = state
        off = pl.multiple_of(j * tk, tk)
        s = fine_mask(s, j)
        m_new = jnp.maximum(m_old, jnp.max(s, axis=-1, keepdims=True))
        p = jnp.exp2(s - m_new).astype(BF16)
        acc = jnp.exp2(m_old - m_new) * acc + _dot(p, va_ref[0, hh, pl.ds(off, tk), :])
        return m_new, acc

    def body(j, states):
        ss = [scores(hh, j) for hh in range(nh)]
        return tuple(update(hh, states[hh], ss[hh], j) for hh in range(nh))

    init = tuple((jnp.full((tq, 1), NEG_BIG, F32), jnp.zeros((tq, LANES), F32)) for _ in range(nh))
    states = lax.fori_loop(0, i + 1, body, init)
    lane = lax.broadcasted_iota(I32, (tq, LANES), 1)
    first = lane < ATTN_HEAD_DIM
    for pr in range(nh // 2):
        acc_e, acc_o = states[2 * pr][1], states[2 * pr + 1][1]
        sum_e = jnp.sum(jnp.where(lane == ATTN_HEAD_DIM, acc_e, 0.0), axis=-1, keepdims=True)
        sum_o = jnp.sum(jnp.where(lane == 0, acc_o, 0.0), axis=-1, keepdims=True)
        o = jnp.where(first, acc_e / sum_e, acc_o / sum_o)
        sq = o * o
        s_all = jnp.sum(sq, axis=-1, keepdims=True)
        s0 = jnp.sum(jnp.where(first, sq, 0.0), axis=-1, keepdims=True)
        ms = jnp.where(first, s0, s_all - s0) * (1.0 / ATTN_HEAD_DIM)
        gsl = slice(pr * LANES, (pr + 1) * LANES)
        o_ref[0, :, gsl] = (o * lax.rsqrt(ms + EPS) * g_ref[:, gsl]).astype(BF16)


def _fox_attn(qa, ka, va, g_row):
    bsz, _, seq_len, _ = qa.shape
    tq = ATT_TQ
    nh = ATT_HEADS_PER_STEP
    wo = nh // 2 * LANES
    kv_spec = pl.BlockSpec((1, nh, seq_len, AUG_W), lambda b, p, i: (b, p, 0, 0))
    return pl.pallas_call(
        _attn_kernel,
        out_shape=jax.ShapeDtypeStruct((bsz, seq_len, ATTN_DIM), BF16),
        grid=(bsz, ATTN_HEADS // nh, seq_len // tq),
        in_specs=[pl.BlockSpec((1, nh, tq, AUG_W), lambda b, p, i: (b, p, i, 0)),
                  kv_spec, kv_spec,
                  pl.BlockSpec((1, wo), lambda b, p, i: (0, p))],
        out_specs=pl.BlockSpec((1, tq, wo), lambda b, p, i: (b, i, p)),
        compiler_params=_cparams(("arbitrary", "arbitrary", "arbitrary")),
        name="fox_attn",
    )(qa, ka, va, g_row)


def _ssd_kernel(xbc_ref, z_ref, sm_ref, cw_ref, cb_ref, dtb_ref, alog_ref, dsk_ref, g_ref, tri_ref, exp_ref,
                o_ref, ext_ref, state_ref):
    c = pl.program_id(1)
    q = SSD_CHUNK

    @pl.when(c == 0)
    def _():
        ext_ref[0:8, :] = jnp.zeros((8, SSD_CONV_DIM), F32)
        state_ref[...] = jnp.zeros_like(state_ref)

    ext_ref[8:8 + q, :] = xbc_ref[0].astype(F32)
    conv = cb_ref[...] + cw_ref[0:1, :] * ext_ref[5:5 + q, :]
    for j in range(1, SSD_CONV):
        conv = conv + cw_ref[j:j + 1, :] * ext_ref[5 + j:5 + j + q, :]
    ext_ref[0:8, :] = ext_ref[q:q + 8, :]
    xc = _silu(conv)
    xs = xc[:, :SSD_DIM]

    dt = _softplus(sm_ref[0] + dtb_ref[...])
    a_dt = -jnp.exp(alog_ref[...]) * dt
    a_cs = _dot_exact_lhs01(tri_ref[...], a_dt)
    a_last = a_cs[q - 1:q, :]
    e_cs = jnp.exp(a_cs)
    dec = jnp.exp(a_last - a_cs)
    a_cs_t = a_cs.T
    expand = exp_ref[...]
    dt_x = _dot_exact_rhs01(dt, expand)
    e_x = _dot_exact_rhs01(e_cs, expand)
    dec_x = _dot_exact_rhs01(dec, expand)
    x_dt = xs * dt_x
    x_dec = (x_dt * dec_x).astype(BF16)
    x_dt_b = x_dt.astype(BF16)

    row = lax.broadcasted_iota(I32, (q, q), 0)
    col = lax.broadcasted_iota(I32, (q, q), 1)
    lower = row >= col
    lane = lax.broadcasted_iota(I32, (q, LANES), 1)
    first = lane < SSD_HEAD_DIM
    y_parts = []
    for g in range(SSD_GROUPS):
        b_g = xc[:, SSD_DIM + g * SSD_STATE:SSD_DIM + (g + 1) * SSD_STATE]
        c_g = xc[:, SSD_DIM + (SSD_GROUPS + g) * SSD_STATE:SSD_DIM + (SSD_GROUPS + g + 1) * SSD_STATE]
        c_gb = c_g.astype(BF16)
        cb = _dot_nt(c_gb, b_g.astype(BF16))
        gs = slice(g * SSD_GROUP_DIM, (g + 1) * SSD_GROUP_DIM)
        st_prev = state_ref[g]
        y_off = _dot(c_gb, st_prev.astype(BF16)) * e_x[:, gs]
        s_new = _dot(b_g.T.astype(BF16), x_dec[:, gs])
        state_ref[g] = st_prev * e_x[q - 1:q, gs] + s_new
        for pr in range(2):
            pair = []
            for hh in range(2):
                h = g * 4 + pr * 2 + hh
                a_col = a_cs[:, DT_LANE0 + h:DT_LANE0 + h + 1]
                a_row = a_cs_t[DT_LANE0 + h:DT_LANE0 + h + 1, :]
                lmat = jnp.where(lower, jnp.exp(jnp.minimum(a_col - a_row, 0.0)), 0.0)
                m_h = (cb * lmat).astype(BF16)
                ps = slice((g * 2 + pr) * LANES, (g * 2 + pr + 1) * LANES)
                pair.append(_dot(m_h, x_dt_b[:, ps]))
            y_parts.append(jnp.where(first, pair[0], pair[1]) + y_off[:, pr * LANES:(pr + 1) * LANES])
    y = jnp.concatenate(y_parts, axis=1) + dsk_ref[...] * xs
    y = y * _silu(z_ref[0].astype(F32))
    outs = []
    for g in range(SSD_GROUPS):
        yg = y[:, g * SSD_GROUP_DIM:(g + 1) * SSD_GROUP_DIM]
        ms = jnp.mean(yg * yg, axis=-1, keepdims=True)
        outs.append(yg * lax.rsqrt(ms + EPS))
    o_ref[0] = (jnp.concatenate(outs, axis=1) * g_ref[...]).astype(BF16)


def _ssd(xbc3, z3, sm3, conv_w, conv_b, dtb_row, alog_row, dskip_row, g_row, tri, expand):
    bsz, seq_len, _ = xbc3.shape
    q = SSD_CHUNK
    blk = lambda b, c: (b, c, 0)
    const = lambda b, c: (0, 0)
    return pl.pallas_call(
        _ssd_kernel,
        out_shape=jax.ShapeDtypeStruct((bsz, seq_len, SSD_DIM), BF16),
        grid=(bsz, seq_len // q),
        in_specs=[pl.BlockSpec((1, q, SSD_CONV_DIM), blk),
                  pl.BlockSpec((1, q, SSD_DIM), blk),
                  pl.BlockSpec((1, q, SMALL_W), blk),
                  pl.BlockSpec((SSD_CONV, SSD_CONV_DIM), const),
                  pl.BlockSpec((1, SSD_CONV_DIM), const),
                  pl.BlockSpec((1, SMALL_W), const),
                  pl.BlockSpec((1, SMALL_W), const),
                  pl.BlockSpec((1, SSD_DIM), const),
                  pl.BlockSpec((1, SSD_DIM), const),
                  pl.BlockSpec((q, q), const),
                  pl.BlockSpec((SMALL_W, SSD_DIM), const)],
        out_specs=pl.BlockSpec((1, q, SSD_DIM), blk),
        scratch_shapes=[pltpu.VMEM((q + 8, SSD_CONV_DIM), F32),
                        pltpu.VMEM((SSD_GROUPS, SSD_STATE, SSD_GROUP_DIM), F32)],
        compiler_params=_cparams(("arbitrary", "arbitrary")),
        name="ssd",
    )(xbc3, z3, sm3, conv_w, conv_b, dtb_row, alog_row, dskip_row, g_row, tri, expand)


def _post_kernel(attn_ref, ssd_ref, x_ref, g1_ref, sh2_ref, sc2_ref, g2_ref, woa_ref, wos_ref, n2_ref,
                 wrt_ref, rb_ref, wgs_ref, wus_ref, wds_ref, upper_ref,
                 x1s_ref, h2_ref, eidx_ref, wts_ref, rank_ref, cnt_ref):
    i = pl.program_id(0)
    tm = x_ref.shape[0]

    @pl.when(i == 0)
    def _():
        cnt_ref[...] = jnp.zeros_like(cnt_ref)

    mixed = _dot(attn_ref[...], woa_ref[...]) + _dot(ssd_ref[...], wos_ref[...])
    x1 = x_ref[...] + g1_ref[0] * mixed
    var = jnp.mean(x1 * x1, axis=-1, keepdims=True)
    h2 = x1 * lax.rsqrt(var + EPS) * n2_ref[...]
    h2 = h2 * (1.0 + sc2_ref[0]) + sh2_ref[0]
    hb = h2.astype(BF16)
    h2_ref[...] = _pack_bf16_pair(h2)
    act = _silu(_dot(hb, wgs_ref[...])) * _dot(hb, wus_ref[...])
    shared = _dot(act.astype(BF16), wds_ref[...])
    x1s_ref[...] = x1 + g2_ref[0] * shared

    scores = _sigmoid(_dot_nt(wrt_ref[...], hb))
    biased = scores + rb_ref[...]
    grp = biased.reshape(N_ROUTE_GROUPS, GROUP_SIZE, tm)
    gi = lax.broadcasted_iota(I32, grp.shape, 1)
    m1 = jnp.max(grp, axis=1, keepdims=True)
    i1 = jnp.min(jnp.where(grp == m1, gi, GROUP_SIZE), axis=1, keepdims=True)
    m2 = jnp.max(jnp.where(gi == i1, -jnp.inf, grp), axis=1, keepdims=True)
    gsc = (m1 + m2).reshape(N_ROUTE_GROUPS, tm)
    gidx = lax.broadcasted_iota(I32, gsc.shape, 0)
    beaten = jnp.zeros(gsc.shape, I32)
    for o in range(N_ROUTE_GROUPS):
        other = gsc[o:o + 1, :]
        beats = (other > gsc) | ((other == gsc) & (gidx > o))
        beaten = beaten + beats.astype(I32)
    gmask = (beaten < TOPK_ROUTE_GROUPS).astype(F32)
    emask = jnp.broadcast_to(gmask.reshape(N_ROUTE_GROUPS, 1, tm), grp.shape).reshape(N_EXPERTS, tm)
    masked = jnp.where(emask > 0.5, biased, -jnp.inf)
    eiota = lax.broadcasted_iota(I32, (N_EXPERTS, tm), 0)
    idx_rows, w_rows = [], []
    sel = jnp.zeros((N_EXPERTS, tm), F32)
    for _ in range(TOP_K):
        mk = jnp.max(masked, axis=0, keepdims=True)
        ik = jnp.min(jnp.where(masked == mk, eiota, N_EXPERTS), axis=0, keepdims=True)
        hit = eiota == ik
        w_rows.append(jnp.sum(jnp.where(hit, scores, 0.0), axis=0, keepdims=True))
        idx_rows.append(ik)
        masked = jnp.where(hit, -jnp.inf, masked)
        sel = jnp.where(hit, 1.0, sel)
    w_all = jnp.concatenate(w_rows, axis=0)
    wts_ref[...] = w_all / jnp.sum(w_all, axis=0, keepdims=True) * ROUTED_SCALE
    eidx_ref[...] = jnp.concatenate(idx_rows, axis=0)
    before = cnt_ref[...][:, 0:1] + _dot(sel.astype(BF16), upper_ref[...])
    rank_rows = [jnp.sum(jnp.where(eiota == ik, before, 0.0), axis=0, keepdims=True) for ik in idx_rows]
    rank_ref[...] = jnp.concatenate(rank_rows, axis=0).astype(I32)
    cnt_ref[...] = cnt_ref[...] + jnp.sum(sel, axis=1, keepdims=True)


def _post_mixer(attn2, ssd2, x2, mod3, w_out_a, w_out_s, norm2_g, w_router_t, rb_col,
                wgs, wus, wds, upper, seq_len):
    t, d = x2.shape
    tm = POST_TM
    per_b = seq_len // tm
    row = lambda i: (i, 0)
    const = lambda i: (0, 0)
    modspec = lambda k: pl.BlockSpec((1, 1, d), lambda i: ((i // per_b) * N_MOD + k, 0, 0))
    slot = lambda i: (0, i)
    return pl.pallas_call(
        _post_kernel,
        out_shape=(jax.ShapeDtypeStruct((t, d), F32),
                   jax.ShapeDtypeStruct((t, d // 2), U32),
                   jax.ShapeDtypeStruct((TOP_K, t), I32),
                   jax.ShapeDtypeStruct((TOP_K, t), F32),
                   jax.ShapeDtypeStruct((TOP_K, t), I32),
                   jax.ShapeDtypeStruct((N_EXPERTS, LANES), F32)),
        grid=(t // tm,),
        in_specs=[pl.BlockSpec((tm, ATTN_DIM), row),
                  pl.BlockSpec((tm, SSD_DIM), row),
                  pl.BlockSpec((tm, d), row),
                  modspec(2), modspec(3), modspec(4), modspec(5),
                  pl.BlockSpec((ATTN_DIM, d), const),
                  pl.BlockSpec((SSD_DIM, d), const),
                  pl.BlockSpec((1, d), const),
                  pl.BlockSpec((N_EXPERTS, d), const),
                  pl.BlockSpec((N_EXPERTS, 1), const),
                  pl.BlockSpec((d, EXPERT_DIM), const),
                  pl.BlockSpec((d, EXPERT_DIM), const),
                  pl.BlockSpec((EXPERT_DIM, d), const),
                  pl.BlockSpec((tm, tm), const)],
        out_specs=(pl.BlockSpec((tm, d), row),
                   pl.BlockSpec((tm, d // 2), row),
                   pl.BlockSpec((TOP_K, tm), slot),
                   pl.BlockSpec((TOP_K, tm), slot),
                   pl.BlockSpec((TOP_K, tm), slot),
                   pl.BlockSpec((N_EXPERTS, LANES), const)),
        compiler_params=_cparams(("arbitrary",)),
        name="post_mixer",
    )(attn2, ssd2, x2, mod3, mod3, mod3, mod3, w_out_a, w_out_s, norm2_g.reshape(1, d),
      w_router_t, rb_col, wgs, wus, wds, upper)


def _dest_kernel(eidx_ref, rank_ref, pst_ref, dest_ref):
    tm = eidx_ref.shape[1]
    eiota = lax.broadcasted_iota(I32, (N_EXPERTS, tm), 0)
    rows = []
    for k in range(TOP_K):
        onehot = jnp.where(eiota == eidx_ref[k:k + 1, :], 1.0, 0.0).astype(BF16)
        r = _dot(pst_ref[...], onehot)
        rows.append(r[0:1, :] + r[1:2, :] + r[2:3, :])
    dest_ref[...] = jnp.concatenate(rows, axis=0).astype(I32) + rank_ref[...]


def _dest_rows(eidx, rank, pstart):
    k, t = eidx.shape
    tm = min(DEST_TM, t)
    pieces = jnp.stack([pstart & 0xFF0000, pstart & 0xFF00, pstart & 0xFF], axis=0)
    pst = jnp.zeros((8, N_EXPERTS), F32).at[:3].set(pieces.astype(F32)).astype(BF16)
    slot = lambda i: (0, i)
    return pl.pallas_call(
        _dest_kernel,
        out_shape=jax.ShapeDtypeStruct((k, t), I32),
        grid=(t // tm,),
        in_specs=[pl.BlockSpec((k, tm), slot), pl.BlockSpec((k, tm), slot),
                  pl.BlockSpec((8, N_EXPERTS), lambda i: (0, 0))],
        out_specs=pl.BlockSpec((k, tm), slot),
        compiler_params=_cparams(("arbitrary",)),
        name="dest_rows",
    )(eidx, rank, pst)


def _sc_mesh():
    return plsc.VectorSubcoreMesh(core_axis_name="c", subcore_axis_name="s")


def _sc_worker(n_workers_per_core):
    return lax.axis_index("s") * n_workers_per_core + lax.axis_index("c")


def _dispatch(dest, h2p, n_rows):
    t, w = h2p.shape
    mesh = _sc_mesh()
    n_workers = mesh.num_cores * mesh.num_subcores
    per_w = t // n_workers
    ch = min(SC_CHUNK, per_w)

    def body(dest_hbm, h_hbm, xs_hbm, idx_v, rows_v, sem):
        base_w = _sc_worker(mesh.num_cores) * per_w

        @pl.loop(0, per_w // ch)
        def _(ci):
            base = pl.multiple_of(base_w + ci * ch, ch)
            pltpu.sync_copy(dest_hbm.at[:, pl.ds(base, ch)], idx_v)
            pltpu.sync_copy(h_hbm.at[pl.ds(base, ch)], rows_v)
            copies = [pltpu.async_copy(rows_v, xs_hbm.at[idx_v.at[k]], sem) for k in range(TOP_K)]
            for cp in copies:
                cp.wait()

    return pl.kernel(
        body,
        out_type=jax.ShapeDtypeStruct((n_rows, w), U32),
        mesh=mesh,
        scratch_types=[pltpu.VMEM((TOP_K, ch), I32), pltpu.VMEM((ch, w), U32), pltpu.SemaphoreType.DMA],
        name="dispatch",
    )(dest, h2p)


def _undispatch(dest, ys, t0, t):
    w = ys.shape[1]
    mesh = _sc_mesh()
    n_workers = mesh.num_cores * mesh.num_subcores
    per_w = t // n_workers
    ch = min(SC_CHUNK, per_w)

    def body(dest_hbm, ys_hbm, ytok_hbm, idx_v, rows_v, sem):
        base_w = _sc_worker(mesh.num_cores) * per_w

        @pl.loop(0, per_w // ch)
        def _(ci):
            base = pl.multiple_of(base_w + ci * ch, ch)
            pltpu.sync_copy(dest_hbm.at[:, pl.ds(t0 + base, ch)], idx_v)
            for k in range(TOP_K):
                pltpu.async_copy(ys_hbm.at[idx_v.at[k]], rows_v, sem).wait()
                pltpu.sync_copy(rows_v, ytok_hbm.at[k, pl.ds(base, ch)])

    return pl.kernel(
        body,
        out_type=jax.ShapeDtypeStruct((TOP_K, t, w), U32),
        mesh=mesh,
        scratch_types=[pltpu.VMEM((TOP_K, ch), I32), pltpu.VMEM((ch, w), U32), pltpu.SemaphoreType.DMA],
        name="undispatch",
    )(dest, ys)


def _expert_kernel(be_ref, nu_ref, x_ref, wg_ref, wu_ref, wd_ref, y_ref, act_ref):
    b = pl.program_id(0)
    n_used = nu_ref[0]

    def gate_up():
        x = _unpack_bf16_pair(x_ref[...]).astype(BF16)
        return (_silu(_dot(x, wg_ref[0].astype(BF16))) * _dot(x, wu_ref[0].astype(BF16))).astype(BF16)

    def down():
        y_ref[...] = _pack_bf16_pair(_dot(act_ref[...], wd_ref[0].astype(BF16)))

    @pl.when(b == 0)
    def _():
        act_ref[...] = gate_up()

    @pl.when((b > 0) & (b < n_used))
    def _():
        down()
        act_ref[...] = gate_up()

    @pl.when(b == n_used)
    def _():
        down()


def _experts(block_expert, n_used, xs, wg, wu, wd):
    n_rows, w = xs.shape
    n_blocks = n_rows // EXP_BLK
    d, f = wg.shape[1], wg.shape[2]
    cur = lambda b, be, nu: (jnp.minimum(b, nu[0] - 1), 0)
    prev = lambda b, be, nu: (jnp.clip(b - 1, 0, nu[0] - 1), 0)
    wcur = lambda b, be, nu: (be[jnp.minimum(b, nu[0] - 1)], 0, 0)
    wprev = lambda b, be, nu: (be[jnp.clip(b - 1, 0, nu[0] - 1)], 0, 0)
    grid_spec = pltpu.PrefetchScalarGridSpec(
        num_scalar_prefetch=2,
        grid=(n_blocks + 1,),
        in_specs=[pl.BlockSpec((EXP_BLK, w), cur),
                  pl.BlockSpec((1, d, f), wcur),
                  pl.BlockSpec((1, d, f), wcur),
                  pl.BlockSpec((1, f, d), wprev)],
        out_specs=pl.BlockSpec((EXP_BLK, w), prev),
        scratch_shapes=[pltpu.VMEM((EXP_BLK, f), BF16)],
    )
    return pl.pallas_call(
        _expert_kernel,
        out_shape=jax.ShapeDtypeStruct((n_rows, w), U32),
        grid_spec=grid_spec,
        compiler_params=_cparams(("arbitrary",)),
        name="experts",
    )(block_expert, n_used, xs, wg, wu, wd)


def _combine_kernel(ytok_ref, x1s_ref, w_ref, g2_ref, nf_ref, mf0_ref, mf1_ref, o_ref):
    w = w_ref[...]
    routed = w[:, 0:1] * _unpack_bf16_pair(ytok_ref[0])
    for k in range(1, TOP_K):
        routed = routed + w[:, k:k + 1] * _unpack_bf16_pair(ytok_ref[k])
    xo = x1s_ref[...] + g2_ref[0] * routed
    var = jnp.mean(xo * xo, axis=-1, keepdims=True)
    y = xo * lax.rsqrt(var + EPS) * nf_ref[...]
    o_ref[...] = y * (1.0 + mf1_ref[0]) + mf0_ref[0]


def _combine_alias_kernel(prev_ref, *refs):
    del prev_ref
    _combine_kernel(*refs)


def _combine(ytok, x1s, wts_t, mod3, normf_g, modf3, seq_len, part, out_prev):
    t, d = x1s.shape
    tm = COMB_TM
    steps = ytok.shape[1] // tm
    i0 = part * steps
    per_b = seq_len // tm
    row = lambda i: (i0 + i, 0)
    in_specs = [pl.BlockSpec((TOP_K, tm, d // 2), lambda i: (0, i, 0)),
                pl.BlockSpec((tm, d), row),
                pl.BlockSpec((tm, TOP_K), row),
                pl.BlockSpec((1, 1, d), lambda i: (((i0 + i) // per_b) * N_MOD + 5, 0, 0)),
                pl.BlockSpec((1, d), lambda i: (0, 0)),
                pl.BlockSpec((1, 1, d), lambda i: (((i0 + i) // per_b) * 2 + 0, 0, 0)),
                pl.BlockSpec((1, 1, d), lambda i: (((i0 + i) // per_b) * 2 + 1, 0, 0))]
    args = (ytok, x1s, wts_t, mod3, normf_g.reshape(1, d), modf3, modf3)
    if out_prev is None:
        body, aliases = _combine_kernel, {}
    else:
        body, aliases = _combine_alias_kernel, {0: 0}
        in_specs = [pl.BlockSpec(memory_space=pl.ANY)] + in_specs
        args = (out_prev,) + args
    return pl.pallas_call(
        body,
        out_shape=jax.ShapeDtypeStruct((t, d), F32),
        grid=(steps,),
        in_specs=in_specs,
        out_specs=pl.BlockSpec((tm, d), row),
        input_output_aliases=aliases,
        compiler_params=_cparams(("arbitrary",)),
        name="combine",
    )(*args)


def _lane_row(vec, lane0):
    return jnp.zeros((1, SMALL_W), F32).at[0, lane0:lane0 + vec.shape[0]].set(vec.astype(F32))


def _layer(x2, mod3, bsz, seq_len, norm1_g, w_in, fg_bias, conv_w, conv_b, dt_bias, a_log, d_skip,
           attn_norm_g, ssd_norm_g, w_out, norm2_g, w_router, router_bias,
           w_gate_e, w_up_e, w_down_e, w_gate_s, w_up_s, w_down_s):
    t, d = x2.shape
    o_q, o_k, o_v, o_fg = 0, ATTN_DIM, 2 * ATTN_DIM, 3 * ATTN_DIM
    o_z = o_fg + ATTN_HEADS
    o_xbc = o_z + SSD_DIM
    o_dt = o_xbc + SSD_CONV_DIM
    small = jnp.zeros((d, SMALL_W), F32)
    small = small.at[:, FG_LANE0:FG_LANE0 + ATTN_HEADS].set(w_in[:, o_fg:o_z])
    small = small.at[:, DT_LANE0:DT_LANE0 + SSD_HEADS].set(w_in[:, o_dt:o_dt + SSD_HEADS])
    w_cat = jnp.concatenate([w_in[:, o_q:o_fg], w_in[:, o_z:o_dt], small], axis=1).astype(BF16)

    tri_in = jnp.tril(jnp.ones((IN_SUB, IN_SUB), F32)).astype(BF16)
    qa, ka, va, z2, xbc2, sm2 = _in_proj(x2, mod3, norm1_g, w_cat, _lane_row(fg_bias, FG_LANE0), tri_in,
                                         bsz, seq_len)
    shp = lambda a: a.reshape(bsz, seq_len, a.shape[-1])
    attn3 = _fox_attn(qa, ka, va, attn_norm_g.reshape(1, ATTN_DIM).astype(F32))

    tri_chunk = jnp.tril(jnp.ones((SSD_CHUNK, SSD_CHUNK), F32)).astype(BF16)
    head_of_lane = jnp.arange(SSD_DIM, dtype=I32) // SSD_HEAD_DIM
    expand = (jnp.arange(SMALL_W, dtype=I32)[:, None] == head_of_lane[None, :] + DT_LANE0).astype(BF16)
    dskip_row = jnp.repeat(d_skip.astype(F32), SSD_HEAD_DIM).reshape(1, SSD_DIM)
    ssd3 = _ssd(shp(xbc2), shp(z2), shp(sm2), conv_w.astype(F32), conv_b.reshape(1, -1).astype(F32),
                _lane_row(dt_bias, DT_LANE0), _lane_row(a_log, DT_LANE0), dskip_row,
                ssd_norm_g.reshape(1, SSD_DIM).astype(F32), tri_chunk, expand)

    upper = jnp.triu(jnp.ones((POST_TM, POST_TM), F32), 1).astype(BF16)
    x1s, h2p, eidx, wts, rank, cnt = _post_mixer(
        attn3.reshape(t, ATTN_DIM), ssd3.reshape(t, SSD_DIM), x2, mod3,
        w_out[:ATTN_DIM].astype(BF16), w_out[ATTN_DIM:].astype(BF16), norm2_g,
        w_router.T.astype(BF16), router_bias.reshape(N_EXPERTS, 1).astype(F32),
        w_gate_s.astype(BF16), w_up_s.astype(BF16), w_down_s.astype(BF16), upper, seq_len)

    counts = cnt[:, 0].astype(I32)
    padded = (counts + EXP_BLK - 1) // EXP_BLK * EXP_BLK
    pend = jnp.cumsum(padded)
    pstart = pend - padded
    n_blocks = (t * TOP_K) // EXP_BLK + N_EXPERTS
    block_start = jnp.arange(n_blocks, dtype=I32) * EXP_BLK
    block_expert = jnp.minimum(jnp.sum((pend[None, :] <= block_start[:, None]).astype(I32), axis=1),
                               N_EXPERTS - 1)
    n_used = (pend[-1] // EXP_BLK).astype(I32).reshape(1)
    dest = _dest_rows(eidx, rank, pstart)

    xs = _dispatch(dest, h2p, n_blocks * EXP_BLK)
    ys = _experts(block_expert, n_used, xs, w_gate_e, w_up_e, w_down_e)
    return dest, ys, x1s, wts


def kernel(x, c, norm1_g, w_ada, b_ada, w_in, fg_bias, conv_w, conv_b, dt_bias, a_log, d_skip, attn_norm_g,
           ssd_norm_g, w_out, norm2_g, w_router, router_bias, w_gate_e, w_up_e, w_down_e, w_gate_s, w_up_s,
           w_down_s, normf_g, w_ada_f, b_ada_f):
    bsz, seq_len, d = x.shape
    assert w_ada.shape[0] == 1, "single-layer kernel"
    t = bsz * seq_len
    mod3 = _modulation(c, w_ada[0], b_ada[0]).reshape(bsz * N_MOD, 1, d)
    modf3 = _modulation(c, w_ada_f, b_ada_f).reshape(bsz * 2, 1, d)
    x2 = x.reshape(t, d)
    dest, ys, x1s, wts = _layer(
        x2, mod3, bsz, seq_len, norm1_g[0], w_in[0], fg_bias[0], conv_w[0], conv_b[0], dt_bias[0], a_log[0],
        d_skip[0], attn_norm_g[0], ssd_norm_g[0], w_out[0], norm2_g[0], w_router[0], router_bias[0],
        w_gate_e[0], w_up_e[0], w_down_e[0], w_gate_s[0], w_up_s[0], w_down_s[0])
    wts_t = wts.T
    tp = t // COMB_PARTS
    out = None
    for p in range(COMB_PARTS):
        ytok = _undispatch(dest, ys, p * tp, tp)
        out = _combine(ytok, x1s, wts_t, mod3, normf_g, modf3, seq_len, p, out)
    return out.reshape(bsz, seq_len, d)
```

```python
import functools

import jax
import jax.numpy as jnp
from jax import lax
from jax.experimental import pallas as pl
from jax.experimental.pallas import tpu as pltpu
from jax.experimental.pallas import tpu_sc as plsc

F32 = jnp.float32
BF16 = jnp.bfloat16
I32 = jnp.int32
U32 = jnp.uint32

EPS = 1e-6
D_MODEL = 1024
N_MOD = 6

ATTN_HEADS = 8
ATTN_HEAD_DIM = 64
ATTN_DIM = ATTN_HEADS * ATTN_HEAD_DIM

SSD_HEADS = 8
SSD_HEAD_DIM = 64
SSD_DIM = SSD_HEADS * SSD_HEAD_DIM
SSD_GROUPS = 2
SSD_STATE = 128
SSD_CONV = 4
SSD_CHUNK = 128
SSD_CONV_DIM = SSD_DIM + 2 * SSD_GROUPS * SSD_STATE
SSD_GROUP_DIM = SSD_DIM // SSD_GROUPS

N_EXPERTS = 256
TOP_K = 8
N_ROUTE_GROUPS = 8
TOPK_ROUTE_GROUPS = 4
GROUP_SIZE = N_EXPERTS // N_ROUTE_GROUPS
EXPERT_DIM = 256
ROUTED_SCALE = 2.5

LANES = 128
SMALL_W = LANES
FG_LANE0 = 0
DT_LANE0 = 8
AUG_W = LANES

IN_TM = 1024
IN_SUB = 512
ATT_TQ = 512
ATT_TK = 512
ATT_MASK_BLK = 128
ATT_MASK_LANES = 32
ATT_HEADS_PER_STEP = 8
POST_TM = 512
SC_CHUNK = 128
EXP_BLK = 1024
COMB_TM = 256
COMB_PARTS = 8
VMEM_LIMIT = 56 * 1024 * 1024
NEG_BIG = -1e30
LOG2E = 1.4426950408889634
DEST_TM = 2048


def _split3(x):
    hi = x.astype(BF16)
    r1 = x - hi.astype(F32)
    mid = r1.astype(BF16)
    lo = (r1 - mid.astype(F32)).astype(BF16)
    return hi, mid, lo


def _dot(a, b):
    return jnp.dot(a, b, preferred_element_type=F32)


def _dot_nt(a, b):
    return lax.dot_general(a, b, (((1,), (1,)), ((), ())), preferred_element_type=F32)


def _dot_exact_lhs01(lhs_bf16, x, pieces=3):
    parts = _split3(x)[:pieces]
    out = _dot(lhs_bf16, parts[0])
    for p in parts[1:]:
        out = out + _dot(lhs_bf16, p)
    return out


def _dot_exact_rhs01(x, rhs_bf16, pieces=2):
    parts = _split3(x)[:pieces]
    out = _dot(parts[0], rhs_bf16)
    for p in parts[1:]:
        out = out + _dot(p, rhs_bf16)
    return out


def _sigmoid(x):
    return 1.0 / (1.0 + jnp.exp(-x))


def _silu(x):
    return x * _sigmoid(x)


def _softplus(x):
    return jnp.maximum(x, 0.0) + jnp.log(1.0 + jnp.exp(-jnp.abs(x)))


def _log_sigmoid(x):
    return jnp.minimum(x, 0.0) - jnp.log(1.0 + jnp.exp(-jnp.abs(x)))


def _pack_bf16_pair(x):
    n = x.shape[1] // 2
    lo = pltpu.bitcast(x[:, :n].astype(BF16).astype(F32), U32)
    hi = pltpu.bitcast(x[:, n:].astype(BF16).astype(F32), U32)
    return (hi & jnp.uint32(0xFFFF0000)) | (lo >> 16)


def _unpack_bf16_pair(w):
    lo = pltpu.bitcast(w << 16, F32)
    hi = pltpu.bitcast(w & jnp.uint32(0xFFFF0000), F32)
    return jnp.concatenate([lo, hi], axis=1)


def _cparams(sem):
    return pltpu.CompilerParams(dimension_semantics=sem, vmem_limit_bytes=VMEM_LIMIT)


def _mod_kernel(c_ref, w_ref, b_ref, o_ref):
    c = c_ref[...]
    o_ref[...] = jnp.dot(_silu(c), w_ref[...], preferred_element_type=F32,
                         precision=lax.Precision.HIGHEST) + b_ref[...]


def _modulation(c, w, b):
    bsz, d = c.shape
    n = w.shape[1]
    tn = 1024
    return pl.pallas_call(
        _mod_kernel,
        out_shape=jax.ShapeDtypeStruct((bsz, n), F32),
        grid=(n // tn,),
        in_specs=[pl.BlockSpec((bsz, d), lambda j: (0, 0)),
                  pl.BlockSpec((d, tn), lambda j: (0, j)),
                  pl.BlockSpec((1, tn), lambda j: (0, j))],
        out_specs=pl.BlockSpec((bsz, tn), lambda j: (0, j)),
        compiler_params=_cparams(("arbitrary",)),
        name="modulation",
    )(c, w, b.reshape(1, n))


_COL_Q, _COL_K, _COL_V, _COL_Z, _COL_XBC, _COL_SM, _COL_END = 0, 512, 1024, 1536, 2048, 3072, 3200


def _inproj_kernel(per_b, x_ref, g_ref, sc_ref, sh_ref, w_ref, fgb_ref, tri_ref,
                   qa_ref, ka_ref, va_ref, z_ref, xbc_ref, sm_ref, carry_ref):
    @pl.when(pl.program_id(0) % per_b == 0)
    def _():
        carry_ref[...] = jnp.zeros_like(carry_ref)

    carry = carry_ref[...]
    for r0 in range(0, x_ref.shape[0], IN_SUB):
        pos0 = (pl.program_id(0) % per_b) * x_ref.shape[0] + r0
        carry = _inproj_rows(r0, pos0, carry, x_ref, g_ref, sc_ref, sh_ref, w_ref, fgb_ref, tri_ref,
                             qa_ref, ka_ref, va_ref, z_ref, xbc_ref, sm_ref)
    carry_ref[...] = carry


def _inproj_rows(r0, pos0, carry, x_ref, g_ref, sc_ref, sh_ref, w_ref, fgb_ref, tri_ref,
                 qa_ref, ka_ref, va_ref, z_ref, xbc_ref, sm_ref):
    tm = IN_SUB
    rows = slice(r0, r0 + tm)
    x = x_ref[rows, :]
    var = jnp.mean(x * x, axis=-1, keepdims=True)
    h = x * lax.rsqrt(var + EPS) * g_ref[...]
    h = h * (1.0 + sc_ref[0]) + sh_ref[0]
    hb = h.astype(BF16)
    sm = _dot(hb, w_ref[:, _COL_SM:_COL_END])
    sm_ref[rows, :] = sm
    log_f = _log_sigmoid(sm + fgb_ref[...]) * LOG2E
    cum = _dot_exact_lhs01(tri_ref[...], log_f) + carry
    hi, mid, lo = (p.astype(F32) for p in _split3(cum))
    nb = ATT_MASK_BLK
    lane1 = lax.broadcasted_iota(I32, (1, AUG_W), 1)
    left1 = lane1 < ATTN_HEAD_DIM

    def block_lanes(base, blk):
        rel = lane1 - (base + 6)
        in_range = (rel >= 0) & (rel < ATT_MASK_LANES)
        return jnp.where(in_range & (rel == blk), 1.0, 0.0), jnp.where(in_range & (rel < blk), NEG_BIG, 0.0)

    def head_rows(out_ref, col0, scale, book_of_head):
        f = _dot(hb, w_ref[:, col0:col0 + ATTN_DIM])
        if scale is not None:
            f = f * scale
        for a in range(tm // nb):
            band = slice(a * nb, (a + 1) * nb)
            blk = pos0 // nb + a
            for pr in range(ATTN_HEADS // 2):
                pair = f[band, pr * AUG_W:(pr + 1) * AUG_W]
                dst = slice(r0 + a * nb, r0 + (a + 1) * nb)
                out_ref[0, 2 * pr, dst, :] = jnp.where(
                    left1, pair, book_of_head(2 * pr, ATTN_HEAD_DIM, band, blk)).astype(BF16)
                out_ref[0, 2 * pr + 1, dst, :] = jnp.where(
                    left1, book_of_head(2 * pr + 1, 0, band, blk), pair).astype(BF16)

    def cols(hd, band):
        c0 = FG_LANE0 + hd
        return hi[band, c0:c0 + 1], mid[band, c0:c0 + 1], lo[band, c0:c0 + 1]

    def book_q(hd, base, band, blk):
        chi, cmid, clo = cols(hd, band)
        rel = lane1 - base
        return jnp.where(rel == 0, chi, jnp.where(rel == 1, cmid, jnp.where(rel == 2, clo,
                         jnp.where((rel >= 3) & (rel < 6), 1.0, block_lanes(base, blk)[0]))))

    def book_k(hd, base, band, blk):
        chi, cmid, clo = cols(hd, band)
        rel = lane1 - base
        return jnp.where(rel == 3, -chi, jnp.where(rel == 4, -cmid, jnp.where(rel == 5, -clo,
                         jnp.where((rel >= 0) & (rel < 3), 1.0, block_lanes(base, blk)[1]))))

    def book_v(hd, base, band, blk):
        return jnp.where(lane1 == base, 1.0, 0.0)

    head_rows(qa_ref, _COL_Q, ATTN_HEAD_DIM ** -0.5 * LOG2E, book_q)
    head_rows(ka_ref, _COL_K, None, book_k)
    head_rows(va_ref, _COL_V, None, book_v)
    z_ref[rows, :] = _dot(hb, w_ref[:, _COL_Z:_COL_XBC]).astype(BF16)
    xbc_ref[rows, :512] = _dot(hb, w_ref[:, _COL_XBC:_COL_XBC + 512]).astype(BF16)
    xbc_ref[rows, 512:] = _dot(hb, w_ref[:, _COL_XBC + 512:_COL_SM]).astype(BF16)
    return cum[tm - 1:tm, :]


def _in_proj(x2, mod3, norm_g, w_cat, fgb_row, tri, bsz, seq_len):
    assert seq_len // ATT_MASK_BLK <= ATT_MASK_LANES, "position blocks must fit the bookkeeping lanes"
    t, d = x2.shape
    tm = IN_TM
    per_b = seq_len // tm
    row = lambda i: (i, 0)
    const = lambda i: (0, 0)
    aug = jax.ShapeDtypeStruct((bsz, ATTN_HEADS, seq_len, AUG_W), BF16)
    aug_spec = pl.BlockSpec((1, ATTN_HEADS, tm, AUG_W), lambda i: (i // per_b, 0, i % per_b, 0))
    return pl.pallas_call(
        functools.partial(_inproj_kernel, per_b),
        out_shape=(aug, aug, aug,
                   jax.ShapeDtypeStruct((t, SSD_DIM), BF16),
                   jax.ShapeDtypeStruct((t, SSD_CONV_DIM), BF16),
                   jax.ShapeDtypeStruct((t, SMALL_W), F32)),
        grid=(t // tm,),
        in_specs=[pl.BlockSpec((tm, d), row),
                  pl.BlockSpec((1, d), const),
                  pl.BlockSpec((1, 1, d), lambda i: ((i // per_b) * N_MOD + 1, 0, 0)),
                  pl.BlockSpec((1, 1, d), lambda i: ((i // per_b) * N_MOD + 0, 0, 0)),
                  pl.BlockSpec((d, _COL_END), const),
                  pl.BlockSpec((1, SMALL_W), const),
                  pl.BlockSpec((IN_SUB, IN_SUB), const)],
        out_specs=(aug_spec, aug_spec, aug_spec,
                   pl.BlockSpec((tm, SSD_DIM), row),
                   pl.BlockSpec((tm, SSD_CONV_DIM), row),
                   pl.BlockSpec((tm, SMALL_W), row)),
        scratch_shapes=[pltpu.VMEM((1, SMALL_W), F32)],
        compiler_params=_cparams(("arbitrary",)),
        name="in_proj",
    )(x2, norm_g.reshape(1, d), mod3, mod3, w_cat, fgb_row, tri)


def _attn_kernel(qa_ref, ka_ref, va_ref, g_ref, o_ref):
    i = pl.program_id(2)
    tq, tk = ATT_TQ, ATT_TK
    nh = ATT_HEADS_PER_STEP
    sub = ATT_MASK_BLK
    diff = lax.broadcasted_iota(I32, (sub, sub), 0) - lax.broadcasted_iota(I32, (sub, sub), 1)

    def fine_mask(s, j):
        keep = diff >= (j - i) * tk
        bands = []
        for a in range(tq // sub):
            band = s[a * sub:(a + 1) * sub, :]
            parts = [band[:, :a * sub]] if a else []
            parts.append(jnp.where(keep, band[:, a * sub:(a + 1) * sub], NEG_BIG))
            if (a + 1) * sub < tk:
                parts.append(band[:, (a + 1) * sub:])
            bands.append(jnp.concatenate(parts, axis=1))
        return jnp.concatenate(bands, axis=0)

    def scores(hh, j):
        off = pl.multiple_of(j * tk, tk)
        return _dot_nt(qa_ref[0, hh], ka_ref[0, hh, pl.ds(off, tk), :])

    def update(hh, state, s, j):
        m_old, acc = state
        off = pl.multiple_of(j * tk, tk)
        s = fine_mask(s, j)
        m_new = jnp.maximum(m_old, jnp.max(s, axis=-1, keepdims=True))
        p = jnp.exp2(s - m_new).astype(BF16)
        acc = jnp.exp2(m_old - m_new) * acc + _dot(p, va_ref[0, hh, pl.ds(off, tk), :])
        return m_new, acc

    def body(j, states):
        ss = [scores(hh, j) for hh in range(nh)]
        return tuple(update(hh, states[hh], ss[hh], j) for hh in range(nh))

    init = tuple((jnp.full((tq, 1), NEG_BIG, F32), jnp.zeros((tq, LANES), F32)) for _ in range(nh))
    states = lax.fori_loop(0, i + 1, body, init)
    lane = lax.broadcasted_iota(I32, (tq, LANES), 1)
    first = lane < ATTN_HEAD_DIM
    for pr in range(nh // 2):
        acc_e, acc_o = states[2 * pr][1], states[2 * pr + 1][1]
        sum_e = jnp.sum(jnp.where(lane == ATTN_HEAD_DIM, acc_e, 0.0), axis=-1, keepdims=True)
        sum_o = jnp.sum(jnp.where(lane == 0, acc_o, 0.0), axis=-1, keepdims=True)
        o = jnp.where(first, acc_e / sum_e, acc_o / sum_o)
        sq = o * o
        s_all = jnp.sum(sq, axis=-1, keepdims=True)
        s0 = jnp.sum(jnp.where(first, sq, 0.0), axis=-1, keepdims=True)
        ms = jnp.where(first, s0, s_all - s0) * (1.0 / ATTN_HEAD_DIM)
        gsl = slice(pr * LANES, (pr + 1) * LANES)
        o_ref[0, :, gsl] = (o * lax.rsqrt(ms + EPS) * g_ref[:, gsl]).astype(BF16)


def _fox_attn(qa, ka, va, g_row):
    bsz, _, seq_len, _ = qa.shape
    tq = ATT_TQ
    nh = ATT_HEADS_PER_STEP
    wo = nh // 2 * LANES
    kv_spec = pl.BlockSpec((1, nh, seq_len, AUG_W), lambda b, p, i: (b, p, 0, 0))
    return pl.pallas_call(
        _attn_kernel,
        out_shape=jax.ShapeDtypeStruct((bsz, seq_len, ATTN_DIM), BF16),
        grid=(bsz, ATTN_HEADS // nh, seq_len // tq),
        in_specs=[pl.BlockSpec((1, nh, tq, AUG_W), lambda b, p, i: (b, p, i, 0)),
                  kv_spec, kv_spec,
                  pl.BlockSpec((1, wo), lambda b, p, i: (0, p))],
        out_specs=pl.BlockSpec((1, tq, wo), lambda b, p, i: (b, i, p)),
        compiler_params=_cparams(("arbitrary", "arbitrary", "arbitrary")),
        name="fox_attn",
    )(qa, ka, va, g_row)


def _ssd_kernel(xbc_ref, z_ref, sm_ref, cw_ref, cb_ref, dtb_ref, alog_ref, dsk_ref, g_ref, tri_ref, exp_ref,
                o_ref, ext_ref, state_ref):
    c = pl.program_id(1)
    q = SSD_CHUNK

    @pl.when(c == 0)
    def _():
        ext_ref[0:8, :] = jnp.zeros((8, SSD_CONV_DIM), F32)
        state_ref[...] = jnp.zeros_like(state_ref)

    ext_ref[8:8 + q, :] = xbc_ref[0].astype(F32)
    conv = cb_ref[...] + cw_ref[0:1, :] * ext_ref[5:5 + q, :]
    for j in range(1, SSD_CONV):
        conv = conv + cw_ref[j:j + 1, :] * ext_ref[5 + j:5 + j + q, :]
    ext_ref[0:8, :] = ext_ref[q:q + 8, :]
    xc = _silu(conv)
    xs = xc[:, :SSD_DIM]

    dt = _softplus(sm_ref[0] + dtb_ref[...])
    a_dt = -jnp.exp(alog_ref[...]) * dt
    a_cs = _dot_exact_lhs01(tri_ref[...], a_dt)
    a_last = a_cs[q - 1:q, :]
    e_cs = jnp.exp(a_cs)
    dec = jnp.exp(a_last - a_cs)
    a_cs_t = a_cs.T
    expand = exp_ref[...]
    dt_x = _dot_exact_rhs01(dt, expand)
    e_x = _dot_exact_rhs01(e_cs, expand)
    dec_x = _dot_exact_rhs01(dec, expand)
    x_dt = xs * dt_x
    x_dec = (x_dt * dec_x).astype(BF16)
    x_dt_b = x_dt.astype(BF16)

    row = lax.broadcasted_iota(I32, (q, q), 0)
    col = lax.broadcasted_iota(I32, (q, q), 1)
    lower = row >= col
    lane = lax.broadcasted_iota(I32, (q, LANES), 1)
    first = lane < SSD_HEAD_DIM
    y_parts = []
    for g in range(SSD_GROUPS):
        b_g = xc[:, SSD_DIM + g * SSD_STATE:SSD_DIM + (g + 1) * SSD_STATE]
        c_g = xc[:, SSD_DIM + (SSD_GROUPS + g) * SSD_STATE:SSD_DIM + (SSD_GROUPS + g + 1) * SSD_STATE]
        c_gb = c_g.astype(BF16)
        cb = _dot_nt(c_gb, b_g.astype(BF16))
        gs = slice(g * SSD_GROUP_DIM, (g + 1) * SSD_GROUP_DIM)
        st_prev = state_ref[g]
        y_off = _dot(c_gb, st_prev.astype(BF16)) * e_x[:, gs]
        s_new = _dot(b_g.T.astype(BF16), x_dec[:, gs])
        state_ref[g] = st_prev * e_x[q - 1:q, gs] + s_new
        for pr in range(2):
            pair = []
            for hh in range(2):
                h = g * 4 + pr * 2 + hh
                a_col = a_cs[:, DT_LANE0 + h:DT_LANE0 + h + 1]
                a_row = a_cs_t[DT_LANE0 + h:DT_LANE0 + h + 1, :]
                lmat = jnp.where(lower, jnp.exp(jnp.minimum(a_col - a_row, 0.0)), 0.0)
                m_h = (cb * lmat).astype(BF16)
                ps = slice((g * 2 + pr) * LANES, (g * 2 + pr + 1) * LANES)
                pair.append(_dot(m_h, x_dt_b[:, ps]))
            y_parts.append(jnp.where(first, pair[0], pair[1]) + y_off[:, pr * LANES:(pr + 1) * LANES])
    y = jnp.concatenate(y_parts, axis=1) + dsk_ref[...] * xs
    y = y * _silu(z_ref[0].astype(F32))
    outs = []
    for g in range(SSD_GROUPS):
        yg = y[:, g * SSD_GROUP_DIM:(g + 1) * SSD_GROUP_DIM]
        ms = jnp.mean(yg * yg, axis=-1, keepdims=True)
        outs.append(yg * lax.rsqrt(ms + EPS))
    o_ref[0] = (jnp.concatenate(outs, axis=1) * g_ref[...]).astype(BF16)


def _ssd(xbc3, z3, sm3, conv_w, conv_b, dtb_row, alog_row, dskip_row, g_row, tri, expand):
    bsz, seq_len, _ = xbc3.shape
    q = SSD_CHUNK
    blk = lambda b, c: (b, c, 0)
    const = lambda b, c: (0, 0)
    return pl.pallas_call(
        _ssd_kernel,
        out_shape=jax.ShapeDtypeStruct((bsz, seq_len, SSD_DIM), BF16),
        grid=(bsz, seq_len // q),
        in_specs=[pl.BlockSpec((1, q, SSD_CONV_DIM), blk),
                  pl.BlockSpec((1, q, SSD_DIM), blk),
                  pl.BlockSpec((1, q, SMALL_W), blk),
                  pl.BlockSpec((SSD_CONV, SSD_CONV_DIM), const),
                  pl.BlockSpec((1, SSD_CONV_DIM), const),
                  pl.BlockSpec((1, SMALL_W), const),
                  pl.BlockSpec((1, SMALL_W), const),
                  pl.BlockSpec((1, SSD_DIM), const),
                  pl.BlockSpec((1, SSD_DIM), const),
                  pl.BlockSpec((q, q), const),
                  pl.BlockSpec((SMALL_W, SSD_DIM), const)],
        out_specs=pl.BlockSpec((1, q, SSD_DIM), blk),
        scratch_shapes=[pltpu.VMEM((q + 8, SSD_CONV_DIM), F32),
                        pltpu.VMEM((SSD_GROUPS, SSD_STATE, SSD_GROUP_DIM), F32)],
        compiler_params=_cparams(("arbitrary", "arbitrary")),
        name="ssd",
    )(xbc3, z3, sm3, conv_w, conv_b, dtb_row, alog_row, dskip_row, g_row, tri, expand)


def _post_kernel(attn_ref, ssd_ref, x_ref, g1_ref, sh2_ref, sc2_ref, g2_ref, woa_ref, wos_ref, n2_ref,
                 wrt_ref, rb_ref, wgs_ref, wus_ref, wds_ref, upper_ref,
                 x1s_ref, h2_ref, eidx_ref, wts_ref, rank_ref, cnt_ref):
    i = pl.program_id(0)
    tm = x_ref.shape[0]

    @pl.when(i == 0)
    def _():
        cnt_ref[...] = jnp.zeros_like(cnt_ref)

    mixed = _dot(attn_ref[...], woa_ref[...]) + _dot(ssd_ref[...], wos_ref[...])
    x1 = x_ref[...] + g1_ref[0] * mixed
    var = jnp.mean(x1 * x1, axis=-1, keepdims=True)
    h2 = x1 * lax.rsqrt(var + EPS) * n2_ref[...]
    h2 = h2 * (1.0 + sc2_ref[0]) + sh2_ref[0]
    hb = h2.astype(BF16)
    h2_ref[...] = _pack_bf16_pair(h2)
    act = _silu(_dot(hb, wgs_ref[...])) * _dot(hb, wus_ref[...])
    shared = _dot(act.astype(BF16), wds_ref[...])
    x1s_ref[...] = x1 + g2_ref[0] * shared

    scores = _sigmoid(_dot_nt(wrt_ref[...], hb))
    biased = scores + rb_ref[...]
    grp = biased.reshape(N_ROUTE_GROUPS, GROUP_SIZE, tm)
    gi = lax.broadcasted_iota(I32, grp.shape, 1)
    m1 = jnp.max(grp, axis=1, keepdims=True)
    i1 = jnp.min(jnp.where(grp == m1, gi, GROUP_SIZE), axis=1, keepdims=True)
    m2 = jnp.max(jnp.where(gi == i1, -jnp.inf, grp), axis=1, keepdims=True)
    gsc = (m1 + m2).reshape(N_ROUTE_GROUPS, tm)
    gidx = lax.broadcasted_iota(I32, gsc.shape, 0)
    beaten = jnp.zeros(gsc.shape, I32)
    for o in range(N_ROUTE_GROUPS):
        other = gsc[o:o + 1, :]
        beats = (other > gsc) | ((other == gsc) & (gidx > o))
        beaten = beaten + beats.astype(I32)
    gmask = (beaten < TOPK_ROUTE_GROUPS).astype(F32)
    emask = jnp.broadcast_to(gmask.reshape(N_ROUTE_GROUPS, 1, tm), grp.shape).reshape(N_EXPERTS, tm)
    masked = jnp.where(emask > 0.5, biased, -jnp.inf)
    eiota = lax.broadcasted_iota(I32, (N_EXPERTS, tm), 0)
    idx_rows, w_rows = [], []
    sel = jnp.zeros((N_EXPERTS, tm), F32)
    for _ in range(TOP_K):
        mk = jnp.max(masked, axis=0, keepdims=True)
        ik = jnp.min(jnp.where(masked == mk, eiota, N_EXPERTS), axis=0, keepdims=True)
        hit = eiota == ik
        w_rows.append(jnp.sum(jnp.where(hit, scores, 0.0), axis=0, keepdims=True))
        idx_rows.append(ik)
        masked = jnp.where(hit, -jnp.inf, masked)
        sel = jnp.where(hit, 1.0, sel)
    w_all = jnp.concatenate(w_rows, axis=0)
    wts_ref[...] = w_all / jnp.sum(w_all, axis=0, keepdims=True) * ROUTED_SCALE
    eidx_ref[...] = jnp.concatenate(idx_rows, axis=0)
    before = cnt_ref[...][:, 0:1] + _dot(sel.astype(BF16), upper_ref[...])
    rank_rows = [jnp.sum(jnp.where(eiota == ik, before, 0.0), axis=0, keepdims=True) for ik in idx_rows]
    rank_ref[...] = jnp.concatenate(rank_rows, axis=0).astype(I32)
    cnt_ref[...] = cnt_ref[...] + jnp.sum(sel, axis=1, keepdims=True)


def _post_mixer(attn2, ssd2, x2, mod3, w_out_a, w_out_s, norm2_g, w_router_t, rb_col,
                wgs, wus, wds, upper, seq_len):
    t, d = x2.shape
    tm = POST_TM
    per_b = seq_len // tm
    row = lambda i: (i, 0)
    const = lambda i: (0, 0)
    modspec = lambda k: pl.BlockSpec((1, 1, d), lambda i: ((i // per_b) * N_MOD + k, 0, 0))
    slot = lambda i: (0, i)
    return pl.pallas_call(
        _post_kernel,
        out_shape=(jax.ShapeDtypeStruct((t, d), F32),
                   jax.ShapeDtypeStruct((t, d // 2), U32),
                   jax.ShapeDtypeStruct((TOP_K, t), I32),
                   jax.ShapeDtypeStruct((TOP_K, t), F32),
                   jax.ShapeDtypeStruct((TOP_K, t), I32),
                   jax.ShapeDtypeStruct((N_EXPERTS, LANES), F32)),
        grid=(t // tm,),
        in_specs=[pl.BlockSpec((tm, ATTN_DIM), row),
                  pl.BlockSpec((tm, SSD_DIM), row),
                  pl.BlockSpec((tm, d), row),
                  modspec(2), modspec(3), modspec(4), modspec(5),
                  pl.BlockSpec((ATTN_DIM, d), const),
                  pl.BlockSpec((SSD_DIM, d), const),
                  pl.BlockSpec((1, d), const),
                  pl.BlockSpec((N_EXPERTS, d), const),
                  pl.BlockSpec((N_EXPERTS, 1), const),
                  pl.BlockSpec((d, EXPERT_DIM), const),
                  pl.BlockSpec((d, EXPERT_DIM), const),
                  pl.BlockSpec((EXPERT_DIM, d), const),
                  pl.BlockSpec((tm, tm), const)],
        out_specs=(pl.BlockSpec((tm, d), row),
                   pl.BlockSpec((tm, d // 2), row),
                   pl.BlockSpec((TOP_K, tm), slot),
                   pl.BlockSpec((TOP_K, tm), slot),
                   pl.BlockSpec((TOP_K, tm), slot),
                   pl.BlockSpec((N_EXPERTS, LANES), const)),
        compiler_params=_cparams(("arbitrary",)),
        name="post_mixer",
    )(attn2, ssd2, x2, mod3, mod3, mod3, mod3, w_out_a, w_out_s, norm2_g.reshape(1, d),
      w_router_t, rb_col, wgs, wus, wds, upper)


def _dest_kernel(eidx_ref, rank_ref, pst_ref, dest_ref):
    tm = eidx_ref.shape[1]
    eiota = lax.broadcasted_iota(I32, (N_EXPERTS, tm), 0)
    rows = []
    for k in range(TOP_K):
        onehot = jnp.where(eiota == eidx_ref[k:k + 1, :], 1.0, 0.0).astype(BF16)
        r = _dot(pst_ref[...], onehot)
        rows.append(r[0:1, :] + r[1:2, :] + r[2:3, :])
    dest_ref[...] = jnp.concatenate(rows, axis=0).astype(I32) + rank_ref[...]


def _dest_rows(eidx, rank, pstart):
    k, t = eidx.shape
    tm = min(DEST_TM, t)
    pieces = jnp.stack([pstart & 0xFF0000, pstart & 0xFF00, pstart & 0xFF], axis=0)
    pst = jnp.zeros((8, N_EXPERTS), F32).at[:3].set(pieces.astype(F32)).astype(BF16)
    slot = lambda i: (0, i)
    return pl.pallas_call(
        _dest_kernel,
        out_shape=jax.ShapeDtypeStruct((k, t), I32),
        grid=(t // tm,),
        in_specs=[pl.BlockSpec((k, tm), slot), pl.BlockSpec((k, tm), slot),
                  pl.BlockSpec((8, N_EXPERTS), lambda i: (0, 0))],
        out_specs=pl.BlockSpec((k, tm), slot),
        compiler_params=_cparams(("arbitrary",)),
        name="dest_rows",
    )(eidx, rank, pst)


def _sc_mesh():
    return plsc.VectorSubcoreMesh(core_axis_name="c", subcore_axis_name="s")


def _sc_worker(n_workers_per_core):
    return lax.axis_index("s") * n_workers_per_core + lax.axis_index("c")


def _dispatch(dest, h2p, n_rows):
    t, w = h2p.shape
    mesh = _sc_mesh()
    n_workers = mesh.num_cores * mesh.num_subcores
    per_w = t // n_workers
    ch = min(SC_CHUNK, per_w)

    def body(dest_hbm, h_hbm, xs_hbm, idx_v, rows_v, sem):
        base_w = _sc_worker(mesh.num_cores) * per_w

        @pl.loop(0, per_w // ch)
        def _(ci):
            base = pl.multiple_of(base_w + ci * ch, ch)
            pltpu.sync_copy(dest_hbm.at[:, pl.ds(base, ch)], idx_v)
            pltpu.sync_copy(h_hbm.at[pl.ds(base, ch)], rows_v)
            copies = [pltpu.async_copy(rows_v, xs_hbm.at[idx_v.at[k]], sem) for k in range(TOP_K)]
            for cp in copies:
                cp.wait()

    return pl.kernel(
        body,
        out_type=jax.ShapeDtypeStruct((n_rows, w), U32),
        mesh=mesh,
        scratch_types=[pltpu.VMEM((TOP_K, ch), I32), pltpu.VMEM((ch, w), U32), pltpu.SemaphoreType.DMA],
        name="dispatch",
    )(dest, h2p)


def _undispatch(dest, ys, t0, t):
    w = ys.shape[1]
    mesh = _sc_mesh()
    n_workers = mesh.num_cores * mesh.num_subcores
    per_w = t // n_workers
    ch = min(SC_CHUNK, per_w)

    def body(dest_hbm, ys_hbm, ytok_hbm, idx_v, rows_v, sem):
        base_w = _sc_worker(mesh.num_cores) * per_w

        @pl.loop(0, per_w // ch)
        def _(ci):
            base = pl.multiple_of(base_w + ci * ch, ch)
            pltpu.sync_copy(dest_hbm.at[:, pl.ds(t0 + base, ch)], idx_v)
            for k in range(TOP_K):
                pltpu.async_copy(ys_hbm.at[idx_v.at[k]], rows_v, sem).wait()
                pltpu.sync_copy(rows_v, ytok_hbm.at[k, pl.ds(base, ch)])

    return pl.kernel(
        body,
        out_type=jax.ShapeDtypeStruct((TOP_K, t, w), U32),
        mesh=mesh,
        scratch_types=[pltpu.VMEM((TOP_K, ch), I32), pltpu.VMEM((ch, w), U32), pltpu.SemaphoreType.DMA],
        name="undispatch",
    )(dest, ys)


def _expert_kernel(be_ref, nu_ref, half_ref, x_ref, wg_ref, wu_ref, wd_ref, y_ref, act_ref):
    b = pl.program_id(0)
    n_used = nu_ref[0]
    hb = EXP_BLK // 2

    def gate_up(rows):
        x = _unpack_bf16_pair(x_ref[0:rows, :]).astype(BF16)
        act_ref[0:rows, :] = (_silu(_dot(x, wg_ref[0].astype(BF16)))
                              * _dot(x, wu_ref[0].astype(BF16))).astype(BF16)

    def down(rows):
        y_ref[0:rows, :] = _pack_bf16_pair(_dot(act_ref[0:rows, :], wd_ref[0].astype(BF16)))

    @pl.when(b == 0)
    def _():
        gate_up(EXP_BLK)

    interior = (b > 0) & (b < n_used)
    cur_half = half_ref[jnp.minimum(b, n_used - 1)] == 1
    prev_half = half_ref[jnp.clip(b - 1, 0, n_used - 1)] == 1
    for p_half in (False, True):
        for c_half in (False, True):
            @pl.when(interior & (prev_half == p_half) & (cur_half == c_half))
            def _(p_half=p_half, c_half=c_half):
                down(hb if p_half else EXP_BLK)
                gate_up(hb if c_half else EXP_BLK)

    @pl.when(b == n_used)
    def _():
        down(EXP_BLK)


def _experts(block_expert, n_used, block_half, xs, wg, wu, wd):
    n_rows, w = xs.shape
    n_blocks = n_rows // EXP_BLK
    d, f = wg.shape[1], wg.shape[2]
    cur = lambda b, be, nu, hf: (jnp.minimum(b, nu[0] - 1), 0)
    prev = lambda b, be, nu, hf: (jnp.clip(b - 1, 0, nu[0] - 1), 0)
    wcur = lambda b, be, nu, hf: (be[jnp.minimum(b, nu[0] - 1)], 0, 0)
    wprev = lambda b, be, nu, hf: (be[jnp.clip(b - 1, 0, nu[0] - 1)], 0, 0)
    grid_spec = pltpu.PrefetchScalarGridSpec(
        num_scalar_prefetch=3,
        grid=(n_blocks + 1,),
        in_specs=[pl.BlockSpec((EXP_BLK, w), cur),
                  pl.BlockSpec((1, d, f), wcur),
                  pl.BlockSpec((1, d, f), wcur),
                  pl.BlockSpec((1, f, d), wprev)],
        out_specs=pl.BlockSpec((EXP_BLK, w), prev),
        scratch_shapes=[pltpu.VMEM((EXP_BLK, f), BF16)],
    )
    return pl.pallas_call(
        _expert_kernel,
        out_shape=jax.ShapeDtypeStruct((n_rows, w), U32),
        grid_spec=grid_spec,
        compiler_params=_cparams(("arbitrary",)),
        name="experts",
    )(block_expert, n_used, block_half, xs, wg, wu, wd)


def _combine_kernel(ytok_ref, x1s_ref, w_ref, g2_ref, nf_ref, mf0_ref, mf1_ref, o_ref):
    w = w_ref[...]
    routed = w[:, 0:1] * _unpack_bf16_pair(ytok_ref[0])
    for k in range(1, TOP_K):
        routed = routed + w[:, k:k + 1] * _unpack_bf16_pair(ytok_ref[k])
    xo = x1s_ref[...] + g2_ref[0] * routed
    var = jnp.mean(xo * xo, axis=-1, keepdims=True)
    y = xo * lax.rsqrt(var + EPS) * nf_ref[...]
    o_ref[...] = y * (1.0 + mf1_ref[0]) + mf0_ref[0]


def _combine_alias_kernel(prev_ref, *refs):
    del prev_ref
    _combine_kernel(*refs)


def _combine(ytok, x1s, wts_t, mod3, normf_g, modf3, seq_len, part, out_prev):
    t, d = x1s.shape
    tm = COMB_TM
    steps = ytok.shape[1] // tm
    i0 = part * steps
    per_b = seq_len // tm
    row = lambda i: (i0 + i, 0)
    in_specs = [pl.BlockSpec((TOP_K, tm, d // 2), lambda i: (0, i, 0)),
                pl.BlockSpec((tm, d), row),
                pl.BlockSpec((tm, TOP_K), row),
                pl.BlockSpec((1, 1, d), lambda i: (((i0 + i) // per_b) * N_MOD + 5, 0, 0)),
                pl.BlockSpec((1, d), lambda i: (0, 0)),
                pl.BlockSpec((1, 1, d), lambda i: (((i0 + i) // per_b) * 2 + 0, 0, 0)),
                pl.BlockSpec((1, 1, d), lambda i: (((i0 + i) // per_b) * 2 + 1, 0, 0))]
    args = (ytok, x1s, wts_t, mod3, normf_g.reshape(1, d), modf3, modf3)
    if out_prev is None:
        body, aliases = _combine_kernel, {}
    else:
        body, aliases = _combine_alias_kernel, {0: 0}
        in_specs = [pl.BlockSpec(memory_space=pl.ANY)] + in_specs
        args = (out_prev,) + args
    return pl.pallas_call(
        body,
        out_shape=jax.ShapeDtypeStruct((t, d), F32),
        grid=(steps,),
        in_specs=in_specs,
        out_specs=pl.BlockSpec((tm, d), row),
        input_output_aliases=aliases,
        compiler_params=_cparams(("arbitrary",)),
        name="combine",
    )(*args)


def _lane_row(vec, lane0):
    return jnp.zeros((1, SMALL_W), F32).at[0, lane0:lane0 + vec.shape[0]].set(vec.astype(F32))


def _layer(x2, mod3, bsz, seq_len, norm1_g, w_in, fg_bias, conv_w, conv_b, dt_bias, a_log, d_skip,
           attn_norm_g, ssd_norm_g, w_out, norm2_g, w_router, router_bias,
           w_gate_e, w_up_e, w_down_e, w_gate_s, w_up_s, w_down_s):
    t, d = x2.shape
    o_q, o_k, o_v, o_fg = 0, ATTN_DIM, 2 * ATTN_DIM, 3 * ATTN_DIM
    o_z = o_fg + ATTN_HEADS
    o_xbc = o_z + SSD_DIM
    o_dt = o_xbc + SSD_CONV_DIM
    small = jnp.zeros((d, SMALL_W), F32)
    small = small.at[:, FG_LANE0:FG_LANE0 + ATTN_HEADS].set(w_in[:, o_fg:o_z])
    small = small.at[:, DT_LANE0:DT_LANE0 + SSD_HEADS].set(w_in[:, o_dt:o_dt + SSD_HEADS])
    w_cat = jnp.concatenate([w_in[:, o_q:o_fg], w_in[:, o_z:o_dt], small], axis=1).astype(BF16)

    tri_in = jnp.tril(jnp.ones((IN_SUB, IN_SUB), F32)).astype(BF16)
    qa, ka, va, z2, xbc2, sm2 = _in_proj(x2, mod3, norm1_g, w_cat, _lane_row(fg_bias, FG_LANE0), tri_in,
                                         bsz, seq_len)
    shp = lambda a: a.reshape(bsz, seq_len, a.shape[-1])
    attn3 = _fox_attn(qa, ka, va, attn_norm_g.reshape(1, ATTN_DIM).astype(F32))

    tri_chunk = jnp.tril(jnp.ones((SSD_CHUNK, SSD_CHUNK), F32)).astype(BF16)
    head_of_lane = jnp.arange(SSD_DIM, dtype=I32) // SSD_HEAD_DIM
    expand = (jnp.arange(SMALL_W, dtype=I32)[:, None] == head_of_lane[None, :] + DT_LANE0).astype(BF16)
    dskip_row = jnp.repeat(d_skip.astype(F32), SSD_HEAD_DIM).reshape(1, SSD_DIM)
    ssd3 = _ssd(shp(xbc2), shp(z2), shp(sm2), conv_w.astype(F32), conv_b.reshape(1, -1).astype(F32),
                _lane_row(dt_bias, DT_LANE0), _lane_row(a_log, DT_LANE0), dskip_row,
                ssd_norm_g.reshape(1, SSD_DIM).astype(F32), tri_chunk, expand)

    upper = jnp.triu(jnp.ones((POST_TM, POST_TM), F32), 1).astype(BF16)
    x1s, h2p, eidx, wts, rank, cnt = _post_mixer(
        attn3.reshape(t, ATTN_DIM), ssd3.reshape(t, SSD_DIM), x2, mod3,
        w_out[:ATTN_DIM].astype(BF16), w_out[ATTN_DIM:].astype(BF16), norm2_g,
        w_router.T.astype(BF16), router_bias.reshape(N_EXPERTS, 1).astype(F32),
        w_gate_s.astype(BF16), w_up_s.astype(BF16), w_down_s.astype(BF16), upper, seq_len)

    counts = cnt[:, 0].astype(I32)
    padded = (counts + EXP_BLK - 1) // EXP_BLK * EXP_BLK
    pend = jnp.cumsum(padded)
    pstart = pend - padded
    n_blocks = (t * TOP_K) // EXP_BLK + N_EXPERTS
    block_start = jnp.arange(n_blocks, dtype=I32) * EXP_BLK
    block_expert = jnp.minimum(jnp.sum((pend[None, :] <= block_start[:, None]).astype(I32), axis=1),
                               N_EXPERTS - 1)
    n_used = (pend[-1] // EXP_BLK).astype(I32).reshape(1)
    row_end = (pstart + counts)[block_expert]
    block_half = (row_end - block_start <= EXP_BLK // 2).astype(I32)
    dest = _dest_rows(eidx, rank, pstart)

    xs = _dispatch(dest, h2p, n_blocks * EXP_BLK)
    ys = _experts(block_expert, n_used, block_half, xs, w_gate_e, w_up_e, w_down_e)
    return dest, ys, x1s, wts


def kernel(x, c, norm1_g, w_ada, b_ada, w_in, fg_bias, conv_w, conv_b, dt_bias, a_log, d_skip, attn_norm_g,
           ssd_norm_g, w_out, norm2_g, w_router, router_bias, w_gate_e, w_up_e, w_down_e, w_gate_s, w_up_s,
           w_down_s, normf_g, w_ada_f, b_ada_f):
    bsz, seq_len, d = x.shape
    assert w_ada.shape[0] == 1, "single-layer kernel"
    t = bsz * seq_len
    mod3 = _modulation(c, w_ada[0], b_ada[0]).reshape(bsz * N_MOD, 1, d)
    modf3 = _modulation(c, w_ada_f, b_ada_f).reshape(bsz * 2, 1, d)
    x2 = x.reshape(t, d)
    dest, ys, x1s, wts = _layer(
        x2, mod3, bsz, seq_len, norm1_g[0], w_in[0], fg_bias[0], conv_w[0], conv_b[0], dt_bias[0], a_log[0],
        d_skip[0], attn_norm_g[0], ssd_norm_g[0], w_out[0], norm2_g[0], w_router[0], router_bias[0],
        w_gate_e[0], w_up_e[0], w_down_e[0], w_gate_s[0], w_up_s[0], w_down_s[0])
    wts_t = wts.T
    tp = t // COMB_PARTS
    out = None
    for p in range(COMB_PARTS):
        ytok = _undispatch(dest, ys, p * tp, tp)
        out = _combine(ytok, x1s, wts_t, mod3, normf_g, modf3, seq_len, p, out)
    return out.reshape(bsz, seq_len, d)
```

```python
import functools

import jax
import jax.numpy as jnp
from jax import lax
from jax.experimental import pallas as pl
from jax.experimental.pallas import tpu as pltpu
from jax.experimental.pallas import tpu_sc as plsc

F32 = jnp.float32
BF16 = jnp.bfloat16
I32 = jnp.int32
U32 = jnp.uint32

EPS = 1e-6
D_MODEL = 1024
N_MOD = 6

ATTN_HEADS = 8
ATTN_HEAD_DIM = 64
ATTN_DIM = ATTN_HEADS * ATTN_HEAD_DIM

SSD_HEADS = 8
SSD_HEAD_DIM = 64
SSD_DIM = SSD_HEADS * SSD_HEAD_DIM
SSD_GROUPS = 2
SSD_STATE = 128
SSD_CONV = 4
SSD_CHUNK = 128
SSD_CONV_DIM = SSD_DIM + 2 * SSD_GROUPS * SSD_STATE
SSD_GROUP_DIM = SSD_DIM // SSD_GROUPS

N_EXPERTS = 256
TOP_K = 8
N_ROUTE_GROUPS = 8
TOPK_ROUTE_GROUPS = 4
GROUP_SIZE = N_EXPERTS // N_ROUTE_GROUPS
EXPERT_DIM = 256
ROUTED_SCALE = 2.5

LANES = 128
SMALL_W = LANES
FG_LANE0 = 0
DT_LANE0 = 8
AUG_W = LANES

IN_TM = 1024
IN_SUB = 512
ATT_TQ = 512
ATT_TK = 512
ATT_MASK_BLK = 128
ATT_MASK_LANES = 32
ATT_HEADS_PER_STEP = 8
POST_TM = 512
SC_CHUNK = 128
EXP_BLK = 1024
COMB_TM = 256
COMB_PARTS = 16
VMEM_LIMIT = 56 * 1024 * 1024
NEG_BIG = -1e30
LOG2E = 1.4426950408889634
DEST_TM = 2048


def _split3(x):
    hi = x.astype(BF16)
    r1 = x - hi.astype(F32)
    mid = r1.astype(BF16)
    lo = (r1 - mid.astype(F32)).astype(BF16)
    return hi, mid, lo


def _dot(a, b):
    return jnp.dot(a, b, preferred_element_type=F32)


def _dot_nt(a, b):
    return lax.dot_general(a, b, (((1,), (1,)), ((), ())), preferred_element_type=F32)


def _dot_exact_lhs01(lhs_bf16, x, pieces=3):
    parts = _split3(x)[:pieces]
    out = _dot(lhs_bf16, parts[0])
    for p in parts[1:]:
        out = out + _dot(lhs_bf16, p)
    return out


def _dot_exact_rhs01(x, rhs_bf16, pieces=2):
    parts = _split3(x)[:pieces]
    out = _dot(parts[0], rhs_bf16)
    for p in parts[1:]:
        out = out + _dot(p, rhs_bf16)
    return out


def _sigmoid(x):
    return 1.0 / (1.0 + jnp.exp(-x))


def _silu(x):
    return x * _sigmoid(x)


def _softplus(x):
    return jnp.maximum(x, 0.0) + jnp.log(1.0 + jnp.exp(-jnp.abs(x)))


def _log_sigmoid(x):
    return jnp.minimum(x, 0.0) - jnp.log(1.0 + jnp.exp(-jnp.abs(x)))


def _pack_bf16_pair(x):
    n = x.shape[1] // 2
    lo = pltpu.bitcast(x[:, :n].astype(BF16).astype(F32), U32)
    hi = pltpu.bitcast(x[:, n:].astype(BF16).astype(F32), U32)
    return (hi & jnp.uint32(0xFFFF0000)) | (lo >> 16)


def _unpack_bf16_pair(w):
    lo = pltpu.bitcast(w << 16, F32)
    hi = pltpu.bitcast(w & jnp.uint32(0xFFFF0000), F32)
    return jnp.concatenate([lo, hi], axis=1)


def _cparams(sem):
    return pltpu.CompilerParams(dimension_semantics=sem, vmem_limit_bytes=VMEM_LIMIT)


def _mod_kernel(c_ref, w_ref, b_ref, o_ref):
    c = c_ref[...]
    o_ref[...] = jnp.dot(_silu(c), w_ref[...], preferred_element_type=F32,
                         precision=lax.Precision.HIGHEST) + b_ref[...]


def _modulation(c, w, b):
    bsz, d = c.shape
    n = w.shape[1]
    tn = 1024
    return pl.pallas_call(
        _mod_kernel,
        out_shape=jax.ShapeDtypeStruct((bsz, n), F32),
        grid=(n // tn,),
        in_specs=[pl.BlockSpec((bsz, d), lambda j: (0, 0)),
                  pl.BlockSpec((d, tn), lambda j: (0, j)),
                  pl.BlockSpec((1, tn), lambda j: (0, j))],
        out_specs=pl.BlockSpec((bsz, tn), lambda j: (0, j)),
        compiler_params=_cparams(("arbitrary",)),
        name="modulation",
    )(c, w, b.reshape(1, n))


_COL_Q, _COL_K, _COL_V, _COL_Z, _COL_XBC, _COL_SM, _COL_END = 0, 512, 1024, 1536, 2048, 3072, 3200


def _inproj_kernel(per_b, x_ref, g_ref, sc_ref, sh_ref, w_ref, fgb_ref, tri_ref,
                   qa_ref, ka_ref, va_ref, z_ref, xbc_ref, sm_ref, carry_ref):
    @pl.when(pl.program_id(0) % per_b == 0)
    def _():
        carry_ref[...] = jnp.zeros_like(carry_ref)

    carry = carry_ref[...]
    for r0 in range(0, x_ref.shape[0], IN_SUB):
        pos0 = (pl.program_id(0) % per_b) * x_ref.shape[0] + r0
        carry = _inproj_rows(r0, pos0, carry, x_ref, g_ref, sc_ref, sh_ref, w_ref, fgb_ref, tri_ref,
                             qa_ref, ka_ref, va_ref, z_ref, xbc_ref, sm_ref)
    carry_ref[...] = carry


def _inproj_rows(r0, pos0, carry, x_ref, g_ref, sc_ref, sh_ref, w_ref, fgb_ref, tri_ref,
                 qa_ref, ka_ref, va_ref, z_ref, xbc_ref, sm_ref):
    tm = IN_SUB
    rows = slice(r0, r0 + tm)
    x = x_ref[rows, :]
    var = jnp.mean(x * x, axis=-1, keepdims=True)
    h = x * lax.rsqrt(var + EPS) * g_ref[...]
    h = h * (1.0 + sc_ref[0]) + sh_ref[0]
    hb = h.astype(BF16)
    sm = _dot(hb, w_ref[:, _COL_SM:_COL_END])
    sm_ref[rows, :] = sm
    log_f = _log_sigmoid(sm + fgb_ref[...]) * LOG2E
    cum = _dot_exact_lhs01(tri_ref[...], log_f) + carry
    hi, mid, lo = (p.astype(F32) for p in _split3(cum))
    nb = ATT_MASK_BLK
    lane1 = lax.broadcasted_iota(I32, (1, AUG_W), 1)
    left1 = lane1 < ATTN_HEAD_DIM

    def block_lanes(base, blk):
        rel = lane1 - (base + 6)
        in_range = (rel >= 0) & (rel < ATT_MASK_LANES)
        return jnp.where(in_range & (rel == blk), 1.0, 0.0), jnp.where(in_range & (rel < blk), NEG_BIG, 0.0)

    def head_rows(out_ref, col0, scale, book_of_head):
        f = _dot(hb, w_ref[:, col0:col0 + ATTN_DIM])
        if scale is not None:
            f = f * scale
        for a in range(tm // nb):
            band = slice(a * nb, (a + 1) * nb)
            blk = pos0 // nb + a
            for pr in range(ATTN_HEADS // 2):
                pair = f[band, pr * AUG_W:(pr + 1) * AUG_W]
                dst = slice(r0 + a * nb, r0 + (a + 1) * nb)
                out_ref[0, 2 * pr, dst, :] = jnp.where(
                    left1, pair, book_of_head(2 * pr, ATTN_HEAD_DIM, band, blk)).astype(BF16)
                out_ref[0, 2 * pr + 1, dst, :] = jnp.where(
                    left1, book_of_head(2 * pr + 1, 0, band, blk), pair).astype(BF16)

    def cols(hd, band):
        c0 = FG_LANE0 + hd
        return hi[band, c0:c0 + 1], mid[band, c0:c0 + 1], lo[band, c0:c0 + 1]

    def book_q(hd, base, band, blk):
        chi, cmid, clo = cols(hd, band)
        rel = lane1 - base
        return jnp.where(rel == 0, chi, jnp.where(rel == 1, cmid, jnp.where(rel == 2, clo,
                         jnp.where((rel >= 3) & (rel < 6), 1.0, block_lanes(base, blk)[0]))))

    def book_k(hd, base, band, blk):
        chi, cmid, clo = cols(hd, band)
        rel = lane1 - base
        return jnp.where(rel == 3, -chi, jnp.where(rel == 4, -cmid, jnp.where(rel == 5, -clo,
                         jnp.where((rel >= 0) & (rel < 3), 1.0, block_lanes(base, blk)[1]))))

    def book_v(hd, base, band, blk):
        return jnp.where(lane1 == base, 1.0, 0.0)

    head_rows(qa_ref, _COL_Q, ATTN_HEAD_DIM ** -0.5 * LOG2E, book_q)
    head_rows(ka_ref, _COL_K, None, book_k)
    head_rows(va_ref, _COL_V, None, book_v)
    z_ref[rows, :] = _dot(hb, w_ref[:, _COL_Z:_COL_XBC]).astype(BF16)
    xbc_ref[rows, :512] = _dot(hb, w_ref[:, _COL_XBC:_COL_XBC + 512]).astype(BF16)
    xbc_ref[rows, 512:] = _dot(hb, w_ref[:, _COL_XBC + 512:_COL_SM]).astype(BF16)
    return cum[tm - 1:tm, :]


def _in_proj(x2, mod3, norm_g, w_cat, fgb_row, tri, bsz, seq_len):
    assert seq_len // ATT_MASK_BLK <= ATT_MASK_LANES, "position blocks must fit the bookkeeping lanes"
    t, d = x2.shape
    tm = IN_TM
    per_b = seq_len // tm
    row = lambda i: (i, 0)
    const = lambda i: (0, 0)
    aug = jax.ShapeDtypeStruct((bsz, ATTN_HEADS, seq_len, AUG_W), BF16)
    aug_spec = pl.BlockSpec((1, ATTN_HEADS, tm, AUG_W), lambda i: (i // per_b, 0, i % per_b, 0))
    return pl.pallas_call(
        functools.partial(_inproj_kernel, per_b),
        out_shape=(aug, aug, aug,
                   jax.ShapeDtypeStruct((t, SSD_DIM), BF16),
                   jax.ShapeDtypeStruct((t, SSD_CONV_DIM), BF16),
                   jax.ShapeDtypeStruct((t, SMALL_W), F32)),
        grid=(t // tm,),
        in_specs=[pl.BlockSpec((tm, d), row),
                  pl.BlockSpec((1, d), const),
                  pl.BlockSpec((1, 1, d), lambda i: ((i // per_b) * N_MOD + 1, 0, 0)),
                  pl.BlockSpec((1, 1, d), lambda i: ((i // per_b) * N_MOD + 0, 0, 0)),
                  pl.BlockSpec((d, _COL_END), const),
                  pl.BlockSpec((1, SMALL_W), const),
                  pl.BlockSpec((IN_SUB, IN_SUB), const)],
        out_specs=(aug_spec, aug_spec, aug_spec,
                   pl.BlockSpec((tm, SSD_DIM), row),
                   pl.BlockSpec((tm, SSD_CONV_DIM), row),
                   pl.BlockSpec((tm, SMALL_W), row)),
        scratch_shapes=[pltpu.VMEM((1, SMALL_W), F32)],
        compiler_params=_cparams(("arbitrary",)),
        name="in_proj",
    )(x2, norm_g.reshape(1, d), mod3, mod3, w_cat, fgb_row, tri)


def _attn_kernel(qa_ref, ka_ref, va_ref, g_ref, o_ref):
    i = pl.program_id(2)
    tq, tk = ATT_TQ, ATT_TK
    nh = ATT_HEADS_PER_STEP
    sub = ATT_MASK_BLK
    diff = lax.broadcasted_iota(I32, (sub, sub), 0) - lax.broadcasted_iota(I32, (sub, sub), 1)

    def fine_mask(s, j):
        keep = diff >= (j - i) * tk
        bands = []
        for a in range(tq // sub):
            band = s[a * sub:(a + 1) * sub, :]
            parts = [band[:, :a * sub]] if a else []
            parts.append(jnp.where(keep, band[:, a * sub:(a + 1) * sub], NEG_BIG))
            if (a + 1) * sub < tk:
                parts.append(band[:, (a + 1) * sub:])
            bands.append(jnp.concatenate(parts, axis=1))
        return jnp.concatenate(bands, axis=0)

    def scores(hh, j):
        off = pl.multiple_of(j * tk, tk)
        return _dot_nt(qa_ref[0, hh], ka_ref[0, hh, pl.ds(off, tk), :])

    def update(hh, state, s, j):
        m_old, acc = state
        off = pl.multiple_of(j * tk, tk)
        s = fine_mask(s, j)
        m_new = jnp.maximum(m_old, jnp.max(s, axis=-1, keepdims=True))
        p = jnp.exp2(s - m_new).astype(BF16)
        acc = jnp.exp2(m_old - m_new) * acc + _dot(p, va_ref[0, hh, pl.ds(off, tk), :])
        return m_new, acc

    def body(j, states):
        ss = [scores(hh, j) for hh in range(nh)]
        return tuple(update(hh, states[hh], ss[hh], j) for hh in range(nh))

    init = tuple((jnp.full((tq, 1), NEG_BIG, F32), jnp.zeros((tq, LANES), F32)) for _ in range(nh))
    states = lax.fori_loop(0, i + 1, body, init)
    lane = lax.broadcasted_iota(I32, (tq, LANES), 1)
    first = lane < ATTN_HEAD_DIM
    for pr in range(nh // 2):
        acc_e, acc_o = states[2 * pr][1], states[2 * pr + 1][1]
        sum_e = jnp.sum(jnp.where(lane == ATTN_HEAD_DIM, acc_e, 0.0), axis=-1, keepdims=True)
        sum_o = jnp.sum(jnp.where(lane == 0, acc_o, 0.0), axis=-1, keepdims=True)
        o = jnp.where(first, acc_e / sum_e, acc_o / sum_o)
        sq = o * o
        s_all = jnp.sum(sq, axis=-1, keepdims=True)
        s0 = jnp.sum(jnp.where(first, sq, 0.0), axis=-1, keepdims=True)
        ms = jnp.where(first, s0, s_all - s0) * (1.0 / ATTN_HEAD_DIM)
        gsl = slice(pr * LANES, (pr + 1) * LANES)
        o_ref[0, :, gsl] = (o * lax.rsqrt(ms + EPS) * g_ref[:, gsl]).astype(BF16)


def _fox_attn(qa, ka, va, g_row):
    bsz, _, seq_len, _ = qa.shape
    tq = ATT_TQ
    nh = ATT_HEADS_PER_STEP
    wo = nh // 2 * LANES
    kv_spec = pl.BlockSpec((1, nh, seq_len, AUG_W), lambda b, p, i: (b, p, 0, 0))
    return pl.pallas_call(
        _attn_kernel,
        out_shape=jax.ShapeDtypeStruct((bsz, seq_len, ATTN_DIM), BF16),
        grid=(bsz, ATTN_HEADS // nh, seq_len // tq),
        in_specs=[pl.BlockSpec((1, nh, tq, AUG_W), lambda b, p, i: (b, p, i, 0)),
                  kv_spec, kv_spec,
                  pl.BlockSpec((1, wo), lambda b, p, i: (0, p))],
        out_specs=pl.BlockSpec((1, tq, wo), lambda b, p, i: (b, i, p)),
        compiler_params=_cparams(("arbitrary", "arbitrary", "arbitrary")),
        name="fox_attn",
    )(qa, ka, va, g_row)


def _ssd_kernel(xbc_ref, z_ref, sm_ref, cw_ref, cb_ref, dtb_ref, alog_ref, dsk_ref, g_ref, tri_ref, exp_ref,
                o_ref, ext_ref, state_ref):
    c = pl.program_id(1)
    q = SSD_CHUNK

    @pl.when(c == 0)
    def _():
        ext_ref[0:8, :] = jnp.zeros((8, SSD_CONV_DIM), F32)
        state_ref[...] = jnp.zeros_like(state_ref)

    ext_ref[8:8 + q, :] = xbc_ref[0].astype(F32)
    conv = cb_ref[...] + cw_ref[0:1, :] * ext_ref[5:5 + q, :]
    for j in range(1, SSD_CONV):
        conv = conv + cw_ref[j:j + 1, :] * ext_ref[5 + j:5 + j + q, :]
    ext_ref[0:8, :] = ext_ref[q:q + 8, :]
    xc = _silu(conv)
    xs = xc[:, :SSD_DIM]

    dt = _softplus(sm_ref[0] + dtb_ref[...])
    a_dt = -jnp.exp(alog_ref[...]) * dt
    a_cs = _dot_exact_lhs01(tri_ref[...], a_dt)
    a_last = a_cs[q - 1:q, :]
    e_cs = jnp.exp(a_cs)
    dec = jnp.exp(a_last - a_cs)
    a_cs_t = a_cs.T
    expand = exp_ref[...]
    dt_x = _dot_exact_rhs01(dt, expand)
    e_x = _dot_exact_rhs01(e_cs, expand)
    dec_x = _dot_exact_rhs01(dec, expand)
    x_dt = xs * dt_x
    x_dec = (x_dt * dec_x).astype(BF16)
    x_dt_b = x_dt.astype(BF16)

    row = lax.broadcasted_iota(I32, (q, q), 0)
    col = lax.broadcasted_iota(I32, (q, q), 1)
    lower = row >= col
    lane = lax.broadcasted_iota(I32, (q, LANES), 1)
    first = lane < SSD_HEAD_DIM
    y_parts = []
    for g in range(SSD_GROUPS):
        b_g = xc[:, SSD_DIM + g * SSD_STATE:SSD_DIM + (g + 1) * SSD_STATE]
        c_g = xc[:, SSD_DIM + (SSD_GROUPS + g) * SSD_STATE:SSD_DIM + (SSD_GROUPS + g + 1) * SSD_STATE]
        c_gb = c_g.astype(BF16)
        cb = _dot_nt(c_gb, b_g.astype(BF16))
        gs = slice(g * SSD_GROUP_DIM, (g + 1) * SSD_GROUP_DIM)
        st_prev = state_ref[g]
        y_off = _dot(c_gb, st_prev.astype(BF16)) * e_x[:, gs]
        s_new = _dot(b_g.T.astype(BF16), x_dec[:, gs])
        state_ref[g] = st_prev * e_x[q - 1:q, gs] + s_new
        for pr in range(2):
            pair = []
            for hh in range(2):
                h = g * 4 + pr * 2 + hh
                a_col = a_cs[:, DT_LANE0 + h:DT_LANE0 + h + 1]
                a_row = a_cs_t[DT_LANE0 + h:DT_LANE0 + h + 1, :]
                lmat = jnp.where(lower, jnp.exp(jnp.minimum(a_col - a_row, 0.0)), 0.0)
                m_h = (cb * lmat).astype(BF16)
                ps = slice((g * 2 + pr) * LANES, (g * 2 + pr + 1) * LANES)
                pair.append(_dot(m_h, x_dt_b[:, ps]))
            y_parts.append(jnp.where(first, pair[0], pair[1]) + y_off[:, pr * LANES:(pr + 1) * LANES])
    y = jnp.concatenate(y_parts, axis=1) + dsk_ref[...] * xs
    y = y * _silu(z_ref[0].astype(F32))
    outs = []
    for g in range(SSD_GROUPS):
        yg = y[:, g * SSD_GROUP_DIM:(g + 1) * SSD_GROUP_DIM]
        ms = jnp.mean(yg * yg, axis=-1, keepdims=True)
        outs.append(yg * lax.rsqrt(ms + EPS))
    o_ref[0] = (jnp.concatenate(outs, axis=1) * g_ref[...]).astype(BF16)


def _ssd(xbc3, z3, sm3, conv_w, conv_b, dtb_row, alog_row, dskip_row, g_row, tri, expand):
    bsz, seq_len, _ = xbc3.shape
    q = SSD_CHUNK
    blk = lambda b, c: (b, c, 0)
    const = lambda b, c: (0, 0)
    return pl.pallas_call(
        _ssd_kernel,
        out_shape=jax.ShapeDtypeStruct((bsz, seq_len, SSD_DIM), BF16),
        grid=(bsz, seq_len // q),
        in_specs=[pl.BlockSpec((1, q, SSD_CONV_DIM), blk),
                  pl.BlockSpec((1, q, SSD_DIM), blk),
                  pl.BlockSpec((1, q, SMALL_W), blk),
                  pl.BlockSpec((SSD_CONV, SSD_CONV_DIM), const),
                  pl.BlockSpec((1, SSD_CONV_DIM), const),
                  pl.BlockSpec((1, SMALL_W), const),
                  pl.BlockSpec((1, SMALL_W), const),
                  pl.BlockSpec((1, SSD_DIM), const),
                  pl.BlockSpec((1, SSD_DIM), const),
                  pl.BlockSpec((q, q), const),
                  pl.BlockSpec((SMALL_W, SSD_DIM), const)],
        out_specs=pl.BlockSpec((1, q, SSD_DIM), blk),
        scratch_shapes=[pltpu.VMEM((q + 8, SSD_CONV_DIM), F32),
                        pltpu.VMEM((SSD_GROUPS, SSD_STATE, SSD_GROUP_DIM), F32)],
        compiler_params=_cparams(("arbitrary", "arbitrary")),
        name="ssd",
    )(xbc3, z3, sm3, conv_w, conv_b, dtb_row, alog_row, dskip_row, g_row, tri, expand)


def _post_kernel(attn_ref, ssd_ref, x_ref, g1_ref, sh2_ref, sc2_ref, g2_ref, woa_ref, wos_ref, n2_ref,
                 wrt_ref, rb_ref, wgs_ref, wus_ref, wds_ref, upper_ref,
                 x1s_ref, h2_ref, eidx_ref, wts_ref, rank_ref, cnt_ref):
    i = pl.program_id(0)
    tm = x_ref.shape[0]

    @pl.when(i == 0)
    def _():
        cnt_ref[...] = jnp.zeros_like(cnt_ref)

    mixed = _dot(attn_ref[...], woa_ref[...]) + _dot(ssd_ref[...], wos_ref[...])
    x1 = x_ref[...] + g1_ref[0] * mixed
    var = jnp.mean(x1 * x1, axis=-1, keepdims=True)
    h2 = x1 * lax.rsqrt(var + EPS) * n2_ref[...]
    h2 = h2 * (1.0 + sc2_ref[0]) + sh2_ref[0]
    hb = h2.astype(BF16)
    h2_ref[...] = _pack_bf16_pair(h2)
    act = _silu(_dot(hb, wgs_ref[...])) * _dot(hb, wus_ref[...])
    shared = _dot(act.astype(BF16), wds_ref[...])
    x1s_ref[...] = x1 + g2_ref[0] * shared

    scores = _sigmoid(_dot_nt(wrt_ref[...], hb))
    biased = scores + rb_ref[...]
    grp = biased.reshape(N_ROUTE_GROUPS, GROUP_SIZE, tm)
    gi = lax.broadcasted_iota(I32, grp.shape, 1)
    m1 = jnp.max(grp, axis=1, keepdims=True)
    i1 = jnp.min(jnp.where(grp == m1, gi, GROUP_SIZE), axis=1, keepdims=True)
    m2 = jnp.max(jnp.where(gi == i1, -jnp.inf, grp), axis=1, keepdims=True)
    gsc = (m1 + m2).reshape(N_ROUTE_GROUPS, tm)
    gidx = lax.broadcasted_iota(I32, gsc.shape, 0)
    beaten = jnp.zeros(gsc.shape, I32)
    for o in range(N_ROUTE_GROUPS):
        other = gsc[o:o + 1, :]
        beats = (other > gsc) | ((other == gsc) & (gidx > o))
        beaten = beaten + beats.astype(I32)
    gmask = (beaten < TOPK_ROUTE_GROUPS).astype(F32)
    emask = jnp.broadcast_to(gmask.reshape(N_ROUTE_GROUPS, 1, tm), grp.shape).reshape(N_EXPERTS, tm)
    masked = jnp.where(emask > 0.5, biased, -jnp.inf)
    eiota = lax.broadcasted_iota(I32, (N_EXPERTS, tm), 0)
    idx_rows, w_rows = [], []
    sel = jnp.zeros((N_EXPERTS, tm), F32)
    for _ in range(TOP_K):
        mk = jnp.max(masked, axis=0, keepdims=True)
        ik = jnp.min(jnp.where(masked == mk, eiota, N_EXPERTS), axis=0, keepdims=True)
        hit = eiota == ik
        w_rows.append(jnp.sum(jnp.where(hit, scores, 0.0), axis=0, keepdims=True))
        idx_rows.append(ik)
        masked = jnp.where(hit, -jnp.inf, masked)
        sel = jnp.where(hit, 1.0, sel)
    w_all = jnp.concatenate(w_rows, axis=0)
    wts_ref[...] = w_all / jnp.sum(w_all, axis=0, keepdims=True) * ROUTED_SCALE
    eidx_ref[...] = jnp.concatenate(idx_rows, axis=0)
    before = cnt_ref[...][:, 0:1] + _dot(sel.astype(BF16), upper_ref[...])
    rank_rows = [jnp.sum(jnp.where(eiota == ik, before, 0.0), axis=0, keepdims=True) for ik in idx_rows]
    rank_ref[...] = jnp.concatenate(rank_rows, axis=0).astype(I32)
    cnt_ref[...] = cnt_ref[...] + jnp.sum(sel, axis=1, keepdims=True)


def _post_mixer(attn2, ssd2, x2, mod3, w_out_a, w_out_s, norm2_g, w_router_t, rb_col,
                wgs, wus, wds, upper, seq_len):
    t, d = x2.shape
    tm = POST_TM
    per_b = seq_len // tm
    row = lambda i: (i, 0)
    const = lambda i: (0, 0)
    modspec = lambda k: pl.BlockSpec((1, 1, d), lambda i: ((i // per_b) * N_MOD + k, 0, 0))
    slot = lambda i: (0, i)
    return pl.pallas_call(
        _post_kernel,
        out_shape=(jax.ShapeDtypeStruct((t, d), F32),
                   jax.ShapeDtypeStruct((t, d // 2), U32),
                   jax.ShapeDtypeStruct((TOP_K, t), I32),
                   jax.ShapeDtypeStruct((TOP_K, t), F32),
                   jax.ShapeDtypeStruct((TOP_K, t), I32),
                   jax.ShapeDtypeStruct((N_EXPERTS, LANES), F32)),
        grid=(t // tm,),
        in_specs=[pl.BlockSpec((tm, ATTN_DIM), row),
                  pl.BlockSpec((tm, SSD_DIM), row),
                  pl.BlockSpec((tm, d), row),
                  modspec(2), modspec(3), modspec(4), modspec(5),
                  pl.BlockSpec((ATTN_DIM, d), const),
                  pl.BlockSpec((SSD_DIM, d), const),
                  pl.BlockSpec((1, d), const),
                  pl.BlockSpec((N_EXPERTS, d), const),
                  pl.BlockSpec((N_EXPERTS, 1), const),
                  pl.BlockSpec((d, EXPERT_DIM), const),
                  pl.BlockSpec((d, EXPERT_DIM), const),
                  pl.BlockSpec((EXPERT_DIM, d), const),
                  pl.BlockSpec((tm, tm), const)],
        out_specs=(pl.BlockSpec((tm, d), row),
                   pl.BlockSpec((tm, d // 2), row),
                   pl.BlockSpec((TOP_K, tm), slot),
                   pl.BlockSpec((TOP_K, tm), slot),
                   pl.BlockSpec((TOP_K, tm), slot),
                   pl.BlockSpec((N_EXPERTS, LANES), const)),
        compiler_params=_cparams(("arbitrary",)),
        name="post_mixer",
    )(attn2, ssd2, x2, mod3, mod3, mod3, mod3, w_out_a, w_out_s, norm2_g.reshape(1, d),
      w_router_t, rb_col, wgs, wus, wds, upper)


def _dest_kernel(eidx_ref, rank_ref, pst_ref, dest_ref):
    tm = eidx_ref.shape[1]
    eiota = lax.broadcasted_iota(I32, (N_EXPERTS, tm), 0)
    rows = []
    for k in range(TOP_K):
        onehot = jnp.where(eiota == eidx_ref[k:k + 1, :], 1.0, 0.0).astype(BF16)
        r = _dot(pst_ref[...], onehot)
        rows.append(r[0:1, :] + r[1:2, :] + r[2:3, :])
    dest_ref[...] = jnp.concatenate(rows, axis=0).astype(I32) + rank_ref[...]


def _dest_rows(eidx, rank, pstart):
    k, t = eidx.shape
    tm = min(DEST_TM, t)
    pieces = jnp.stack([pstart & 0xFF0000, pstart & 0xFF00, pstart & 0xFF], axis=0)
    pst = jnp.zeros((8, N_EXPERTS), F32).at[:3].set(pieces.astype(F32)).astype(BF16)
    slot = lambda i: (0, i)
    return pl.pallas_call(
        _dest_kernel,
        out_shape=jax.ShapeDtypeStruct((k, t), I32),
        grid=(t // tm,),
        in_specs=[pl.BlockSpec((k, tm), slot), pl.BlockSpec((k, tm), slot),
                  pl.BlockSpec((8, N_EXPERTS), lambda i: (0, 0))],
        out_specs=pl.BlockSpec((k, tm), slot),
        compiler_params=_cparams(("arbitrary",)),
        name="dest_rows",
    )(eidx, rank, pst)


def _sc_mesh():
    return plsc.VectorSubcoreMesh(core_axis_name="c", subcore_axis_name="s")


def _sc_worker(n_workers_per_core):
    return lax.axis_index("s") * n_workers_per_core + lax.axis_index("c")


def _dispatch(dest, h2p, n_rows):
    t, w = h2p.shape
    mesh = _sc_mesh()
    n_workers = mesh.num_cores * mesh.num_subcores
    per_w = t // n_workers
    ch = min(SC_CHUNK, per_w)

    def body(dest_hbm, h_hbm, xs_hbm, idx_v, rows_v, sem):
        base_w = _sc_worker(mesh.num_cores) * per_w

        @pl.loop(0, per_w // ch)
        def _(ci):
            base = pl.multiple_of(base_w + ci * ch, ch)
            pltpu.sync_copy(dest_hbm.at[:, pl.ds(base, ch)], idx_v)
            pltpu.sync_copy(h_hbm.at[pl.ds(base, ch)], rows_v)
            copies = [pltpu.async_copy(rows_v, xs_hbm.at[idx_v.at[k]], sem) for k in range(TOP_K)]
            for cp in copies:
                cp.wait()

    return pl.kernel(
        body,
        out_type=jax.ShapeDtypeStruct((n_rows, w), U32),
        mesh=mesh,
        scratch_types=[pltpu.VMEM((TOP_K, ch), I32), pltpu.VMEM((ch, w), U32), pltpu.SemaphoreType.DMA],
        name="dispatch",
    )(dest, h2p)


def _undispatch(dest, ys, t0, t):
    w = ys.shape[1]
    mesh = _sc_mesh()
    n_workers = mesh.num_cores * mesh.num_subcores
    per_w = t // n_workers
    ch = min(SC_CHUNK, per_w)

    def body(dest_hbm, ys_hbm, ytok_hbm, idx_v, rows_v, sem):
        base_w = _sc_worker(mesh.num_cores) * per_w

        @pl.loop(0, per_w // ch)
        def _(ci):
            base = pl.multiple_of(base_w + ci * ch, ch)
            pltpu.sync_copy(dest_hbm.at[:, pl.ds(t0 + base, ch)], idx_v)
            for k in range(TOP_K):
                pltpu.async_copy(ys_hbm.at[idx_v.at[k]], rows_v, sem).wait()
                pltpu.sync_copy(rows_v, ytok_hbm.at[k, pl.ds(base, ch)])

    return pl.kernel(
        body,
        out_type=jax.ShapeDtypeStruct((TOP_K, t, w), U32),
        mesh=mesh,
        scratch_types=[pltpu.VMEM((TOP_K, ch), I32), pltpu.VMEM((ch, w), U32), pltpu.SemaphoreType.DMA],
        name="undispatch",
    )(dest, ys)


def _expert_kernel(be_ref, nu_ref, half_ref, x_ref, wg_ref, wu_ref, wd_ref, y_ref, act_ref):
    b = pl.program_id(0)
    n_used = nu_ref[0]
    hb = EXP_BLK // 2

    def gate_up(rows):
        x = _unpack_bf16_pair(x_ref[0:rows, :]).astype(BF16)
        act_ref[0:rows, :] = (_silu(_dot(x, wg_ref[0].astype(BF16)))
                              * _dot(x, wu_ref[0].astype(BF16))).astype(BF16)

    def down(rows):
        y_ref[0:rows, :] = _pack_bf16_pair(_dot(act_ref[0:rows, :], wd_ref[0].astype(BF16)))

    @pl.when(b == 0)
    def _():
        gate_up(EXP_BLK)

    interior = (b > 0) & (b < n_used)
    cur_half = half_ref[jnp.minimum(b, n_used - 1)] == 1
    prev_half = half_ref[jnp.clip(b - 1, 0, n_used - 1)] == 1
    for p_half in (False, True):
        for c_half in (False, True):
            @pl.when(interior & (prev_half == p_half) & (cur_half == c_half))
            def _(p_half=p_half, c_half=c_half):
                down(hb if p_half else EXP_BLK)
                gate_up(hb if c_half else EXP_BLK)

    @pl.when(b == n_used)
    def _():
        down(EXP_BLK)


def _experts(block_expert, n_used, block_half, xs, wg, wu, wd):
    n_rows, w = xs.shape
    n_blocks = n_rows // EXP_BLK
    d, f = wg.shape[1], wg.shape[2]
    cur = lambda b, be, nu, hf: (jnp.minimum(b, nu[0] - 1), 0)
    prev = lambda b, be, nu, hf: (jnp.clip(b - 1, 0, nu[0] - 1), 0)
    wcur = lambda b, be, nu, hf: (be[jnp.minimum(b, nu[0] - 1)], 0, 0)
    wprev = lambda b, be, nu, hf: (be[jnp.clip(b - 1, 0, nu[0] - 1)], 0, 0)
    grid_spec = pltpu.PrefetchScalarGridSpec(
        num_scalar_prefetch=3,
        grid=(n_blocks + 1,),
        in_specs=[pl.BlockSpec((EXP_BLK, w), cur),
                  pl.BlockSpec((1, d, f), wcur),
                  pl.BlockSpec((1, d, f), wcur),
                  pl.BlockSpec((1, f, d), wprev)],
        out_specs=pl.BlockSpec((EXP_BLK, w), prev),
        scratch_shapes=[pltpu.VMEM((EXP_BLK, f), BF16)],
    )
    return pl.pallas_call(
        _expert_kernel,
        out_shape=jax.ShapeDtypeStruct((n_rows, w), U32),
        grid_spec=grid_spec,
        compiler_params=_cparams(("arbitrary",)),
        name="experts",
    )(block_expert, n_used, block_half, xs, wg, wu, wd)


def _combine_kernel(ytok_ref, x1s_ref, w_ref, g2_ref, nf_ref, mf0_ref, mf1_ref, o_ref):
    w = w_ref[...]
    routed = w[:, 0:1] * _unpack_bf16_pair(ytok_ref[0])
    for k in range(1, TOP_K):
        routed = routed + w[:, k:k + 1] * _unpack_bf16_pair(ytok_ref[k])
    xo = x1s_ref[...] + g2_ref[0] * routed
    var = jnp.mean(xo * xo, axis=-1, keepdims=True)
    y = xo * lax.rsqrt(var + EPS) * nf_ref[...]
    o_ref[...] = y * (1.0 + mf1_ref[0]) + mf0_ref[0]


def _combine_alias_kernel(prev_ref, *refs):
    del prev_ref
    _combine_kernel(*refs)


def _combine(ytok, x1s, wts_t, mod3, normf_g, modf3, seq_len, part, out_prev):
    t, d = x1s.shape
    tm = COMB_TM
    steps = ytok.shape[1] // tm
    i0 = part * steps
    per_b = seq_len // tm
    row = lambda i: (i0 + i, 0)
    in_specs = [pl.BlockSpec((TOP_K, tm, d // 2), lambda i: (0, i, 0)),
                pl.BlockSpec((tm, d), row),
                pl.BlockSpec((tm, TOP_K), row),
                pl.BlockSpec((1, 1, d), lambda i: (((i0 + i) // per_b) * N_MOD + 5, 0, 0)),
                pl.BlockSpec((1, d), lambda i: (0, 0)),
                pl.BlockSpec((1, 1, d), lambda i: (((i0 + i) // per_b) * 2 + 0, 0, 0)),
                pl.BlockSpec((1, 1, d), lambda i: (((i0 + i) // per_b) * 2 + 1, 0, 0))]
    args = (ytok, x1s, wts_t, mod3, normf_g.reshape(1, d), modf3, modf3)
    if out_prev is None:
        body, aliases = _combine_kernel, {}
    else:
        body, aliases = _combine_alias_kernel, {0: 0}
        in_specs = [pl.BlockSpec(memory_space=pl.ANY)] + in_specs
        args = (out_prev,) + args
    return pl.pallas_call(
        body,
        out_shape=jax.ShapeDtypeStruct((t, d), F32),
        grid=(steps,),
        in_specs=in_specs,
        out_specs=pl.BlockSpec((tm, d), row),
        input_output_aliases=aliases,
        compiler_params=_cparams(("arbitrary",)),
        name="combine",
    )(*args)


def _lane_row(vec, lane0):
    return jnp.zeros((1, SMALL_W), F32).at[0, lane0:lane0 + vec.shape[0]].set(vec.astype(F32))


def _layer(x2, mod3, bsz, seq_len, norm1_g, w_in, fg_bias, conv_w, conv_b, dt_bias, a_log, d_skip,
           attn_norm_g, ssd_norm_g, w_out, norm2_g, w_router, router_bias,
           w_gate_e, w_up_e, w_down_e, w_gate_s, w_up_s, w_down_s):
    t, d = x2.shape
    o_q, o_k, o_v, o_fg = 0, ATTN_DIM, 2 * ATTN_DIM, 3 * ATTN_DIM
    o_z = o_fg + ATTN_HEADS
    o_xbc = o_z + SSD_DIM
    o_dt = o_xbc + SSD_CONV_DIM
    small = jnp.zeros((d, SMALL_W), F32)
    small = small.at[:, FG_LANE0:FG_LANE0 + ATTN_HEADS].set(w_in[:, o_fg:o_z])
    small = small.at[:, DT_LANE0:DT_LANE0 + SSD_HEADS].set(w_in[:, o_dt:o_dt + SSD_HEADS])
    w_cat = jnp.concatenate([w_in[:, o_q:o_fg], w_in[:, o_z:o_dt], small], axis=1).astype(BF16)

    tri_in = jnp.tril(jnp.ones((IN_SUB, IN_SUB), F32)).astype(BF16)
    qa, ka, va, z2, xbc2, sm2 = _in_proj(x2, mod3, norm1_g, w_cat, _lane_row(fg_bias, FG_LANE0), tri_in,
                                         bsz, seq_len)
    shp = lambda a: a.reshape(bsz, seq_len, a.shape[-1])
    attn3 = _fox_attn(qa, ka, va, attn_norm_g.reshape(1, ATTN_DIM).astype(F32))

    tri_chunk = jnp.tril(jnp.ones((SSD_CHUNK, SSD_CHUNK), F32)).astype(BF16)
    head_of_lane = jnp.arange(SSD_DIM, dtype=I32) // SSD_HEAD_DIM
    expand = (jnp.arange(SMALL_W, dtype=I32)[:, None] == head_of_lane[None, :] + DT_LANE0).astype(BF16)
    dskip_row = jnp.repeat(d_skip.astype(F32), SSD_HEAD_DIM).reshape(1, SSD_DIM)
    ssd3 = _ssd(shp(xbc2), shp(z2), shp(sm2), conv_w.astype(F32), conv_b.reshape(1, -1).astype(F32),
                _lane_row(dt_bias, DT_LANE0), _lane_row(a_log, DT_LANE0), dskip_row,
                ssd_norm_g.reshape(1, SSD_DIM).astype(F32), tri_chunk, expand)

    upper = jnp.triu(jnp.ones((POST_TM, POST_TM), F32), 1).astype(BF16)
    x1s, h2p, eidx, wts, rank, cnt = _post_mixer(
        attn3.reshape(t, ATTN_DIM), ssd3.reshape(t, SSD_DIM), x2, mod3,
        w_out[:ATTN_DIM].astype(BF16), w_out[ATTN_DIM:].astype(BF16), norm2_g,
        w_router.T.astype(BF16), router_bias.reshape(N_EXPERTS, 1).astype(F32),
        w_gate_s.astype(BF16), w_up_s.astype(BF16), w_down_s.astype(BF16), upper, seq_len)

    counts = cnt[:, 0].astype(I32)
    padded = (counts + EXP_BLK - 1) // EXP_BLK * EXP_BLK
    pend = jnp.cumsum(padded)
    pstart = pend - padded
    n_blocks = (t * TOP_K) // EXP_BLK + N_EXPERTS
    block_start = jnp.arange(n_blocks, dtype=I32) * EXP_BLK
    block_expert = jnp.minimum(jnp.sum((pend[None, :] <= block_start[:, None]).astype(I32), axis=1),
                               N_EXPERTS - 1)
    n_used = (pend[-1] // EXP_BLK).astype(I32).reshape(1)
    row_end = (pstart + counts)[block_expert]
    block_half = (row_end - block_start <= EXP_BLK // 2).astype(I32)
    dest = _dest_rows(eidx, rank, pstart)

    xs = _dispatch(dest, h2p, n_blocks * EXP_BLK)
    ys = _experts(block_expert, n_used, block_half, xs, w_gate_e, w_up_e, w_down_e)
    return dest, ys, x1s, wts


def kernel(x, c, norm1_g, w_ada, b_ada, w_in, fg_bias, conv_w, conv_b, dt_bias, a_log, d_skip, attn_norm_g,
           ssd_norm_g, w_out, norm2_g, w_router, router_bias, w_gate_e, w_up_e, w_down_e, w_gate_s, w_up_s,
           w_down_s, normf_g, w_ada_f, b_ada_f):
    bsz, seq_len, d = x.shape
    assert w_ada.shape[0] == 1, "single-layer kernel"
    t = bsz * seq_len
    mod3 = _modulation(c, w_ada[0], b_ada[0]).reshape(bsz * N_MOD, 1, d)
    modf3 = _modulation(c, w_ada_f, b_ada_f).reshape(bsz * 2, 1, d)
    x2 = x.reshape(t, d)
    dest, ys, x1s, wts = _layer(
        x2, mod3, bsz, seq_len, norm1_g[0], w_in[0], fg_bias[0], conv_w[0], conv_b[0], dt_bias[0], a_log[0],
        d_skip[0], attn_norm_g[0], ssd_norm_g[0], w_out[0], norm2_g[0], w_router[0], router_bias[0],
        w_gate_e[0], w_up_e[0], w_down_e[0], w_gate_s[0], w_up_s[0], w_down_s[0])
    wts_t = wts.T
    tp = t // COMB_PARTS
    out = None
    for p in range(COMB_PARTS):
        ytok = _undispatch(dest, ys, p * tp, tp)
        out = _combine(ytok, x1s, wts_t, mod3, normf_g, modf3, seq_len, p, out)
    return out.reshape(bsz, seq_len, d)
```
